```python
import math
import jax, jax.numpy as jnp
from jax import lax
import numpy as np

D_MODEL = 1024
BATCH = 4
SEQ = 4096
DEPTH = 2
DEC_BATCH = 32
DEC_SEQ = 4
PAST_LEN = 8192
PAGE_SIZE = 128

HA_HEADS = 4
HA_DK = 128
HA_DV = 128
HB_HEADS = 8
HB_DIM = 64
HC_HEADS = 4
HC_DK = 128
HC_DV = 128
CONV_W = 4
PEER_HEADS = 8
PEER_TOPK = 16
N_KEYS = 128
N_EXPERTS = N_KEYS * N_KEYS
PEER_QDIM = 256
LIN_CHUNK = 64
Q_BLOCK = 128
PEER_BLOCK = 128
EPS = 1e-6

A_W = HA_HEADS * HA_DK
A_V = HA_HEADS * HA_DV
B_W = HB_HEADS * HB_DIM
C_K = HC_HEADS * HC_DK
C_V = HC_HEADS * HC_DV
CONV_CH = 2 * C_K + C_V
IN_WIDTH = 2 * A_W + 2 * A_V + 3 * B_W + HB_HEADS + 2 * C_K + 2 * C_V + 2 * HC_HEADS + 3 * D_MODEL

kernel_name = "hybrid_hgrn2_fox_gdn_peer_adaln_step"


def _rms(x, g):
    xf = x.astype(jnp.float32)
    r = lax.rsqrt(jnp.mean(xf * xf, axis=-1, keepdims=True) + EPS)
    return (xf * r).astype(x.dtype) * g


def _l2n(x):
    return x * lax.rsqrt(jnp.sum(x * x, axis=-1, keepdims=True) + EPS)


def _heads(t, h):
    return t.reshape(t.shape[0], t.shape[1], h, -1)


def _split_cols(z):
    widths = (A_W, A_W, A_V, A_V, B_W, B_W, B_W, HB_HEADS,
              C_K, C_K, C_V, HC_HEADS, HC_HEADS, C_V, D_MODEL, D_MODEL, D_MODEL)
    out = []
    start = 0
    for w in widths:
        out.append(z[..., start:start + w])
        start += w
    return out


def _hgrn_lower_bounds(logits):
    cs = jnp.cumsum(jax.nn.softmax(logits.astype(jnp.float32), axis=0), axis=0)
    return jnp.maximum(cs - cs[0:1], 0.0)


def _hgrn2_chunked(q, logf, k, v, s0):
    b, L, h, dk = q.shape
    dv = v.shape[-1]
    C = LIN_CHUNK if L % LIN_CHUNK == 0 else L
    n = L // C

    def to_chunks(t):
        return t.reshape(b, n, C, h, t.shape[-1]).transpose(1, 0, 3, 2, 4)

    causal = jnp.tril(jnp.ones((C, C), dtype=bool))

    def step(S, inp):
        qc, gc, kc, vc = inp
        A = jnp.cumsum(gc, axis=2)
        o_inter = jnp.einsum("bhtk,bhkv->bhtv", qc * jnp.exp(A), S)
        rel = A[:, :, :, None, :] - A[:, :, None, :, :]
        dec = jnp.exp(jnp.where(causal[:, :, None], rel, -jnp.inf))
        att = jnp.einsum("bhtk,bhtsk,bhsk->bhts", qc, dec, kc)
        o = o_inter + jnp.einsum("bhts,bhsv->bhtv", att, vc)
        a_last = A[:, :, -1]
        S_new = jnp.exp(a_last)[..., None] * S + jnp.einsum(
            "bhsk,bhsv->bhkv", kc * jnp.exp(a_last[:, :, None, :] - A), vc)
        return S_new, o

    S_fin, o = lax.scan(step, s0.astype(jnp.float32),
                        (to_chunks(q), to_chunks(logf), to_chunks(k), to_chunks(v)))
    return o.transpose(1, 0, 3, 2, 4).reshape(b, L, h, dv), S_fin


def _gdn_chunked(q, k, v, g, beta, s0):
    b, L, h, dk = q.shape
    dv = v.shape[-1]
    C = LIN_CHUNK if L % LIN_CHUNK == 0 else L
    n = L // C

    def chunks4(t):
        return t.reshape(b, n, C, h, t.shape[-1]).transpose(1, 0, 3, 2, 4)

    def chunks3(t):
        return t.reshape(b, n, C, h).transpose(1, 0, 3, 2)

    causal = jnp.tril(jnp.ones((C, C), dtype=bool))
    strict = jnp.tril(jnp.ones((C, C), dtype=bool), -1)
    eye = jnp.eye(C, dtype=jnp.float32)

    def step(S, inp):
        qc, kc, vc, gc, bc = inp
        G = jnp.cumsum(gc, axis=-1)
        rel = G[..., :, None] - G[..., None, :]
        d_strict = jnp.exp(jnp.where(strict, rel, -jnp.inf))
        d_causal = jnp.exp(jnp.where(causal, rel, -jnp.inf))
        eG = jnp.exp(G)[..., None]
        lmat = bc[..., None] * jnp.einsum("bhtk,bhsk->bhts", kc, kc) * d_strict
        rhs = bc[..., None] * (vc - eG * jnp.einsum("bhtk,bhkv->bhtv", kc, S))
        u = lax.linalg.triangular_solve(eye + lmat, rhs, left_side=True, lower=True)
        qk = jnp.einsum("bhtk,bhsk->bhts", qc, kc) * d_causal
        o = eG * jnp.einsum("bhtk,bhkv->bhtv", qc, S) + jnp.einsum("bhts,bhsv->bhtv", qk, u)
        g_last = G[..., -1:]
        S_new = jnp.exp(g_last)[..., None] * S + jnp.einsum(
            "bhsk,bhsv->bhkv", kc * jnp.exp(g_last - G)[..., None], u)
        return S_new, o

    S_fin, o = lax.scan(step, s0.astype(jnp.float32),
                        (chunks4(q), chunks4(k), chunks4(v), chunks3(g), chunks3(beta)))
    return o.transpose(1, 0, 3, 2, 4).reshape(b, L, h, dv), S_fin


def _causal_conv(x, buf, w):
    xp = jnp.concatenate([buf.astype(x.dtype), x], axis=1)
    L = x.shape[1]
    y = xp[:, 0:L] * w[0]
    for j in range(1, CONV_W):
        y = y + xp[:, j:j + L] * w[j]
    return jax.nn.silu(y), xp[:, -(CONV_W - 1):]


def _fox_attend(q, Fq, pos_q, k, v, Fk, pos_k):
    s = jnp.einsum("bthd,bshd->bhts", q, k).astype(jnp.float32) * (HB_DIM ** -0.5)
    s = s + (Fq[..., :, None] - Fk[..., None, :])
    s = jnp.where(pos_k[None, :] <= pos_q[:, None], s, -jnp.inf)
    p = jax.nn.softmax(s, axis=-1)
    return jnp.einsum("bhts,bshd->bthd", p.astype(v.dtype), v)


def _fox_prompt(q, k, v, logf):
    b, S, h, d = q.shape
    F = jnp.cumsum(logf, axis=1).transpose(0, 2, 1)
    nb = S // Q_BLOCK
    pos_k = jnp.arange(S)
    qb = q.reshape(b, nb, Q_BLOCK, h, d).transpose(1, 0, 2, 3, 4)
    Fb = F.reshape(b, h, nb, Q_BLOCK).transpose(2, 0, 1, 3)

    def blk(args):
        i, qi, Fi = args
        return _fox_attend(qi, Fi, i * Q_BLOCK + jnp.arange(Q_BLOCK), k, v, F, pos_k)

    o = lax.map(blk, (jnp.arange(nb), qb, Fb))
    return o.transpose(1, 0, 2, 3, 4).reshape(b, S, h * d)


def _fox_sample(q, k, v, logf, k_past, v_past, logf_past):
    b, T, h, d = q.shape
    P = k_past.shape[1]
    kk = jnp.concatenate([k_past.astype(k.dtype), k], axis=1)
    vv = jnp.concatenate([v_past.astype(v.dtype), v], axis=1)
    lf = jnp.concatenate([logf_past.astype(jnp.float32), logf], axis=1)
    F = jnp.cumsum(lf, axis=1).transpose(0, 2, 1)
    o = _fox_attend(q, F[..., P:], P + jnp.arange(T), kk, vv, F, jnp.arange(P + T))
    return o.reshape(b, T, h * d)


def _mixers(l, h, p, lb, fox_fn, s_hgrn, s_gdn, conv_buf):
    b, L, _ = h.shape
    f32 = jnp.float32
    z = h @ p["w_in"][l]
    (fa, qa, ia, ga, qb, kb, vb, fb, qc, kc, vc, ac, bc, zc,
     gate_a, gate_b, gate_c) = _split_cols(z)

    lb_h = lb.reshape(HA_HEADS, HA_DK)
    log_f = jnp.logaddexp(jnp.log(lb_h), jnp.log1p(-lb_h) + jax.nn.log_sigmoid(_heads(fa, HA_HEADS).astype(f32)))
    k_in = -jnp.expm1(log_f)
    o_a, s_hgrn_new = _hgrn2_chunked(jax.nn.silu(_heads(qa, HA_HEADS).astype(f32)), log_f, k_in,
                                     _heads(ia, HA_HEADS).astype(f32), s_hgrn)
    o_a = _rms(o_a, p["hgrn_norm"][l]) * jax.nn.silu(_heads(ga, HA_HEADS).astype(f32))
    y_a = o_a.reshape(b, L, A_V).astype(h.dtype) @ p["w_br_a"][l]

    q_b = _rms(_heads(qb, HB_HEADS), p["fox_q_norm"][l])
    k_b = _rms(_heads(kb, HB_HEADS), p["fox_k_norm"][l])
    v_b = _heads(vb, HB_HEADS)
    logf_b = jax.nn.log_sigmoid((fb + p["fox_b_f"][l]).astype(f32))
    y_b = fox_fn(l, q_b, k_b, v_b, logf_b) @ p["w_br_b"][l]

    qkv, conv_new = _causal_conv(jnp.concatenate([qc, kc, vc], axis=-1), conv_buf, p["gdn_conv_w"][l])
    q_c = _l2n(_heads(qkv[..., :C_K], HC_HEADS).astype(f32)) * (HC_DK ** -0.5)
    k_c = _l2n(_heads(qkv[..., C_K:2 * C_K], HC_HEADS).astype(f32))
    v_c = _heads(qkv[..., 2 * C_K:], HC_HEADS).astype(f32)
    g_c = -jnp.exp(p["gdn_a_log"][l].astype(f32)) * jax.nn.softplus((ac + p["gdn_dt_bias"][l]).astype(f32))
    beta = jax.nn.sigmoid(bc.astype(f32))
    o_c, s_gdn_new = _gdn_chunked(q_c, k_c, v_c, g_c, beta, s_gdn)
    o_c = _rms(o_c, p["gdn_norm"][l]) * jax.nn.silu(_heads(zc, HC_HEADS).astype(f32))
    y_c = o_c.reshape(b, L, C_V).astype(h.dtype) @ p["w_br_c"][l]

    merged = jax.nn.sigmoid(gate_a) * y_a + jax.nn.sigmoid(gate_b) * y_b + jax.nn.sigmoid(gate_c) * y_c
    return merged @ p["w_out"][l], (k_b, v_b, logf_b, s_hgrn_new, s_gdn_new, conv_new)


def _peer(x, w_q, keys, u_tab, v_tab):
    t_all = x.shape[0]
    blk = PEER_BLOCK if t_all % PEER_BLOCK == 0 else t_all

    def run(xb):
        q = (xb @ w_q).reshape(blk, PEER_HEADS, 2, PEER_QDIM // 2)
        s = jnp.einsum("thpd,pnd->thpn", q, keys).astype(jnp.float32)
        sv, si = lax.top_k(s, PEER_TOPK)
        cand = sv[:, :, 0, :, None] + sv[:, :, 1, None, :]
        cidx = si[:, :, 0, :, None] * N_KEYS + si[:, :, 1, None, :]
        best, pos = lax.top_k(cand.reshape(blk, PEER_HEADS, PEER_TOPK * PEER_TOPK), PEER_TOPK)
        eidx = jnp.take_along_axis(cidx.reshape(blk, PEER_HEADS, PEER_TOPK * PEER_TOPK), pos, axis=-1)
        g = jax.nn.softmax(best, axis=-1)
        act = jax.nn.gelu(jnp.einsum("thkd,td->thk", u_tab[eidx], xb).astype(jnp.float32))
        return jnp.einsum("thk,thkd->td", (g * act).astype(xb.dtype), v_tab[eidx])

    y = lax.map(run, x.reshape(t_all // blk, blk, D_MODEL))
    return y.reshape(t_all, D_MODEL)


def _trunk(x, c, fox_fn, s_hgrn, s_gdn, s_conv, p):
    b, L, _ = x.shape
    lbs = _hgrn_lower_bounds(p["hgrn_lb_logits"])
    outs = [[] for _ in range(6)]
    for l in range(DEPTH):
        mod = (jax.nn.silu(c) @ p["w_ada"][l] + p["b_ada"][l])[:, None, :]
        sh1, sc1, gt1, sh2, sc2, gt2 = jnp.split(mod, 6, axis=-1)
        h = _rms(x, p["norm_mix"][l]) * (1.0 + sc1) + sh1
        mix, st = _mixers(l, h, p, lbs[l], fox_fn, s_hgrn[l], s_gdn[l], s_conv[l])
        x = x + gt1 * mix
        h2 = _rms(x, p["norm_ffn"][l]) * (1.0 + sc2) + sh2
        ffn = _peer(h2.reshape(b * L, D_MODEL), p["peer_w_q"][l], p["peer_keys"][l],
                    p["peer_u"][l], p["peer_v"][l])
        x = x + gt2 * ffn.reshape(b, L, D_MODEL)
        for lst, s in zip(outs, st):
            lst.append(s)
    return x, [jnp.stack(o) for o in outs]


def setup_inputs(seed: int = 0) -> dict:
    key = jax.random.key(seed)
    k = jax.random.split(key, 32)
    f32 = jnp.float32
    n_pages = PAST_LEN // PAGE_SIZE
    used = DEC_BATCH * n_pages
    n_pool = used + max(1, used // 4)

    def nrm(i, shape, scale=1.0):
        return jax.random.normal(k[i], shape, f32) * scale

    def gain(i, shape):
        return 1.0 + 0.05 * jax.random.normal(k[i], shape, f32)

    dt = jnp.exp(jax.random.uniform(k[24], (DEPTH, HC_HEADS), f32, math.log(1e-3), math.log(1e-1)))
    return {
        "x_prompt": nrm(0, (BATCH, SEQ, D_MODEL)),
        "x_sample": nrm(1, (DEC_BATCH, DEC_SEQ, D_MODEL)),
        "c_prompt": nrm(2, (BATCH, D_MODEL)),
        "c_sample": nrm(3, (DEC_BATCH, D_MODEL)),
        "cache_fox_k": nrm(4, (DEPTH, n_pool, PAGE_SIZE, HB_HEADS, HB_DIM)),
        "cache_fox_v": nrm(5, (DEPTH, n_pool, PAGE_SIZE, HB_HEADS, HB_DIM)),
        "cache_fox_logf": jax.nn.log_sigmoid(4.0 + 0.5 * jax.random.normal(k[6], (DEPTH, n_pool, PAGE_SIZE, HB_HEADS), f32)),
        "page_table": jax.random.permutation(k[7], n_pool)[:used].reshape(DEC_BATCH, n_pages).astype(jnp.int32),
        "state_hgrn": nrm(8, (DEPTH, DEC_BATCH, HA_HEADS, HA_DK, HA_DV), 0.5),
        "state_gdn": nrm(9, (DEPTH, DEC_BATCH, HC_HEADS, HC_DK, HC_DV), 0.1),
        "state_gdn_conv": nrm(10, (DEPTH, DEC_BATCH, CONV_W - 1, CONV_CH)),
        "w_ada": nrm(11, (DEPTH, D_MODEL, 6 * D_MODEL), 0.2 * D_MODEL ** -0.5),
        "b_ada": nrm(12, (DEPTH, 6 * D_MODEL), 0.02),
        "norm_mix": gain(13, (DEPTH, D_MODEL)),
        "norm_ffn": gain(14, (DEPTH, D_MODEL)),
        "w_in": nrm(15, (DEPTH, D_MODEL, IN_WIDTH), D_MODEL ** -0.5),
        "hgrn_lb_logits": nrm(16, (DEPTH, A_W), 0.5),
        "hgrn_norm": gain(17, (DEPTH, HA_DV)),
        "fox_b_f": 3.0 + nrm(18, (DEPTH, HB_HEADS), 0.5),
        "fox_q_norm": gain(19, (DEPTH, HB_DIM)),
        "fox_k_norm": gain(20, (DEPTH, HB_DIM)),
        "gdn_conv_w": nrm(21, (DEPTH, CONV_W, CONV_CH), CONV_W ** -0.5),
        "gdn_a_log": jnp.log(jax.random.uniform(k[22], (DEPTH, HC_HEADS), f32, 1.0, 16.0)),
        "gdn_dt_bias": dt + jnp.log(-jnp.expm1(-dt)),
        "gdn_norm": gain(23, (DEPTH, HC_DV)),
        "w_br_a": nrm(25, (DEPTH, A_V, D_MODEL), A_V ** -0.5),
        "w_br_b": nrm(26, (DEPTH, B_W, D_MODEL), B_W ** -0.5),
        "w_br_c": nrm(27, (DEPTH, C_V, D_MODEL), C_V ** -0.5),
        "w_out": nrm(28, (DEPTH, D_MODEL, D_MODEL), D_MODEL ** -0.5),
        "peer_w_q": nrm(29, (DEPTH, D_MODEL, PEER_HEADS * PEER_QDIM), D_MODEL ** -0.5),
        "peer_keys": nrm(30, (DEPTH, 2, N_KEYS, PEER_QDIM // 2), (PEER_QDIM // 2) ** -0.5),
        "peer_u": nrm(31, (DEPTH, N_EXPERTS, D_MODEL), D_MODEL ** -0.5),
        "peer_v": jax.random.normal(jax.random.fold_in(key, 99), (DEPTH, N_EXPERTS, D_MODEL), f32),
    }


def reference(x_prompt, x_sample, c_prompt, c_sample, cache_fox_k, cache_fox_v, cache_fox_logf, page_table,
              state_hgrn, state_gdn, state_gdn_conv, w_ada, b_ada, norm_mix, norm_ffn, w_in, hgrn_lb_logits,
              hgrn_norm, fox_b_f, fox_q_norm, fox_k_norm, gdn_conv_w, gdn_a_log, gdn_dt_bias, gdn_norm,
              w_br_a, w_br_b, w_br_c, w_out, peer_w_q, peer_keys, peer_u, peer_v):
    p = {"w_ada": w_ada, "b_ada": b_ada, "norm_mix": norm_mix, "norm_ffn": norm_ffn, "w_in": w_in,
         "hgrn_lb_logits": hgrn_lb_logits, "hgrn_norm": hgrn_norm, "fox_b_f": fox_b_f,
         "fox_q_norm": fox_q_norm, "fox_k_norm": fox_k_norm, "gdn_conv_w": gdn_conv_w,
         "gdn_a_log": gdn_a_log, "gdn_dt_bias": gdn_dt_bias, "gdn_norm": gdn_norm,
         "w_br_a": w_br_a, "w_br_b": w_br_b, "w_br_c": w_br_c, "w_out": w_out,
         "peer_w_q": peer_w_q, "peer_keys": peer_keys, "peer_u": peer_u, "peer_v": peer_v}
    bp = x_prompt.shape[0]
    db, n_pages = page_table.shape

    hgrn0 = jnp.zeros((DEPTH, bp, HA_HEADS, HA_DK, HA_DV), jnp.float32)
    gdn0 = jnp.zeros((DEPTH, bp, HC_HEADS, HC_DK, HC_DV), jnp.float32)
    conv0 = jnp.zeros((DEPTH, bp, CONV_W - 1, CONV_CH), x_prompt.dtype)

    def fox_prompt_fn(l, q, k, v, lf):
        return _fox_prompt(q, k, v, lf)

    def fox_sample_fn(l, q, k, v, lf):
        kp = cache_fox_k[l][page_table].reshape(db, n_pages * PAGE_SIZE, HB_HEADS, HB_DIM)
        vp = cache_fox_v[l][page_table].reshape(db, n_pages * PAGE_SIZE, HB_HEADS, HB_DIM)
        lp = cache_fox_logf[l][page_table].reshape(db, n_pages * PAGE_SIZE, HB_HEADS)
        return _fox_sample(q, k, v, lf, kp, vp, lp)

    y_prompt, (fox_k_p, fox_v_p, fox_lf_p, hgrn_p, gdn_p, conv_p) = _trunk(
        x_prompt, c_prompt, fox_prompt_fn, hgrn0, gdn0, conv0, p)
    y_sample, (fox_k_s, fox_v_s, fox_lf_s, hgrn_s, gdn_s, conv_s) = _trunk(
        x_sample, c_sample, fox_sample_fn, state_hgrn, state_gdn, state_gdn_conv, p)
    return (y_prompt, y_sample, fox_k_p, fox_v_p, fox_lf_p, hgrn_p, gdn_p, conv_p,
            fox_k_s, fox_v_s, fox_lf_s, hgrn_s, gdn_s, conv_s)
```

```python
import functools
import math

import jax
import jax.numpy as jnp
from jax import lax
from jax.experimental import pallas as pl
from jax.experimental.pallas import tpu as pltpu

F32 = jnp.float32
BF16 = jnp.bfloat16
HI = lax.Precision.HIGHEST
NEG_INF = float("-inf")

D = 1024
DEPTH = 2
HA_H, HA_DK, HA_DV = 4, 128, 128
HB_H, HB_D = 8, 64
HC_H, HC_DK, HC_DV = 4, 128, 128
CONV_W = 4
P_H, P_TOPK, P_NK, P_QD = 8, 16, 128, 256
N_EXP = P_NK * P_NK
EPS = 1e-6
A_W = HA_H * HA_DK
B_W = HB_H * HB_D
C_K = HC_H * HC_DK
CONV_CH = 3 * C_K
PAGE = 128

LANES = 128
SUBLANES = 8
VMEM_LIMIT = 52 * 1024 * 1024

OFF_FA, OFF_QA, OFF_IA, OFF_GA = 0, 512, 1024, 1536
OFF_GATE = 2048
OFF_QB, OFF_KB, OFF_VB = 5120, 5632, 6144
OFF_QC, OFF_KC, OFF_VC = 6656, 7168, 7680
OFF_ZC = 8192
OFF_FB = 8704
OFF_AB = 8832
NP = 8960

LIN_C = 64
PAGES_PER_STEP = 8


def _dot(a, b):
    return jnp.dot(a, b, preferred_element_type=F32)


def _dot_hi(a, b):
    return jnp.dot(a, b, preferred_element_type=F32, precision=HI)


def _dot_nt(a, b):
    return lax.dot_general(a, b, (((1,), (1,)), ((), ())), preferred_element_type=F32)


def _dot_tn(a, b, precision=None):
    return lax.dot_general(a, b, (((0,), (0,)), ((), ())), preferred_element_type=F32, precision=precision)


def _iota(shape, dim):
    return lax.broadcasted_iota(jnp.int32, shape, dim)


def _sigmoid(x):
    return jax.nn.sigmoid(x)


def _silu(x):
    return x * jax.nn.sigmoid(x)


def _softplus(x):
    return jnp.maximum(x, 0.0) + jnp.log1p(jnp.exp(-jnp.abs(x)))


def _log_sigmoid(x):
    return jnp.minimum(x, 0.0) - jnp.log1p(jnp.exp(-jnp.abs(x)))


def _params(sem):
    return pltpu.CompilerParams(dimension_semantics=sem, vmem_limit_bytes=VMEM_LIMIT)


def _chunk_start(c, c_len):
    return c * c_len if isinstance(c, int) else pl.multiple_of(c * c_len, c_len)


def _for_chunks(n_chunks, body):
    if n_chunks == 1:
        body(0, 0)
    else:
        lax.fori_loop(0, n_chunks, body, 0)


def _tile(n, pref):
    t = min(n, pref)
    assert n % t == 0, (n, pref)
    return t


def _ada_kernel(c_ref, w_ref, b_ref, o_ref):
    sc = _silu(c_ref[...]).astype(BF16)
    o_ref[...] = _dot(sc, w_ref[...].astype(BF16)) + b_ref[...]


def _ada(c_all, w_ada, b_ada):
    rows = c_all.shape[0]
    tn = 768
    return pl.pallas_call(
        _ada_kernel,
        out_shape=jax.ShapeDtypeStruct((DEPTH, rows, 6 * D), F32),
        grid=(DEPTH, 6 * D // tn),
        in_specs=[pl.BlockSpec((rows, D), lambda l, n: (0, 0)),
                  pl.BlockSpec((None, D, tn), lambda l, n: (l, 0, n)),
                  pl.BlockSpec((None, 1, tn), lambda l, n: (l, 0, n))],
        out_specs=pl.BlockSpec((None, rows, tn), lambda l, n: (l, 0, n)),
        compiler_params=_params(("parallel", "parallel")),
        name="ada",
    )(c_all, w_ada, b_ada.reshape(DEPTH, 1, 6 * D))


def _mod_spec(mod, tm, k):
    if mod.shape[1] == 1:
        return pl.BlockSpec((None, 1, D), lambda b, i, *_: (b, 0, k))
    return pl.BlockSpec((None, tm, D), lambda b, i, *_: (b, i, k))


def _in_kernel(x_ref, sh_ref, sc_ref, g_ref, w_ref, z_ref, h_scr):
    @pl.when(pl.program_id(2) == 0)
    def _():
        x = x_ref[...]
        r = lax.rsqrt(jnp.mean(x * x, axis=-1, keepdims=True) + EPS)
        h = (x * r) * g_ref[...] * (1.0 + sc_ref[...]) + sh_ref[...]
        h_scr[...] = h.astype(BF16)

    z_ref[...] = _dot(h_scr[...], w_ref[...])


def _in_proj(x, mod, gain, w_packed):
    bg, lg, _ = x.shape
    tm = _tile(lg, 512)
    tn = 1280
    return pl.pallas_call(
        _in_kernel,
        out_shape=jax.ShapeDtypeStruct((bg, lg, NP), F32),
        grid=(bg, lg // tm, NP // tn),
        in_specs=[pl.BlockSpec((None, tm, D), lambda b, i, n: (b, i, 0)),
                  _mod_spec(mod, tm, 0), _mod_spec(mod, tm, 1),
                  pl.BlockSpec((1, D), lambda b, i, n: (0, 0)),
                  pl.BlockSpec((D, tn), lambda b, i, n: (0, n))],
        out_specs=pl.BlockSpec((None, tm, tn), lambda b, i, n: (b, i, n)),
        scratch_shapes=[pltpu.VMEM((tm, D), BF16)],
        compiler_params=_params(("parallel", "parallel", "arbitrary")),
        name="in_proj",
    )(x, mod, mod, gain.reshape(1, D), w_packed)


def _hgrn_kernel(fa_ref, qa_ref, ia_ref, ga_ref, lbl_ref, nw_ref, s0_ref, o_ref, sout_ref, s_scr,
                 *, layer, chunk, n_chunks, l_valid, l_padded):
    i = pl.program_id(2)
    c_len = chunk

    @pl.when(i == 0)
    def _():
        s_scr[...] = s0_ref[...]

    lg = lbl_ref[...]
    e = jnp.exp(lg - jnp.max(lg, axis=0, keepdims=True))
    p = e / jnp.sum(e, axis=0, keepdims=True)
    cs = p[0:1]
    for j in range(1, layer + 1):
        cs = cs + p[j:j + 1]
    lb = jnp.maximum(cs - p[0:1], 0.0)
    log_lb = jnp.log(lb)
    log1m_lb = jnp.log1p(-lb)

    tri = (_iota((c_len, c_len), 1) <= _iota((c_len, c_len), 0)).astype(F32)
    lane = _iota((c_len, c_len), 1)
    row1 = _iota((c_len, 1), 0)
    ones_cv = jnp.ones((c_len, HA_DV), F32)

    def chunk_body(c, carry):
        r = _chunk_start(c, c_len)
        fa = fa_ref[pl.ds(r, c_len), :]
        qa = qa_ref[pl.ds(r, c_len), :]
        v = ia_ref[pl.ds(r, c_len), :]
        ga = ga_ref[pl.ds(r, c_len), :]

        b_ = log1m_lb + _log_sigmoid(fa)
        log_f = jnp.maximum(log_lb, b_) + jnp.log1p(jnp.exp(-jnp.abs(log_lb - b_)))
        k = (1.0 - lb) * _sigmoid(-fa)
        if l_valid < l_padded:
            valid = (i * (n_chunks * c_len) + r + row1) < l_valid
            log_f = jnp.where(valid, log_f, 0.0)
            k = jnp.where(valid, k, 0.0)
        q = _silu(qa)
        a_cum = _dot_hi(tri, log_f)

        att = jnp.zeros((c_len, c_len), F32)
        for s in range(c_len):
            r0 = (s // SUBLANES) * SUBLANES
            rel = a_cum[r0:] - a_cum[s:s + 1]
            ok = (row1[r0:] >= s)
            dec = jnp.exp(jnp.where(ok, rel, NEG_INF))
            col = jnp.sum(q[r0:] * k[s:s + 1] * dec, axis=-1, keepdims=True)
            if r0 > 0:
                col = jnp.concatenate([jnp.zeros((r0, 1), F32), col], axis=0)
            att = jnp.where(lane == s, col, att)

        s_prev = s_scr[...]
        qd = q * jnp.exp(a_cum)
        o = _dot(qd.astype(BF16), s_prev.astype(BF16)) + _dot(att.astype(BF16), v.astype(BF16))
        a_last = a_cum[c_len - 1:c_len]
        kd = k * jnp.exp(a_last - a_cum)
        dec_s = jnp.exp(_dot_tn(log_f, ones_cv, precision=HI))
        s_scr[...] = dec_s * s_prev + _dot_tn(kd.astype(BF16), v.astype(BF16))

        rr = lax.rsqrt(jnp.mean(o * o, axis=-1, keepdims=True) + EPS)
        o_ref[pl.ds(r, c_len), :] = ((o * rr) * nw_ref[...] * _silu(ga)).astype(o_ref.dtype)
        return carry

    _for_chunks(n_chunks, chunk_body)

    @pl.when(i == pl.num_programs(2) - 1)
    def _():
        sout_ref[...] = s_scr[...]


def _hgrn(z3, lb_logits, norm_w, s0, *, layer, chunk, l_valid):
    bs, lp, _ = z3.shape
    tb = _tile(lp, 4 * chunk)
    kern = functools.partial(_hgrn_kernel, layer=layer, chunk=chunk, n_chunks=tb // chunk,
                             l_valid=l_valid, l_padded=lp)

    def col(off):
        return pl.BlockSpec((None, tb, LANES), lambda b, h, i: (b, i, off // LANES + h))

    return pl.pallas_call(
        kern,
        out_shape=(jax.ShapeDtypeStruct((bs, lp, A_W), BF16),
                   jax.ShapeDtypeStruct((bs, HA_H, HA_DK, HA_DV), F32)),
        grid=(bs, HA_H, lp // tb),
        in_specs=[col(OFF_FA), col(OFF_QA), col(OFF_IA), col(OFF_GA),
                  pl.BlockSpec((DEPTH, LANES), lambda b, h, i: (0, h)),
                  pl.BlockSpec((1, HA_DV), lambda b, h, i: (0, 0)),
                  pl.BlockSpec((None, None, HA_DK, HA_DV), lambda b, h, i: (b, h, 0, 0))],
        out_specs=(pl.BlockSpec((None, tb, LANES), lambda b, h, i: (b, i, h)),
                   pl.BlockSpec((None, None, HA_DK, HA_DV), lambda b, h, i: (b, h, 0, 0))),
        scratch_shapes=[pltpu.VMEM((HA_DK, HA_DV), F32)],
        compiler_params=_params(("parallel", "parallel", "arbitrary")),
        name="hgrn2",
    )(z3, z3, z3, z3, lb_logits, norm_w.reshape(1, HA_DV), s0)


def _gdn_kernel(q_ref, k_ref, v_ref, zc_ref, ab_ref, cw_ref, alog_ref, dt_ref, nw_ref, conv0_ref, s0_ref,
                o_ref, sout_ref, s_scr, prev_scr, act_scr, gb_scr, x_scr,
                *, chunk, n_chunks, l_valid, l_padded):
    i = pl.program_id(1)
    c_len = chunk
    tb = n_chunks * c_len

    @pl.when(i == 0)
    def _():
        s_scr[...] = s0_ref[...]
        prev_scr[...] = conv0_ref[...]

    row8 = _iota((SUBLANES, C_K), 0)

    def conv(x, prev, w):
        y = x * w[CONV_W - 1:CONV_W]
        for j in range(1, CONV_W):
            xr = pltpu.roll(x, j, 0)
            head = jnp.where(row8 < j, pltpu.roll(prev, j, 0), xr[:SUBLANES])
            xs = head if tb == SUBLANES else jnp.concatenate([head, xr[SUBLANES:]], axis=0)
            y = y + xs * w[CONV_W - 1 - j:CONV_W - j]
        return _silu(y)

    for n, ref in enumerate((q_ref, k_ref, v_ref)):
        x = ref[...]
        lo, hi = n * C_K, (n + 1) * C_K
        act_scr[:, lo:hi] = conv(x, prev_scr[:, lo:hi], cw_ref[:, lo:hi])
        prev_scr[:, lo:hi] = x[tb - SUBLANES:]

    ab = ab_ref[...]
    g_all = -jnp.exp(alog_ref[...]) * _softplus(ab + dt_ref[...])
    b_all = _sigmoid(ab)
    if l_valid < l_padded:
        valid = (i * tb + _iota((tb, 1), 0)) < l_valid
        g_all = jnp.where(valid, g_all, 0.0)
        b_all = jnp.where(valid, b_all, 0.0)
    gb_scr[:, 0:LANES] = g_all
    gb_scr[:, LANES:2 * LANES] = b_all

    ii = _iota((c_len, c_len), 0)
    jj = _iota((c_len, c_len), 1)
    tri = (jj <= ii).astype(F32)
    tri_u = (ii <= jj).astype(F32)

    def chunk_body(c, carry):
        r = _chunk_start(c, c_len)
        heads = []
        for h in range(HC_H):
            lo, hi = h * HC_DK, (h + 1) * HC_DK
            qh = act_scr[pl.ds(r, c_len), lo:hi]
            kh = act_scr[pl.ds(r, c_len), C_K + lo:C_K + hi]
            vh = act_scr[pl.ds(r, c_len), 2 * C_K + lo:2 * C_K + hi]
            qh = qh * lax.rsqrt(jnp.sum(qh * qh, axis=-1, keepdims=True) + EPS) * (HC_DK ** -0.5)
            kh = kh * lax.rsqrt(jnp.sum(kh * kh, axis=-1, keepdims=True) + EPS)
            g_col = gb_scr[pl.ds(r, c_len), h:h + 1]
            b_col = gb_scr[pl.ds(r, c_len), LANES + HC_H + h:LANES + HC_H + h + 1]
            g_b = jnp.broadcast_to(g_col, (c_len, LANES))
            g_cum = _dot_hi(tri, g_b)
            g_row = _dot_tn(g_b, tri_u, precision=HI)[:c_len]
            rel = g_cum[:, :c_len] - g_row
            d_causal = jnp.exp(jnp.where(jj <= ii, rel, NEG_INF))
            d_strict_t = jnp.exp(jnp.where(jj > ii, -rel, NEG_INF))
            kb = (b_col * kh).astype(BF16)
            khb = kh.astype(BF16)
            l_t = _dot_nt(khb, kb) * d_strict_t
            qk = _dot_nt(qh.astype(BF16), khb) * d_causal
            e_g = jnp.exp(g_cum)
            x_scr[h, :, 0:HC_DK] = b_col * e_g * kh
            x_scr[h, :, HC_DK:] = b_col * vh
            heads.append((qh, kh, l_t, qk, e_g, g_cum))

        for t in range(1, c_len):
            r1 = ((t + SUBLANES - 1) // SUBLANES) * SUBLANES
            for h in range(HC_H):
                l_t = heads[h][2]
                contrib = jnp.sum(x_scr[h, 0:r1, :] * l_t[0:r1, t:t + 1], axis=0, keepdims=True)
                x_scr[h, t:t + 1, :] = x_scr[h, t:t + 1, :] - contrib

        for h in range(HC_H):
            qh, kh, _, qk, e_g, g_cum = heads[h]
            lo, hi = h * HC_DV, (h + 1) * HC_DV
            xs = x_scr[h]
            s_prev = s_scr[h]
            s_b = s_prev.astype(BF16)
            u = xs[:, HC_DK:] - _dot(xs[:, :HC_DK].astype(BF16), s_b)
            u_b = u.astype(BF16)
            o = e_g * _dot(qh.astype(BF16), s_b) + _dot(qk.astype(BF16), u_b)
            g_last = g_cum[c_len - 1:c_len]
            kd = kh * jnp.exp(g_last - g_cum)
            s_scr[h] = jnp.exp(g_last) * s_prev + _dot_tn(kd.astype(BF16), u_b)
            rr = lax.rsqrt(jnp.mean(o * o, axis=-1, keepdims=True) + EPS)
            zc = zc_ref[pl.ds(r, c_len), lo:hi]
            o_ref[pl.ds(r, c_len), lo:hi] = ((o * rr) * nw_ref[...] * _silu(zc)).astype(o_ref.dtype)
        return carry

    _for_chunks(n_chunks, chunk_body)

    @pl.when(i == pl.num_programs(1) - 1)
    def _():
        sout_ref[...] = s_scr[...]


def _gdn(z3, conv_w, a_log, dt_bias, norm_w, conv0, s0, *, chunk, l_valid):
    bs, lp, _ = z3.shape
    tb = _tile(lp, 4 * chunk)
    kern = functools.partial(_gdn_kernel, chunk=chunk, n_chunks=tb // chunk, l_valid=l_valid, l_padded=lp)
    pad = jnp.zeros((LANES - HC_H,), F32)
    alog_row = jnp.concatenate([a_log, pad]).reshape(1, LANES)
    dt_row = jnp.concatenate([dt_bias, pad]).reshape(1, LANES)

    def wide(off):
        return pl.BlockSpec((None, tb, C_K), lambda b, i: (b, i, off // C_K))

    return pl.pallas_call(
        kern,
        out_shape=(jax.ShapeDtypeStruct((bs, lp, C_K), BF16),
                   jax.ShapeDtypeStruct((bs, HC_H, HC_DK, HC_DV), F32)),
        grid=(bs, lp // tb),
        in_specs=[wide(OFF_QC), wide(OFF_KC), wide(OFF_VC), wide(OFF_ZC),
                  pl.BlockSpec((None, tb, LANES), lambda b, i: (b, i, OFF_AB // LANES)),
                  pl.BlockSpec((CONV_W, CONV_CH), lambda b, i: (0, 0)),
                  pl.BlockSpec((1, LANES), lambda b, i: (0, 0)),
                  pl.BlockSpec((1, LANES), lambda b, i: (0, 0)),
                  pl.BlockSpec((1, HC_DV), lambda b, i: (0, 0)),
                  pl.BlockSpec((None, SUBLANES, CONV_CH), lambda b, i: (b, 0, 0)),
                  pl.BlockSpec((None, HC_H, HC_DK, HC_DV), lambda b, i: (b, 0, 0, 0))],
        out_specs=(pl.BlockSpec((None, tb, C_K), lambda b, i: (b, i, 0)),
                   pl.BlockSpec((None, HC_H, HC_DK, HC_DV), lambda b, i: (b, 0, 0, 0))),
        scratch_shapes=[pltpu.VMEM((HC_H, HC_DK, HC_DV), F32),
                        pltpu.VMEM((SUBLANES, CONV_CH), F32),
                        pltpu.VMEM((tb, CONV_CH), F32),
                        pltpu.VMEM((tb, 2 * LANES), F32),
                        pltpu.VMEM((HC_H, chunk, HC_DK + HC_DV), F32)],
        compiler_params=_params(("parallel", "arbitrary")),
        name="gdn",
    )(z3, z3, z3, z3, z3, conv_w, alog_row, dt_row, norm_w.reshape(1, HC_DV), conv0, s0)


def _foxprep_kernel(q_ref, k_ref, v_ref, fb_ref, gq_ref, gk_ref, bf_ref, bd_ref,
                    qo_ref, ko_ref, kbo_ref, vbo_ref, lfo_ref, *rest, cumsum):
    bd = bd_ref[...]

    def head_rms(x, g):
        x2 = x * x
        hi = x2.astype(BF16)
        lo = (x2 - hi.astype(F32)).astype(BF16)
        ss = _dot(hi, bd) + _dot(lo, bd)
        return x * lax.rsqrt(ss * (1.0 / HB_D) + EPS) * g

    qn = head_rms(q_ref[...], gq_ref[...])
    kn = head_rms(k_ref[...], gk_ref[...])
    qo_ref[...] = (qn * (HB_D ** -0.5)).astype(BF16)
    ko_ref[...] = kn
    kbo_ref[...] = kn.astype(BF16)
    vbo_ref[...] = v_ref[...].astype(BF16)
    lf = _log_sigmoid(fb_ref[...] + bf_ref[...])
    lf = jnp.where(_iota(lf.shape, 1) < HB_H, lf, 0.0)
    lfo_ref[...] = lf
    if cumsum:
        ft_ref, carry = rest
        tm = lf.shape[0]

        @pl.when(pl.program_id(1) == 0)
        def _():
            carry[...] = jnp.zeros_like(carry)

        tri = (_iota((tm, tm), 1) <= _iota((tm, tm), 0)).astype(F32)
        f_cum = _dot_hi(tri, lf) + carry[...]
        carry[...] = f_cum[tm - 1:tm]
        ft_ref[...] = f_cum.T[:HB_H]


def _fox_prep(z3, gq, gk, b_f, *, cumsum):
    bg, lg, _ = z3.shape
    tm = _tile(lg, 256)
    gq_row = jnp.tile(gq, HB_H).reshape(1, B_W)
    gk_row = jnp.tile(gk, HB_H).reshape(1, B_W)
    bf_row = jnp.concatenate([b_f, jnp.zeros((LANES - HB_H,), F32)]).reshape(1, LANES)
    seg = jnp.arange(B_W) // HB_D
    bd = (seg[:, None] == seg[None, :]).astype(BF16)

    def wide(off):
        return pl.BlockSpec((None, tm, B_W), lambda b, i: (b, i, off // B_W))

    tok = pl.BlockSpec((None, tm, B_W), lambda b, i: (b, i, 0))
    out_shape = [jax.ShapeDtypeStruct((bg, lg, B_W), BF16), jax.ShapeDtypeStruct((bg, lg, B_W), F32),
                 jax.ShapeDtypeStruct((bg, lg, B_W), BF16), jax.ShapeDtypeStruct((bg, lg, B_W), BF16),
                 jax.ShapeDtypeStruct((bg, lg, LANES), F32)]
    out_specs = [tok, tok, tok, tok, pl.BlockSpec((None, tm, LANES), lambda b, i: (b, i, 0))]
    scratch = []
    if cumsum:
        out_shape.append(jax.ShapeDtypeStruct((bg, HB_H, lg), F32))
        out_specs.append(pl.BlockSpec((None, HB_H, tm), lambda b, i: (b, 0, i)))
        scratch.append(pltpu.VMEM((1, LANES), F32))
    return pl.pallas_call(
        functools.partial(_foxprep_kernel, cumsum=cumsum),
        out_shape=tuple(out_shape),
        grid=(bg, lg // tm),
        in_specs=[wide(OFF_QB), wide(OFF_KB), wide(OFF_VB),
                  pl.BlockSpec((None, tm, LANES), lambda b, i: (b, i, OFF_FB // LANES)),
                  pl.BlockSpec((1, B_W), lambda b, i: (0, 0)),
                  pl.BlockSpec((1, B_W), lambda b, i: (0, 0)),
                  pl.BlockSpec((1, LANES), lambda b, i: (0, 0)),
                  pl.BlockSpec((B_W, B_W), lambda b, i: (0, 0))],
        out_specs=tuple(out_specs),
        scratch_shapes=scratch,
        compiler_params=_params(("parallel", "arbitrary")),
        name="fox_prep",
    )(z3, z3, z3, z3, gq_row, gk_row, bf_row, bd)


def _foxattn_kernel(q_ref, k_ref, v_ref, f_ref, o_ref, *, tq):
    qi = pl.program_id(2)
    q = q_ref[...]
    lane_q = _iota(q.shape, 1)
    q_heads = (jnp.where(lane_q < HB_D, q, jnp.zeros_like(q)), jnp.where(lane_q >= HB_D, q, jnp.zeros_like(q)))
    row = _iota((tq, tq), 0)
    colm = _iota((tq, tq), 1)

    def step(j, carry, masked):
        r = pl.multiple_of(j * tq, tq)
        kj = k_ref[pl.ds(r, tq), :]
        vj = v_ref[pl.ds(r, tq), :]
        fj = f_ref[j]
        out = []
        for hh in range(2):
            m, l, acc = carry[hh]
            s = _dot_nt(q_heads[hh], kj) - fj[hh:hh + 1, :]
            if masked:
                s = jnp.where(colm <= row, s, NEG_INF)
            m_new = jnp.maximum(m, jnp.max(s, axis=-1, keepdims=True))
            alpha = jnp.exp(m - m_new)
            p = jnp.exp(s - m_new)
            l = alpha * l + jnp.sum(p, axis=-1, keepdims=True)
            acc = alpha * acc + _dot(p.astype(BF16), vj)
            out.append((m_new, l, acc))
        return tuple(out)

    init = tuple((jnp.full((tq, 1), -1e30, F32), jnp.zeros((tq, 1), F32), jnp.zeros((tq, LANES), F32))
                 for _ in range(2))
    carry = lax.fori_loop(0, qi, lambda j, c: step(j, c, False), init)
    (_, l0, a0), (_, l1, a1) = step(qi, carry, True)
    o = jnp.where(_iota((tq, LANES), 1) < HB_D, a0 / l0, a1 / l1)
    o_ref[...] = o.astype(o_ref.dtype)


def _fox_attn_prompt(qb, kb, vb, ft):
    bg, s_len, _ = qb.shape
    tq = _tile(s_len, 512)
    nk = s_len // tq
    pairs = HB_H // 2
    f5 = ft.reshape(bg, pairs, 2, nk, tq).transpose(0, 1, 3, 2, 4)
    return pl.pallas_call(
        functools.partial(_foxattn_kernel, tq=tq),
        out_shape=jax.ShapeDtypeStruct((bg, s_len, B_W), BF16),
        grid=(bg, pairs, nk),
        in_specs=[pl.BlockSpec((None, tq, LANES), lambda b, p, i: (b, i, p)),
                  pl.BlockSpec((None, s_len, LANES), lambda b, p, i: (b, 0, p)),
                  pl.BlockSpec((None, s_len, LANES), lambda b, p, i: (b, 0, p)),
                  pl.BlockSpec((None, None, nk, 2, tq), lambda b, p, i: (b, p, 0, 0, 0))],
        out_specs=pl.BlockSpec((None, tq, LANES), lambda b, p, i: (b, i, p)),
        compiler_params=_params(("parallel", "parallel", "arbitrary")),
        name="fox_attn_prompt",
    )(qb, kb, vb, f5)


def _foxsample_kernel(pt_ref, q_ref, kn_ref, vn_ref, lfn_ref, *rest, n_new):
    ps = PAGES_PER_STEP
    k_pages = rest[0:ps]
    v_pages = rest[ps:2 * ps]
    f_pages = rest[2 * ps:3 * ps]
    o_ref, q_scr, m_scr, l_scr, acc_scr, fc_scr = rest[3 * ps:]
    g = pl.program_id(1)
    rows = n_new * HB_H
    hmask = (_iota((HB_H, B_W), 1) // HB_D) == _iota((HB_H, B_W), 0)

    @pl.when(g == 0)
    def _():
        for t in range(n_new):
            qt = jnp.broadcast_to(q_ref[t:t + 1, :], (HB_H, B_W))
            q_scr[t * HB_H:(t + 1) * HB_H, :] = jnp.where(hmask, qt, 0.0)
        m_scr[...] = jnp.full_like(m_scr, -1e30)
        l_scr[...] = jnp.zeros_like(l_scr)
        acc_scr[...] = jnp.zeros_like(acc_scr)
        fc_scr[...] = jnp.zeros_like(fc_scr)

    tri_u = (_iota((PAGE, PAGE), 0) <= _iota((PAGE, PAGE), 1)).astype(F32)

    def update(s, v_b):
        m = m_scr[...]
        m_new = jnp.maximum(m, jnp.max(s, axis=-1, keepdims=True))
        alpha = jnp.exp(m - m_new)
        p = jnp.exp(s - m_new)
        l_scr[...] = alpha * l_scr[...] + jnp.sum(p, axis=-1, keepdims=True)
        acc_scr[...] = alpha * acc_scr[...] + _dot(p.astype(BF16), v_b)
        m_scr[...] = m_new

    def cum_forget(lf_t):
        f_loc = _dot_hi(lf_t, tri_u) + fc_scr[...]
        fc_scr[...] = f_loc[:, PAGE - 1:PAGE]
        return f_loc

    f_parts = [cum_forget(f_pages[r][...]) for r in range(ps)]
    f_cat = jnp.concatenate(f_parts, axis=1)
    bias = jnp.concatenate([f_cat] * n_new, axis=0)
    k_cat = jnp.concatenate([k_pages[r][...].astype(BF16) for r in range(ps)], axis=0)
    v_cat = jnp.concatenate([v_pages[r][...].astype(BF16) for r in range(ps)], axis=0)
    q_rows = q_scr[...].astype(BF16)
    update(_dot_nt(q_rows, k_cat) - bias, v_cat)

    @pl.when(g == pl.num_programs(1) - 1)
    def _():
        f_new = cum_forget(lfn_ref[...])
        s = _dot_nt(q_rows, kn_ref[...].astype(BF16)) - jnp.concatenate([f_new] * n_new, axis=0)
        visible = _iota((rows, PAGE), 1) <= (_iota((rows, PAGE), 0) // HB_H)
        update(jnp.where(visible, s, NEG_INF), vn_ref[...].astype(BF16))
        o = acc_scr[...] / l_scr[...]
        out = jnp.zeros((SUBLANES, B_W), F32)
        out_row = _iota((SUBLANES, B_W), 0)
        for t in range(n_new):
            ot = jnp.where(hmask, o[t * HB_H:(t + 1) * HB_H, :], 0.0)
            out = jnp.where(out_row == t, jnp.sum(ot, axis=0, keepdims=True), out)
        o_ref[...] = out.astype(o_ref.dtype)


def _fox_attn_sample(layer, q_new, k_new, v_new, lf_new, cache_k, cache_v, cache_lf_t, page_table):
    db, n_new, _ = q_new.shape
    n_pages = page_table.shape[1]
    n_pool = cache_k.shape[0] // DEPTH
    ps = PAGES_PER_STEP
    assert n_pages % ps == 0 and n_new <= SUBLANES
    qp = jnp.pad(q_new, ((0, 0), (0, SUBLANES - n_new), (0, 0)))
    knp = jnp.pad(k_new, ((0, 0), (0, PAGE - n_new), (0, 0)))
    vnp = jnp.pad(v_new, ((0, 0), (0, PAGE - n_new), (0, 0)))
    lfp = jnp.pad(lf_new.transpose(0, 2, 1), ((0, 0), (0, 0), (0, PAGE - n_new)))
    base = layer * n_pool

    def page_spec(r, rows, cols):
        return pl.BlockSpec((None, rows, cols), lambda b, g, pt: (base + pt[b, g * ps + r], 0, 0))

    in_specs = [pl.BlockSpec((None, SUBLANES, B_W), lambda b, g, pt: (b, 0, 0)),
                pl.BlockSpec((None, PAGE, B_W), lambda b, g, pt: (b, 0, 0)),
                pl.BlockSpec((None, PAGE, B_W), lambda b, g, pt: (b, 0, 0)),
                pl.BlockSpec((None, HB_H, PAGE), lambda b, g, pt: (b, 0, 0))]
    in_specs += [page_spec(r, PAGE, B_W) for r in range(ps)]
    in_specs += [page_spec(r, PAGE, B_W) for r in range(ps)]
    in_specs += [page_spec(r, HB_H, PAGE) for r in range(ps)]
    rows = n_new * HB_H
    return pl.pallas_call(
        functools.partial(_foxsample_kernel, n_new=n_new),
        out_shape=jax.ShapeDtypeStruct((db, SUBLANES, B_W), BF16),
        grid_spec=pltpu.PrefetchScalarGridSpec(
            num_scalar_prefetch=1,
            grid=(db, n_pages // ps),
            in_specs=in_specs,
            out_specs=pl.BlockSpec((None, SUBLANES, B_W), lambda b, g, pt: (b, 0, 0)),
            scratch_shapes=[pltpu.VMEM((rows, B_W), F32), pltpu.VMEM((rows, 1), F32),
                            pltpu.VMEM((rows, 1), F32), pltpu.VMEM((rows, B_W), F32),
                            pltpu.VMEM((HB_H, 1), F32)]),
        compiler_params=_params(("parallel", "arbitrary")),
        name="fox_attn_sample",
    )(page_table, qp, knp, vnp, lfp, *([cache_k] * ps), *([cache_v] * ps), *([cache_lf_t] * ps))


def _merge_kernel(oa_ref, ob_ref, oc_ref, ga_ref, gb_ref, gc_ref, x_ref, gt1_ref, sh2_ref, sc2_ref,
                  wa_ref, wb_ref, wc_ref, wo_ref, nf_ref, wq_ref, keys_ref,
                  x1_ref, h2t_ref, st_ref):
    merged = (_sigmoid(ga_ref[...]) * _dot(oa_ref[...], wa_ref[...])
              + _sigmoid(gb_ref[...]) * _dot(ob_ref[...], wb_ref[...])
              + _sigmoid(gc_ref[...]) * _dot(oc_ref[...], wc_ref[...]))
    x1 = x_ref[...] + gt1_ref[...] * _dot(merged.astype(BF16), wo_ref[...])
    x1_ref[...] = x1
    r = lax.rsqrt(jnp.mean(x1 * x1, axis=-1, keepdims=True) + EPS)
    h2 = (x1 * r) * nf_ref[...] * (1.0 + sc2_ref[...]) + sh2_ref[...]
    h2t_ref[...] = h2.T.astype(BF16)
    qb = _dot(h2.astype(BF16), wq_ref[...]).astype(BF16)
    half = P_QD // 2
    for h in range(P_H):
        for p in range(2):
            lo = (h * 2 + p) * half
            st_ref[lo:lo + half, :] = _dot_nt(keys_ref[p], qb[:, lo:lo + half])


def _merge(oa, ob, oc, z3, x, mod, w_a, w_b, w_c, w_o, norm_ffn, w_q, keys):
    bg, lg, _ = x.shape
    tm = _tile(lg, 256)
    nl = lg // tm
    t_all = bg * lg

    def tok(width):
        return pl.BlockSpec((None, tm, width), lambda b, i: (b, i, 0))

    def gate(k):
        return pl.BlockSpec((None, tm, D), lambda b, i: (b, i, OFF_GATE // D + k))

    def full(shape):
        return pl.BlockSpec(shape, lambda b, i: (0,) * len(shape))

    return pl.pallas_call(
        _merge_kernel,
        out_shape=(jax.ShapeDtypeStruct((bg, lg, D), F32),
                   jax.ShapeDtypeStruct((D, t_all), BF16),
                   jax.ShapeDtypeStruct((P_H * P_QD, t_all), F32)),
        grid=(bg, nl),
        in_specs=[tok(A_W), tok(B_W), tok(C_K), gate(0), gate(1), gate(2), tok(D),
                  _mod_spec(mod, tm, 2), _mod_spec(mod, tm, 3), _mod_spec(mod, tm, 4),
                  full((A_W, D)), full((B_W, D)), full((C_K, D)), full((D, D)), full((1, D)),
                  full((D, P_H * P_QD)), full((2, P_NK, P_QD // 2))],
        out_specs=(tok(D),
                   pl.BlockSpec((D, tm), lambda b, i: (0, b * nl + i)),
                   pl.BlockSpec((P_H * P_QD, tm), lambda b, i: (0, b * nl + i))),
        compiler_params=_params(("parallel", "parallel")),
        name="merge",
    )(oa, ob, oc, z3, z3, z3, x, mod, mod, mod, w_a, w_b, w_c, w_o, norm_ffn.reshape(1, D), w_q, keys)


def _topk_kernel(st_ref, o_ref):
    tt = st_ref.shape[1]

    def top_rows(s):
        rows = []
        cur = s
        for it in range(P_TOPK):
            m = jnp.max(cur, axis=0, keepdims=True)
            rows.append(m)
            if it + 1 < P_TOPK:
                cur = jnp.where(cur == m, NEG_INF, cur)
        return rows

    rank = _iota((P_TOPK, tt), 0)
    thr_rows, nrm_rows = [], []
    for h in range(P_H):
        lo = h * P_QD
        v1 = top_rows(st_ref[lo:lo + P_NK, :])
        v2 = top_rows(st_ref[lo + P_NK:lo + 2 * P_NK, :])
        v1s = jnp.concatenate(v1, axis=0)
        v2s = jnp.concatenate(v2, axis=0)
        groups = []
        for b in range(3):
            groups.append(jnp.where(rank < P_TOPK // (b + 1), v1s + v2[b], NEG_INF))
        for a in range(4):
            nb = P_TOPK // (a + 1)
            n_rows = P_TOPK if nb > SUBLANES else SUBLANES
            rk = _iota((n_rows, tt), 0)
            ok = jnp.where(rk >= 3, rk, nb) < nb
            groups.append(jnp.where(ok, v2s[:n_rows] + v1[a], NEG_INF))
        m_top = None
        z = None
        for it in range(P_TOPK):
            m = functools.reduce(jnp.maximum, [jnp.max(gp, axis=0, keepdims=True) for gp in groups])
            if it == 0:
                m_top = m
                z = jnp.ones_like(m)
            else:
                z = z + jnp.exp(m - m_top)
            if it + 1 < P_TOPK:
                groups = [jnp.where(gp == m, NEG_INF, gp) for gp in groups]
        thr_rows.append(m)
        nrm_rows.append(-(m_top + jnp.log(z)))
    o_ref[...] = jnp.concatenate(thr_rows + nrm_rows, axis=0)


def _topk(st):
    t_all = st.shape[1]
    tt = _tile(t_all, 256)
    return pl.pallas_call(
        _topk_kernel,
        out_shape=jax.ShapeDtypeStruct((2 * P_H, t_all), F32),
        grid=(t_all // tt,),
        in_specs=[pl.BlockSpec((P_H * P_QD, tt), lambda t: (0, t))],
        out_specs=pl.BlockSpec((2 * P_H, tt), lambda t: (0, t)),
        compiler_params=_params(("parallel",)),
        name="peer_topk",
    )(st)


def _peer_kernel(st_ref, stat_ref, h2t_ref, u_ref, vt_ref, x1_ref, gt2_ref, o_ref, acc_scr, wa_scr, *, ti):
    e = pl.program_id(2)

    @pl.when(e == 0)
    def _():
        acc_scr[...] = jnp.zeros_like(acc_scr)

    ht = _dot(u_ref[...], h2t_ref[...])
    c0 = math.sqrt(2.0 / math.pi)
    for ii in range(ti):
        hs = ht[ii * P_NK:(ii + 1) * P_NK]
        act = 0.5 * hs * (1.0 + jnp.tanh(c0 * (hs + 0.044715 * (hs * hs * hs))))
        w = jnp.zeros_like(hs)
        for h in range(P_H):
            lo = h * P_QD
            a_row = st_ref[pl.ds(lo + e * ti + ii, 1), :]
            c = st_ref[lo + P_NK:lo + 2 * P_NK, :] + a_row
            w = w + jnp.where(c >= stat_ref[h:h + 1, :], jnp.exp(c + stat_ref[P_H + h:P_H + h + 1, :]), 0.0)
        wa_scr[ii * P_NK:(ii + 1) * P_NK, :] = (w * act).astype(BF16)
    acc_scr[...] += _dot(vt_ref[...], wa_scr[...])

    @pl.when(e == pl.num_programs(2) - 1)
    def _():
        o_ref[...] = x1_ref[...] + gt2_ref[...] * acc_scr[...].T


def _peer(st, stats, h2t, u_b, vt_b, x1, mod):
    bg, lg, _ = x1.shape
    tt = _tile(lg, 512)
    nl = lg // tt
    ti = 4
    te = ti * P_NK
    return pl.pallas_call(
        functools.partial(_peer_kernel, ti=ti),
        out_shape=jax.ShapeDtypeStruct((bg, lg, D), F32),
        grid=(bg, nl, N_EXP // te),
        in_specs=[pl.BlockSpec((P_H * P_QD, tt), lambda b, i, e: (0, b * nl + i)),
                  pl.BlockSpec((2 * P_H, tt), lambda b, i, e: (0, b * nl + i)),
                  pl.BlockSpec((D, tt), lambda b, i, e: (0, b * nl + i)),
                  pl.BlockSpec((te, D), lambda b, i, e: (e, 0)),
                  pl.BlockSpec((D, te), lambda b, i, e: (0, e)),
                  pl.BlockSpec((None, tt, D), lambda b, i, e: (b, i, 0)),
                  _mod_spec(mod, tt, 5)],
        out_specs=pl.BlockSpec((None, tt, D), lambda b, i, e: (b, i, 0)),
        scratch_shapes=[pltpu.VMEM((D, tt), F32), pltpu.VMEM((te, tt), BF16)],
        compiler_params=_params(("parallel", "parallel", "arbitrary")),
        name="peer_experts",
    )(st, stats, h2t, u_b, vt_b, x1, mod)


def _pack_w_in(w):
    pad = jnp.zeros((D, LANES - 8), w.dtype)
    o_fb = 2 * A_W + 2 * A_W + 3 * B_W
    o_c = o_fb + HB_H
    o_ab = o_c + 3 * C_K
    o_zc = o_ab + 2 * HC_H
    o_gate = o_zc + C_K
    packed = jnp.concatenate([
        w[:, 0:4 * A_W], w[:, o_gate:o_gate + 3 * D], w[:, 4 * A_W:o_fb], w[:, o_c:o_ab],
        w[:, o_zc:o_gate], w[:, o_fb:o_c], pad, w[:, o_ab:o_zc], pad], axis=1)
    assert packed.shape[1] == NP
    return packed.astype(BF16)


def _layer(l, x, mod, w, sample):
    bg, lg, _ = x.shape
    z = _in_proj(x, mod, w["norm_mix"][l], w["w_in"][l])
    if sample is None:
        z3 = z
        bs, l_seq = bg, lg
        chunk = LIN_C
        hgrn0 = jnp.zeros((bs, HA_H, HA_DK, HA_DV), F32)
        gdn0 = jnp.zeros((bs, HC_H, HC_DK, HC_DV), F32)
        conv0 = jnp.zeros((bs, SUBLANES, CONV_CH), F32)
    else:
        bs, l_seq = sample["db"], sample["t"]
        chunk = SUBLANES
        z3 = jnp.pad(z.reshape(bs, l_seq, NP), ((0, 0), (0, SUBLANES - l_seq), (0, 0)))
        hgrn0 = sample["state_hgrn"][l]
        gdn0 = sample["state_gdn"][l]
        conv0 = jnp.pad(sample["state_conv"][l], ((0, 0), (SUBLANES - (CONV_W - 1), 0), (0, 0)))

    o_a, s_hgrn = _hgrn(z3, w["hgrn_lb_logits"], w["hgrn_norm"][l], hgrn0, layer=l, chunk=chunk, l_valid=l_seq)
    o_c, s_gdn = _gdn(z3, w["gdn_conv_w"][l], w["gdn_a_log"][l], w["gdn_dt_bias"][l], w["gdn_norm"][l],
                      conv0, gdn0, chunk=chunk, l_valid=l_seq)
    zseq = z.reshape(bs, l_seq, NP)
    conv_new = zseq[:, l_seq - (CONV_W - 1):, OFF_QC:OFF_QC + CONV_CH]

    prep = _fox_prep(z, w["fox_q_norm"][l], w["fox_k_norm"][l], w["fox_b_f"][l], cumsum=sample is None)
    qn, kn, knb, vnb, lf = prep[:5]
    v_b = z[:, :, OFF_VB:OFF_VB + B_W]
    if sample is None:
        o_b = _fox_attn_prompt(qn, knb, vnb, prep[5])
    else:
        o_b = _fox_attn_sample(l, qn.astype(F32).reshape(bs, l_seq, B_W), kn.reshape(bs, l_seq, B_W),
                               v_b.reshape(bs, l_seq, B_W), lf[0, :, :HB_H].reshape(bs, l_seq, HB_H),
                               sample["cache_k"], sample["cache_v"], sample["cache_lf_t"], sample["page_table"])
        o_b = o_b[:, :l_seq].reshape(bg, lg, B_W)
        o_a = o_a[:, :l_seq].reshape(bg, lg, A_W)
        o_c = o_c[:, :l_seq].reshape(bg, lg, C_K)

    x1, h2t, st = _merge(o_a, o_b, o_c, z, x, mod, w["w_br_a"][l], w["w_br_b"][l], w["w_br_c"][l],
                         w["w_out"][l], w["norm_ffn"][l], w["peer_w_q"][l], w["peer_keys"][l])
    stats = _topk(st)
    x2 = _peer(st, stats, h2t, w["peer_u"][l], w["peer_vt"][l], x1, mod)

    k_leaf = kn.reshape(bs, l_seq, HB_H, HB_D)
    v_leaf = v_b.reshape(bs, l_seq, HB_H, HB_D)
    lf_leaf = lf[:, :, :HB_H].reshape(bs, l_seq, HB_H)
    return x2, (k_leaf, v_leaf, lf_leaf, s_hgrn, s_gdn, conv_new)


def _trunk(x, mods, w, sample):
    leaves = [[] for _ in range(6)]
    for l in range(DEPTH):
        x, st = _layer(l, x, mods[l], w, sample)
        for lst, s in zip(leaves, st):
            lst.append(s)
    return x, [jnp.stack(v) for v in leaves]


def kernel(x_prompt, x_sample, c_prompt, c_sample, cache_fox_k, cache_fox_v, cache_fox_logf, page_table,
           state_hgrn, state_gdn, state_gdn_conv, w_ada, b_ada, norm_mix, norm_ffn, w_in, hgrn_lb_logits,
           hgrn_norm, fox_b_f, fox_q_norm, fox_k_norm, gdn_conv_w, gdn_a_log, gdn_dt_bias, gdn_norm,
           w_br_a, w_br_b, w_br_c, w_out, peer_w_q, peer_keys, peer_u, peer_v):
    bp = x_prompt.shape[0]
    db, t_new, _ = x_sample.shape
    n_pool = cache_fox_k.shape[1]

    w = {
        "norm_mix": norm_mix, "norm_ffn": norm_ffn, "hgrn_lb_logits": hgrn_lb_logits, "hgrn_norm": hgrn_norm,
        "fox_b_f": fox_b_f, "fox_q_norm": fox_q_norm, "fox_k_norm": fox_k_norm, "gdn_conv_w": gdn_conv_w,
        "gdn_a_log": gdn_a_log, "gdn_dt_bias": gdn_dt_bias, "gdn_norm": gdn_norm,
        "w_in": [_pack_w_in(w_in[l]) for l in range(DEPTH)],
        "w_br_a": w_br_a.astype(BF16), "w_br_b": w_br_b.astype(BF16), "w_br_c": w_br_c.astype(BF16),
        "w_out": w_out.astype(BF16), "peer_w_q": peer_w_q.astype(BF16), "peer_keys": peer_keys.astype(BF16),
        "peer_u": peer_u.astype(BF16), "peer_vt": peer_v.astype(BF16).transpose(0, 2, 1),
    }

    n_c = bp + db
    c_all = jnp.pad(jnp.concatenate([c_prompt, c_sample], axis=0), ((0, (-n_c) % SUBLANES), (0, 0)))
    mod = _ada(c_all, w_ada, b_ada)
    mods_p = [mod[l, :bp].reshape(bp, 1, 6 * D) for l in range(DEPTH)]
    mods_s = [jnp.repeat(mod[l, bp:n_c], t_new, axis=0).reshape(1, db * t_new, 6 * D) for l in range(DEPTH)]

    sample = {
        "db": db, "t": t_new, "page_table": page_table,
        "state_hgrn": state_hgrn, "state_gdn": state_gdn, "state_conv": state_gdn_conv,
        "cache_k": cache_fox_k.reshape(DEPTH * n_pool, PAGE, B_W),
        "cache_v": cache_fox_v.reshape(DEPTH * n_pool, PAGE, B_W),
        "cache_lf_t": cache_fox_logf.transpose(0, 1, 3, 2).reshape(DEPTH * n_pool, HB_H, PAGE),
    }

    y_p, leaves_p = _trunk(x_prompt, mods_p, w, None)
    y_s, leaves_s = _trunk(x_sample.reshape(1, db * t_new, D), mods_s, w, sample)
    return (y_p, y_s.reshape(db, t_new, D), *leaves_p, *leaves_s)
```

```python
import functools
import math

import jax
import jax.numpy as jnp
from jax import lax
from jax.experimental import pallas as pl
from jax.experimental.pallas import tpu as pltpu

F32 = jnp.float32
BF16 = jnp.bfloat16
HI = lax.Precision.HIGHEST
NEG_INF = float("-inf")

D = 1024
DEPTH = 2
HA_H, HA_DK, HA_DV = 4, 128, 128
HB_H, HB_D = 8, 64
HC_H, HC_DK, HC_DV = 4, 128, 128
CONV_W = 4
P_H, P_TOPK, P_NK, P_QD = 8, 16, 128, 256
N_EXP = P_NK * P_NK
EPS = 1e-6
A_W = HA_H * HA_DK
B_W = HB_H * HB_D
C_K = HC_H * HC_DK
CONV_CH = 3 * C_K
PAGE = 128

LANES = 128
SUBLANES = 8
VMEM_LIMIT = 52 * 1024 * 1024

OFF_FA, OFF_QA, OFF_IA, OFF_GA = 0, 512, 1024, 1536
OFF_GATE = 2048
OFF_QB, OFF_KB, OFF_VB = 5120, 5632, 6144
OFF_QC, OFF_KC, OFF_VC = 6656, 7168, 7680
OFF_ZC = 8192
OFF_FB = 8704
OFF_AB = 8832
NP = 8960

LIN_C = 64
HGRN_SUB = 16
PEER_JB = 32
PAGES_PER_STEP = 8


def _dot(a, b):
    return jnp.dot(a, b, preferred_element_type=F32)


def _dot_hi(a, b):
    return jnp.dot(a, b, preferred_element_type=F32, precision=HI)


def _dot_nt(a, b):
    return lax.dot_general(a, b, (((1,), (1,)), ((), ())), preferred_element_type=F32)


def _dot_tn(a, b, precision=None):
    return lax.dot_general(a, b, (((0,), (0,)), ((), ())), preferred_element_type=F32, precision=precision)


def _iota(shape, dim):
    return lax.broadcasted_iota(jnp.int32, shape, dim)


def _sigmoid(x):
    return jax.nn.sigmoid(x)


def _silu(x):
    return x * jax.nn.sigmoid(x)


def _softplus(x):
    return jnp.maximum(x, 0.0) + jnp.log1p(jnp.exp(-jnp.abs(x)))


def _log_sigmoid(x):
    return jnp.minimum(x, 0.0) - jnp.log1p(jnp.exp(-jnp.abs(x)))


def _params(sem):
    return pltpu.CompilerParams(dimension_semantics=sem, vmem_limit_bytes=VMEM_LIMIT)


def _chunk_start(c, c_len):
    return c * c_len if isinstance(c, int) else pl.multiple_of(c * c_len, c_len)


def _for_chunks(n_chunks, body):
    if n_chunks == 1:
        body(0, 0)
    else:
        lax.fori_loop(0, n_chunks, body, 0)


def _tile(n, pref):
    t = min(n, pref)
    assert n % t == 0, (n, pref)
    return t


def _ada_kernel(c_ref, w_ref, b_ref, o_ref):
    sc = _silu(c_ref[...]).astype(BF16)
    o_ref[...] = _dot(sc, w_ref[...].astype(BF16)) + b_ref[...]


def _ada(c_all, w_ada, b_ada):
    rows = c_all.shape[0]
    tn = 768
    return pl.pallas_call(
        _ada_kernel,
        out_shape=jax.ShapeDtypeStruct((DEPTH, rows, 6 * D), F32),
        grid=(DEPTH, 6 * D // tn),
        in_specs=[pl.BlockSpec((rows, D), lambda l, n: (0, 0)),
                  pl.BlockSpec((None, D, tn), lambda l, n: (l, 0, n)),
                  pl.BlockSpec((None, 1, tn), lambda l, n: (l, 0, n))],
        out_specs=pl.BlockSpec((None, rows, tn), lambda l, n: (l, 0, n)),
        compiler_params=_params(("parallel", "parallel")),
        name="ada",
    )(c_all, w_ada, b_ada.reshape(DEPTH, 1, 6 * D))


def _mod_spec(mod, tm, k):
    if mod.shape[1] == 1:
        return pl.BlockSpec((None, 1, D), lambda b, i, *_: (b, 0, k))
    return pl.BlockSpec((None, tm, D), lambda b, i, *_: (b, i, k))


def _in_kernel(x_ref, sh_ref, sc_ref, g_ref, w_ref, z_ref, h_scr):
    @pl.when(pl.program_id(2) == 0)
    def _():
        x = x_ref[...]
        r = lax.rsqrt(jnp.mean(x * x, axis=-1, keepdims=True) + EPS)
        h = (x * r) * g_ref[...] * (1.0 + sc_ref[...]) + sh_ref[...]
        h_scr[...] = h.astype(BF16)

    z_ref[...] = _dot(h_scr[...], w_ref[...])


def _in_proj(x, mod, gain, w_packed):
    bg, lg, _ = x.shape
    tm = _tile(lg, 512)
    tn = 1280
    return pl.pallas_call(
        _in_kernel,
        out_shape=jax.ShapeDtypeStruct((bg, lg, NP), F32),
        grid=(bg, lg // tm, NP // tn),
        in_specs=[pl.BlockSpec((None, tm, D), lambda b, i, n: (b, i, 0)),
                  _mod_spec(mod, tm, 0), _mod_spec(mod, tm, 1),
                  pl.BlockSpec((1, D), lambda b, i, n: (0, 0)),
                  pl.BlockSpec((D, tn), lambda b, i, n: (0, n))],
        out_specs=pl.BlockSpec((None, tm, tn), lambda b, i, n: (b, i, n)),
        scratch_shapes=[pltpu.VMEM((tm, D), BF16)],
        compiler_params=_params(("parallel", "parallel", "arbitrary")),
        name="in_proj",
    )(x, mod, mod, gain.reshape(1, D), w_packed)


def _hgrn_kernel(fa_ref, qa_ref, ia_ref, ga_ref, lbl_ref, nw_ref, s0_ref, o_ref, sout_ref, s_scr,
                 *, layer, chunk, n_chunks, l_valid, l_padded):
    i = pl.program_id(2)
    c_len = chunk

    @pl.when(i == 0)
    def _():
        s_scr[...] = s0_ref[...]

    lg = lbl_ref[...]
    e = jnp.exp(lg - jnp.max(lg, axis=0, keepdims=True))
    p = e / jnp.sum(e, axis=0, keepdims=True)
    cs = p[0:1]
    for j in range(1, layer + 1):
        cs = cs + p[j:j + 1]
    lb = jnp.maximum(cs - p[0:1], 0.0)
    log_lb = jnp.log(lb)
    log1m_lb = jnp.log1p(-lb)

    sub = min(HGRN_SUB, c_len)
    tri = (_iota((c_len, c_len), 1) <= _iota((c_len, c_len), 0)).astype(F32)
    lane = _iota((sub, c_len), 1)
    row1 = _iota((c_len, 1), 0)
    ones_cv = jnp.ones((c_len, HA_DV), F32)

    def chunk_body(c, carry):
        r = _chunk_start(c, c_len)
        fa = fa_ref[pl.ds(r, c_len), :]
        qa = qa_ref[pl.ds(r, c_len), :]
        v = ia_ref[pl.ds(r, c_len), :]
        ga = ga_ref[pl.ds(r, c_len), :]

        b_ = log1m_lb + _log_sigmoid(fa)
        log_f = jnp.maximum(log_lb, b_) + jnp.log1p(jnp.exp(-jnp.abs(log_lb - b_)))
        k = (1.0 - lb) * _sigmoid(-fa)
        if l_valid < l_padded:
            valid = (i * (n_chunks * c_len) + r + row1) < l_valid
            log_f = jnp.where(valid, log_f, 0.0)
            k = jnp.where(valid, k, 0.0)
        q = _silu(qa)
        a_cum = _dot_hi(tri, log_f)

        blocks = []
        for bi in range(c_len // sub):
            lo, hi = bi * sub, (bi + 1) * sub
            if bi == 0:
                att_b = jnp.zeros((sub, c_len), F32)
            else:
                a_ref = a_cum[lo - 1:lo]
                qs = q[lo:hi] * jnp.exp(a_cum[lo:hi] - a_ref)
                ks = jnp.where(row1 < lo, k * jnp.exp(jnp.minimum(a_ref - a_cum, 0.0)), 0.0)
                att_b = _dot_nt(qs.astype(BF16), ks.astype(BF16))
            for s in range(lo, hi):
                r0 = (s // SUBLANES) * SUBLANES
                rel = a_cum[r0:hi] - a_cum[s:s + 1]
                dec = jnp.exp(jnp.where(row1[r0:hi] >= s, rel, NEG_INF))
                col = jnp.sum(q[r0:hi] * k[s:s + 1] * dec, axis=-1, keepdims=True)
                if r0 > lo:
                    col = jnp.concatenate([jnp.zeros((r0 - lo, 1), F32), col], axis=0)
                att_b = jnp.where(lane == s, col, att_b)
            blocks.append(att_b)
        att = blocks[0] if len(blocks) == 1 else jnp.concatenate(blocks, axis=0)

        s_prev = s_scr[...]
        qd = q * jnp.exp(a_cum)
        o = _dot(qd.astype(BF16), s_prev.astype(BF16)) + _dot(att.astype(BF16), v.astype(BF16))
        a_last = a_cum[c_len - 1:c_len]
        kd = k * jnp.exp(a_last - a_cum)
        dec_s = jnp.exp(_dot_tn(log_f, ones_cv, precision=HI))
        s_scr[...] = dec_s * s_prev + _dot_tn(kd.astype(BF16), v.astype(BF16))

        rr = lax.rsqrt(jnp.mean(o * o, axis=-1, keepdims=True) + EPS)
        o_ref[pl.ds(r, c_len), :] = ((o * rr) * nw_ref[...] * _silu(ga)).astype(o_ref.dtype)
        return carry

    _for_chunks(n_chunks, chunk_body)

    @pl.when(i == pl.num_programs(2) - 1)
    def _():
        sout_ref[...] = s_scr[...]


def _hgrn(z3, lb_logits, norm_w, s0, *, layer, chunk, l_valid):
    bs, lp, _ = z3.shape
    tb = _tile(lp, 4 * chunk)
    kern = functools.partial(_hgrn_kernel, layer=layer, chunk=chunk, n_chunks=tb // chunk,
                             l_valid=l_valid, l_padded=lp)

    def col(off):
        return pl.BlockSpec((None, tb, LANES), lambda b, h, i: (b, i, off // LANES + h))

    return pl.pallas_call(
        kern,
        out_shape=(jax.ShapeDtypeStruct((bs, lp, A_W), BF16),
                   jax.ShapeDtypeStruct((bs, HA_H, HA_DK, HA_DV), F32)),
        grid=(bs, HA_H, lp // tb),
        in_specs=[col(OFF_FA), col(OFF_QA), col(OFF_IA), col(OFF_GA),
                  pl.BlockSpec((DEPTH, LANES), lambda b, h, i: (0, h)),
                  pl.BlockSpec((1, HA_DV), lambda b, h, i: (0, 0)),
                  pl.BlockSpec((None, None, HA_DK, HA_DV), lambda b, h, i: (b, h, 0, 0))],
        out_specs=(pl.BlockSpec((None, tb, LANES), lambda b, h, i: (b, i, h)),
                   pl.BlockSpec((None, None, HA_DK, HA_DV), lambda b, h, i: (b, h, 0, 0))),
        scratch_shapes=[pltpu.VMEM((HA_DK, HA_DV), F32)],
        compiler_params=_params(("parallel", "parallel", "arbitrary")),
        name="hgrn2",
    )(z3, z3, z3, z3, lb_logits, norm_w.reshape(1, HA_DV), s0)


def _gdn_kernel(q_ref, k_ref, v_ref, zc_ref, ab_ref, cw_ref, alog_ref, dt_ref, nw_ref, conv0_ref, s0_ref,
                o_ref, sout_ref, s_scr, prev_scr, act_scr, gb_scr, x_scr,
                *, chunk, n_chunks, l_valid, l_padded):
    i = pl.program_id(1)
    c_len = chunk
    tb = n_chunks * c_len

    @pl.when(i == 0)
    def _():
        s_scr[...] = s0_ref[...]
        prev_scr[...] = conv0_ref[...]

    row8 = _iota((SUBLANES, C_K), 0)

    def conv(x, prev, w):
        y = x * w[CONV_W - 1:CONV_W]
        for j in range(1, CONV_W):
            xr = pltpu.roll(x, j, 0)
            head = jnp.where(row8 < j, pltpu.roll(prev, j, 0), xr[:SUBLANES])
            xs = head if tb == SUBLANES else jnp.concatenate([head, xr[SUBLANES:]], axis=0)
            y = y + xs * w[CONV_W - 1 - j:CONV_W - j]
        return _silu(y)

    for n, ref in enumerate((q_ref, k_ref, v_ref)):
        x = ref[...]
        lo, hi = n * C_K, (n + 1) * C_K
        act_scr[:, lo:hi] = conv(x, prev_scr[:, lo:hi], cw_ref[:, lo:hi])
        prev_scr[:, lo:hi] = x[tb - SUBLANES:]

    ab = ab_ref[...]
    g_all = -jnp.exp(alog_ref[...]) * _softplus(ab + dt_ref[...])
    b_all = _sigmoid(ab)
    if l_valid < l_padded:
        valid = (i * tb + _iota((tb, 1), 0)) < l_valid
        g_all = jnp.where(valid, g_all, 0.0)
        b_all = jnp.where(valid, b_all, 0.0)
    gb_scr[:, 0:LANES] = g_all
    gb_scr[:, LANES:2 * LANES] = b_all

    ii = _iota((c_len, c_len), 0)
    jj = _iota((c_len, c_len), 1)
    tri = (jj <= ii).astype(F32)
    tri_u = (ii <= jj).astype(F32)

    def chunk_body(c, carry):
        r = _chunk_start(c, c_len)
        heads = []
        for h in range(HC_H):
            lo, hi = h * HC_DK, (h + 1) * HC_DK
            qh = act_scr[pl.ds(r, c_len), lo:hi]
            kh = act_scr[pl.ds(r, c_len), C_K + lo:C_K + hi]
            vh = act_scr[pl.ds(r, c_len), 2 * C_K + lo:2 * C_K + hi]
            qh = qh * lax.rsqrt(jnp.sum(qh * qh, axis=-1, keepdims=True) + EPS) * (HC_DK ** -0.5)
            kh = kh * lax.rsqrt(jnp.sum(kh * kh, axis=-1, keepdims=True) + EPS)
            g_col = gb_scr[pl.ds(r, c_len), h:h + 1]
            b_col = gb_scr[pl.ds(r, c_len), LANES + HC_H + h:LANES + HC_H + h + 1]
            g_b = jnp.broadcast_to(g_col, (c_len, LANES))
            g_cum = _dot_hi(tri, g_b)
            g_row = _dot_tn(g_b, tri_u, precision=HI)[:c_len]
            rel = g_cum[:, :c_len] - g_row
            d_causal = jnp.exp(jnp.where(jj <= ii, rel, NEG_INF))
            d_strict_t = jnp.exp(jnp.where(jj > ii, -rel, NEG_INF))
            kb = (b_col * kh).astype(BF16)
            khb = kh.astype(BF16)
            l_t = _dot_nt(khb, kb) * d_strict_t
            qk = _dot_nt(qh.astype(BF16), khb) * d_causal
            e_g = jnp.exp(g_cum)
            x_scr[h, :, 0:HC_DK] = b_col * e_g * kh
            x_scr[h, :, HC_DK:] = b_col * vh
            heads.append((qh, kh, l_t, qk, e_g, g_cum))

        for t in range(1, c_len):
            r1 = ((t + SUBLANES - 1) // SUBLANES) * SUBLANES
            for h in range(HC_H):
                l_t = heads[h][2]
                contrib = jnp.sum(x_scr[h, 0:r1, :] * l_t[0:r1, t:t + 1], axis=0, keepdims=True)
                x_scr[h, t:t + 1, :] = x_scr[h, t:t + 1, :] - contrib

        for h in range(HC_H):
            qh, kh, _, qk, e_g, g_cum = heads[h]
            lo, hi = h * HC_DV, (h + 1) * HC_DV
            xs = x_scr[h]
            s_prev = s_scr[h]
            s_b = s_prev.astype(BF16)
            u = xs[:, HC_DK:] - _dot(xs[:, :HC_DK].astype(BF16), s_b)
            u_b = u.astype(BF16)
            o = e_g * _dot(qh.astype(BF16), s_b) + _dot(qk.astype(BF16), u_b)
            g_last = g_cum[c_len - 1:c_len]
            kd = kh * jnp.exp(g_last - g_cum)
            s_scr[h] = jnp.exp(g_last) * s_prev + _dot_tn(kd.astype(BF16), u_b)
            rr = lax.rsqrt(jnp.mean(o * o, axis=-1, keepdims=True) + EPS)
            zc = zc_ref[pl.ds(r, c_len), lo:hi]
            o_ref[pl.ds(r, c_len), lo:hi] = ((o * rr) * nw_ref[...] * _silu(zc)).astype(o_ref.dtype)
        return carry

    _for_chunks(n_chunks, chunk_body)

    @pl.when(i == pl.num_programs(1) - 1)
    def _():
        sout_ref[...] = s_scr[...]


def _gdn(z3, conv_w, a_log, dt_bias, norm_w, conv0, s0, *, chunk, l_valid):
    bs, lp, _ = z3.shape
    tb = _tile(lp, 4 * chunk)
    kern = functools.partial(_gdn_kernel, chunk=chunk, n_chunks=tb // chunk, l_valid=l_valid, l_padded=lp)
    pad = jnp.zeros((LANES - HC_H,), F32)
    alog_row = jnp.concatenate([a_log, pad]).reshape(1, LANES)
    dt_row = jnp.concatenate([dt_bias, pad]).reshape(1, LANES)

    def wide(off):
        return pl.BlockSpec((None, tb, C_K), lambda b, i: (b, i, off // C_K))

    return pl.pallas_call(
        kern,
        out_shape=(jax.ShapeDtypeStruct((bs, lp, C_K), BF16),
                   jax.ShapeDtypeStruct((bs, HC_H, HC_DK, HC_DV), F32)),
        grid=(bs, lp // tb),
        in_specs=[wide(OFF_QC), wide(OFF_KC), wide(OFF_VC), wide(OFF_ZC),
                  pl.BlockSpec((None, tb, LANES), lambda b, i: (b, i, OFF_AB // LANES)),
                  pl.BlockSpec((CONV_W, CONV_CH), lambda b, i: (0, 0)),
                  pl.BlockSpec((1, LANES), lambda b, i: (0, 0)),
                  pl.BlockSpec((1, LANES), lambda b, i: (0, 0)),
                  pl.BlockSpec((1, HC_DV), lambda b, i: (0, 0)),
                  pl.BlockSpec((None, SUBLANES, CONV_CH), lambda b, i: (b, 0, 0)),
                  pl.BlockSpec((None, HC_H, HC_DK, HC_DV), lambda b, i: (b, 0, 0, 0))],
        out_specs=(pl.BlockSpec((None, tb, C_K), lambda b, i: (b, i, 0)),
                   pl.BlockSpec((None, HC_H, HC_DK, HC_DV), lambda b, i: (b, 0, 0, 0))),
        scratch_shapes=[pltpu.VMEM((HC_H, HC_DK, HC_DV), F32),
                        pltpu.VMEM((SUBLANES, CONV_CH), F32),
                        pltpu.VMEM((tb, CONV_CH), F32),
                        pltpu.VMEM((tb, 2 * LANES), F32),
                        pltpu.VMEM((HC_H, chunk, HC_DK + HC_DV), F32)],
        compiler_params=_params(("parallel", "arbitrary")),
        name="gdn",
    )(z3, z3, z3, z3, z3, conv_w, alog_row, dt_row, norm_w.reshape(1, HC_DV), conv0, s0)


def _foxprep_kernel(q_ref, k_ref, v_ref, fb_ref, gq_ref, gk_ref, bf_ref, bd_ref,
                    qo_ref, ko_ref, kbo_ref, vbo_ref, lfo_ref, *rest, cumsum):
    bd = bd_ref[...]

    def head_rms(x, g):
        x2 = x * x
        hi = x2.astype(BF16)
        lo = (x2 - hi.astype(F32)).astype(BF16)
        ss = _dot(hi, bd) + _dot(lo, bd)
        return x * lax.rsqrt(ss * (1.0 / HB_D) + EPS) * g

    qn = head_rms(q_ref[...], gq_ref[...])
    kn = head_rms(k_ref[...], gk_ref[...])
    qo_ref[...] = (qn * (HB_D ** -0.5)).astype(BF16)
    ko_ref[...] = kn
    kbo_ref[...] = kn.astype(BF16)
    vbo_ref[...] = v_ref[...].astype(BF16)
    lf = _log_sigmoid(fb_ref[...] + bf_ref[...])
    lf = jnp.where(_iota(lf.shape, 1) < HB_H, lf, 0.0)
    lfo_ref[...] = lf
    if cumsum:
        ft_ref, carry = rest
        tm = lf.shape[0]

        @pl.when(pl.program_id(1) == 0)
        def _():
            carry[...] = jnp.zeros_like(carry)

        tri = (_iota((tm, tm), 1) <= _iota((tm, tm), 0)).astype(F32)
        f_cum = _dot_hi(tri, lf) + carry[...]
        carry[...] = f_cum[tm - 1:tm]
        ft_ref[...] = f_cum.T[:HB_H]


def _fox_prep(z3, gq, gk, b_f, *, cumsum):
    bg, lg, _ = z3.shape
    tm = _tile(lg, 256)
    gq_row = jnp.tile(gq, HB_H).reshape(1, B_W)
    gk_row = jnp.tile(gk, HB_H).reshape(1, B_W)
    bf_row = jnp.concatenate([b_f, jnp.zeros((LANES - HB_H,), F32)]).reshape(1, LANES)
    seg = jnp.arange(B_W) // HB_D
    bd = (seg[:, None] == seg[None, :]).astype(BF16)

    def wide(off):
        return pl.BlockSpec((None, tm, B_W), lambda b, i: (b, i, off // B_W))

    tok = pl.BlockSpec((None, tm, B_W), lambda b, i: (b, i, 0))
    out_shape = [jax.ShapeDtypeStruct((bg, lg, B_W), BF16), jax.ShapeDtypeStruct((bg, lg, B_W), F32),
                 jax.ShapeDtypeStruct((bg, lg, B_W), BF16), jax.ShapeDtypeStruct((bg, lg, B_W), BF16),
                 jax.ShapeDtypeStruct((bg, lg, LANES), F32)]
    out_specs = [tok, tok, tok, tok, pl.BlockSpec((None, tm, LANES), lambda b, i: (b, i, 0))]
    scratch = []
    if cumsum:
        out_shape.append(jax.ShapeDtypeStruct((bg, HB_H, lg), F32))
        out_specs.append(pl.BlockSpec((None, HB_H, tm), lambda b, i: (b, 0, i)))
        scratch.append(pltpu.VMEM((1, LANES), F32))
    return pl.pallas_call(
        functools.partial(_foxprep_kernel, cumsum=cumsum),
        out_shape=tuple(out_shape),
        grid=(bg, lg // tm),
        in_specs=[wide(OFF_QB), wide(OFF_KB), wide(OFF_VB),
                  pl.BlockSpec((None, tm, LANES), lambda b, i: (b, i, OFF_FB // LANES)),
                  pl.BlockSpec((1, B_W), lambda b, i: (0, 0)),
                  pl.BlockSpec((1, B_W), lambda b, i: (0, 0)),
                  pl.BlockSpec((1, LANES), lambda b, i: (0, 0)),
                  pl.BlockSpec((B_W, B_W), lambda b, i: (0, 0))],
        out_specs=tuple(out_specs),
        scratch_shapes=scratch,
        compiler_params=_params(("parallel", "arbitrary")),
        name="fox_prep",
    )(z3, z3, z3, z3, gq_row, gk_row, bf_row, bd)


def _foxattn_kernel(q_ref, k_ref, v_ref, f_ref, o_ref, *, tq):
    qi = pl.program_id(2)
    q = q_ref[...]
    lane_q = _iota(q.shape, 1)
    q_heads = (jnp.where(lane_q < HB_D, q, jnp.zeros_like(q)), jnp.where(lane_q >= HB_D, q, jnp.zeros_like(q)))
    row = _iota((tq, tq), 0)
    colm = _iota((tq, tq), 1)

    def step(j, carry, masked):
        r = pl.multiple_of(j * tq, tq)
        kj = k_ref[pl.ds(r, tq), :]
        vj = v_ref[pl.ds(r, tq), :]
        fj = f_ref[j]
        out = []
        for hh in range(2):
            m, l, acc = carry[hh]
            s = _dot_nt(q_heads[hh], kj) - fj[hh:hh + 1, :]
            if masked:
                s = jnp.where(colm <= row, s, NEG_INF)
            m_new = jnp.maximum(m, jnp.max(s, axis=-1, keepdims=True))
            alpha = jnp.exp(m - m_new)
            p = jnp.exp(s - m_new)
            l = alpha * l + jnp.sum(p, axis=-1, keepdims=True)
            acc = alpha * acc + _dot(p.astype(BF16), vj)
            out.append((m_new, l, acc))
        return tuple(out)

    init = tuple((jnp.full((tq, 1), -1e30, F32), jnp.zeros((tq, 1), F32), jnp.zeros((tq, LANES), F32))
                 for _ in range(2))
    carry = lax.fori_loop(0, qi, lambda j, c: step(j, c, False), init)
    (_, l0, a0), (_, l1, a1) = step(qi, carry, True)
    o = jnp.where(_iota((tq, LANES), 1) < HB_D, a0 / l0, a1 / l1)
    o_ref[...] = o.astype(o_ref.dtype)


def _fox_attn_prompt(qb, kb, vb, ft):
    bg, s_len, _ = qb.shape
    tq = _tile(s_len, 512)
    nk = s_len // tq
    pairs = HB_H // 2
    f5 = ft.reshape(bg, pairs, 2, nk, tq).transpose(0, 1, 3, 2, 4)
    return pl.pallas_call(
        functools.partial(_foxattn_kernel, tq=tq),
        out_shape=jax.ShapeDtypeStruct((bg, s_len, B_W), BF16),
        grid=(bg, pairs, nk),
        in_specs=[pl.BlockSpec((None, tq, LANES), lambda b, p, i: (b, i, p)),
                  pl.BlockSpec((None, s_len, LANES), lambda b, p, i: (b, 0, p)),
                  pl.BlockSpec((None, s_len, LANES), lambda b, p, i: (b, 0, p)),
                  pl.BlockSpec((None, None, nk, 2, tq), lambda b, p, i: (b, p, 0, 0, 0))],
        out_specs=pl.BlockSpec((None, tq, LANES), lambda b, p, i: (b, i, p)),
        compiler_params=_params(("parallel", "parallel", "arbitrary")),
        name="fox_attn_prompt",
    )(qb, kb, vb, f5)


def _foxsample_kernel(pt_ref, q_ref, kn_ref, vn_ref, lfn_ref, *rest, n_new):
    ps = PAGES_PER_STEP
    k_pages = rest[0:ps]
    v_pages = rest[ps:2 * ps]
    f_pages = rest[2 * ps:3 * ps]
    o_ref, q_scr, m_scr, l_scr, acc_scr, fc_scr = rest[3 * ps:]
    g = pl.program_id(1)
    rows = n_new * HB_H
    hmask = (_iota((HB_H, B_W), 1) // HB_D) == _iota((HB_H, B_W), 0)

    @pl.when(g == 0)
    def _():
        for t in range(n_new):
            qt = jnp.broadcast_to(q_ref[t:t + 1, :], (HB_H, B_W))
            q_scr[t * HB_H:(t + 1) * HB_H, :] = jnp.where(hmask, qt, 0.0)
        m_scr[...] = jnp.full_like(m_scr, -1e30)
        l_scr[...] = jnp.zeros_like(l_scr)
        acc_scr[...] = jnp.zeros_like(acc_scr)
        fc_scr[...] = jnp.zeros_like(fc_scr)

    tri_u = (_iota((PAGE, PAGE), 0) <= _iota((PAGE, PAGE), 1)).astype(F32)

    def update(s, v_b):
        m = m_scr[...]
        m_new = jnp.maximum(m, jnp.max(s, axis=-1, keepdims=True))
        alpha = jnp.exp(m - m_new)
        p = jnp.exp(s - m_new)
        l_scr[...] = alpha * l_scr[...] + jnp.sum(p, axis=-1, keepdims=True)
        acc_scr[...] = alpha * acc_scr[...] + _dot(p.astype(BF16), v_b)
        m_scr[...] = m_new

    def cum_forget(lf_t):
        f_loc = _dot_hi(lf_t, tri_u) + fc_scr[...]
        fc_scr[...] = f_loc[:, PAGE - 1:PAGE]
        return f_loc

    f_parts = [cum_forget(f_pages[r][...]) for r in range(ps)]
    f_cat = jnp.concatenate(f_parts, axis=1)
    bias = jnp.concatenate([f_cat] * n_new, axis=0)
    k_cat = jnp.concatenate([k_pages[r][...].astype(BF16) for r in range(ps)], axis=0)
    v_cat = jnp.concatenate([v_pages[r][...].astype(BF16) for r in range(ps)], axis=0)
    q_rows = q_scr[...].astype(BF16)
    update(_dot_nt(q_rows, k_cat) - bias, v_cat)

    @pl.when(g == pl.num_programs(1) - 1)
    def _():
        f_new = cum_forget(lfn_ref[...])
        s = _dot_nt(q_rows, kn_ref[...].astype(BF16)) - jnp.concatenate([f_new] * n_new, axis=0)
        visible = _iota((rows, PAGE), 1) <= (_iota((rows, PAGE), 0) // HB_H)
        update(jnp.where(visible, s, NEG_INF), vn_ref[...].astype(BF16))
        o = acc_scr[...] / l_scr[...]
        out = jnp.zeros((SUBLANES, B_W), F32)
        out_row = _iota((SUBLANES, B_W), 0)
        for t in range(n_new):
            ot = jnp.where(hmask, o[t * HB_H:(t + 1) * HB_H, :], 0.0)
            out = jnp.where(out_row == t, jnp.sum(ot, axis=0, keepdims=True), out)
        o_ref[...] = out.astype(o_ref.dtype)


def _fox_attn_sample(layer, q_new, k_new, v_new, lf_new, cache_k, cache_v, cache_lf_t, page_table):
    db, n_new, _ = q_new.shape
    n_pages = page_table.shape[1]
    n_pool = cache_k.shape[0] // DEPTH
    ps = PAGES_PER_STEP
    assert n_pages % ps == 0 and n_new <= SUBLANES
    qp = jnp.pad(q_new, ((0, 0), (0, SUBLANES - n_new), (0, 0)))
    knp = jnp.pad(k_new, ((0, 0), (0, PAGE - n_new), (0, 0)))
    vnp = jnp.pad(v_new, ((0, 0), (0, PAGE - n_new), (0, 0)))
    lfp = jnp.pad(lf_new.transpose(0, 2, 1), ((0, 0), (0, 0), (0, PAGE - n_new)))
    base = layer * n_pool

    def page_spec(r, rows, cols):
        return pl.BlockSpec((None, rows, cols), lambda b, g, pt: (base + pt[b, g * ps + r], 0, 0))

    in_specs = [pl.BlockSpec((None, SUBLANES, B_W), lambda b, g, pt: (b, 0, 0)),
                pl.BlockSpec((None, PAGE, B_W), lambda b, g, pt: (b, 0, 0)),
                pl.BlockSpec((None, PAGE, B_W), lambda b, g, pt: (b, 0, 0)),
                pl.BlockSpec((None, HB_H, PAGE), lambda b, g, pt: (b, 0, 0))]
    in_specs += [page_spec(r, PAGE, B_W) for r in range(ps)]
    in_specs += [page_spec(r, PAGE, B_W) for r in range(ps)]
    in_specs += [page_spec(r, HB_H, PAGE) for r in range(ps)]
    rows = n_new * HB_H
    return pl.pallas_call(
        functools.partial(_foxsample_kernel, n_new=n_new),
        out_shape=jax.ShapeDtypeStruct((db, SUBLANES, B_W), BF16),
        grid_spec=pltpu.PrefetchScalarGridSpec(
            num_scalar_prefetch=1,
            grid=(db, n_pages // ps),
            in_specs=in_specs,
            out_specs=pl.BlockSpec((None, SUBLANES, B_W), lambda b, g, pt: (b, 0, 0)),
            scratch_shapes=[pltpu.VMEM((rows, B_W), F32), pltpu.VMEM((rows, 1), F32),
                            pltpu.VMEM((rows, 1), F32), pltpu.VMEM((rows, B_W), F32),
                            pltpu.VMEM((HB_H, 1), F32)]),
        compiler_params=_params(("parallel", "arbitrary")),
        name="fox_attn_sample",
    )(page_table, qp, knp, vnp, lfp, *([cache_k] * ps), *([cache_v] * ps), *([cache_lf_t] * ps))


def _foxnative_kernel(pt_ref, q_ref, kn_ref, vn_ref, lfn_ref, *rest, n_new):
    ps = PAGES_PER_STEP
    k_pages = rest[0:ps]
    v_pages = rest[ps:2 * ps]
    f_pages = rest[2 * ps:3 * ps]
    o_ref, m_scr, l_scr, acc_scr, fc_scr = rest[3 * ps:]
    g = pl.program_id(1)
    cols = n_new * HB_H

    @pl.when(g == 0)
    def _():
        m_scr[...] = jnp.full_like(m_scr, -1e30)
        l_scr[...] = jnp.zeros_like(l_scr)
        acc_scr[...] = jnp.zeros_like(acc_scr)
        fc_scr[...] = jnp.zeros_like(fc_scr)

    q2 = q_ref[...].astype(BF16)
    expand = ((_iota((HB_H, cols), 1) % HB_H) == _iota((HB_H, cols), 0)).astype(F32)

    def blocks(parts, n, causal):
        rows = n * HB_H
        tri = (_iota((n, n), 1) <= _iota((n, n), 0)).astype(F32)
        r_i = _iota((rows, cols), 0)
        c_i = _iota((rows, cols), 1)
        keep = (r_i % HB_H) == (c_i % HB_H)
        if causal:
            keep = keep & ((r_i // HB_H) <= (c_i // HB_H))
        f_loc = [_dot_hi(tri, lf) for _, _, lf in parts]
        off = fc_scr[...]
        scores = []
        for (k3, _, _), fl in zip(parts, f_loc):
            fe = _dot_hi(fl + off, expand)
            off = off + fl[n - 1:n]
            bias = jnp.broadcast_to(fe[:, None, :], (n, HB_H, cols)).reshape(rows, cols)
            s = _dot_nt(k3.reshape(rows, HB_D).astype(BF16), q2) - bias
            scores.append(jnp.where(keep, s, NEG_INF))
        fc_scr[...] = off
        m = m_scr[...]
        m_new = functools.reduce(jnp.maximum, [jnp.max(s, axis=0, keepdims=True) for s in scores] + [m])
        alpha = jnp.exp(m - m_new)
        l_new = alpha * l_scr[...]
        acc = alpha * acc_scr[...]
        for (_, v3, _), s in zip(parts, scores):
            p = jnp.exp(s - m_new)
            l_new = l_new + jnp.sum(p, axis=0, keepdims=True)
            acc = acc + _dot_tn(v3.reshape(rows, HB_D).astype(BF16), p.astype(BF16))
        l_scr[...] = l_new
        acc_scr[...] = acc
        m_scr[...] = m_new

    blocks([(k_pages[r][...], v_pages[r][...], f_pages[r][...]) for r in range(ps)], PAGE, False)

    @pl.when(g == pl.num_programs(1) - 1)
    def _():
        blocks([(kn_ref[...], vn_ref[...], lfn_ref[...])], SUBLANES, True)
        o_ref[...] = acc_scr[...] / l_scr[...]


def _fox_attn_sample_native(layer, q_new, k_new, v_new, lf_new, cache_k, cache_v, cache_lf, page_table):
    db, n_new, _ = q_new.shape
    n_pages = page_table.shape[1]
    ps = PAGES_PER_STEP
    assert n_pages % ps == 0 and n_new <= SUBLANES
    cols = n_new * HB_H
    pad_t = ((0, 0), (0, SUBLANES - n_new), (0, 0), (0, 0))
    q2 = q_new.reshape(db, cols, HB_D)
    kn = jnp.pad(k_new.reshape(db, n_new, HB_H, HB_D), pad_t)
    vn = jnp.pad(v_new.reshape(db, n_new, HB_H, HB_D), pad_t)
    lfn = jnp.pad(lf_new, pad_t[:3])

    def kv_spec(r):
        return pl.BlockSpec((None, None, PAGE, HB_H, HB_D), lambda b, g, pt: (layer, pt[b, g * ps + r], 0, 0, 0))

    def f_spec(r):
        return pl.BlockSpec((None, None, PAGE, HB_H), lambda b, g, pt: (layer, pt[b, g * ps + r], 0, 0))

    in_specs = [pl.BlockSpec((None, cols, HB_D), lambda b, g, pt: (b, 0, 0)),
                pl.BlockSpec((None, SUBLANES, HB_H, HB_D), lambda b, g, pt: (b, 0, 0, 0)),
                pl.BlockSpec((None, SUBLANES, HB_H, HB_D), lambda b, g, pt: (b, 0, 0, 0)),
                pl.BlockSpec((None, SUBLANES, HB_H), lambda b, g, pt: (b, 0, 0))]
    in_specs += [kv_spec(r) for r in range(ps)] + [kv_spec(r) for r in range(ps)] + [f_spec(r) for r in range(ps)]
    out_t = pl.pallas_call(
        functools.partial(_foxnative_kernel, n_new=n_new),
        out_shape=jax.ShapeDtypeStruct((db, HB_D, cols), F32),
        grid_spec=pltpu.PrefetchScalarGridSpec(
            num_scalar_prefetch=1,
            grid=(db, n_pages // ps),
            in_specs=in_specs,
            out_specs=pl.BlockSpec((None, HB_D, cols), lambda b, g, pt: (b, 0, 0)),
            scratch_shapes=[pltpu.VMEM((1, cols), F32), pltpu.VMEM((1, cols), F32),
                            pltpu.VMEM((HB_D, cols), F32), pltpu.VMEM((1, HB_H), F32)]),
        compiler_params=_params(("parallel", "arbitrary")),
        name="fox_attn_sample",
    )(page_table, q2, kn, vn, lfn, *([cache_k] * ps), *([cache_v] * ps), *([cache_lf] * ps))
    return out_t.transpose(0, 2, 1).reshape(db, n_new, B_W)


def _merge_kernel(oa_ref, ob_ref, oc_ref, ga_ref, gb_ref, gc_ref, x_ref, gt1_ref, sh2_ref, sc2_ref,
                  wa_ref, wb_ref, wc_ref, wo_ref, nf_ref, wq_ref, keys_ref,
                  x1_ref, h2t_ref, st_ref):
    merged = (_sigmoid(ga_ref[...]) * _dot(oa_ref[...], wa_ref[...])
              + _sigmoid(gb_ref[...]) * _dot(ob_ref[...], wb_ref[...])
              + _sigmoid(gc_ref[...]) * _dot(oc_ref[...], wc_ref[...]))
    x1 = x_ref[...] + gt1_ref[...] * _dot(merged.astype(BF16), wo_ref[...])
    x1_ref[...] = x1
    r = lax.rsqrt(jnp.mean(x1 * x1, axis=-1, keepdims=True) + EPS)
    h2 = (x1 * r) * nf_ref[...] * (1.0 + sc2_ref[...]) + sh2_ref[...]
    h2t_ref[...] = h2.T.astype(BF16)
    qb = _dot(h2.astype(BF16), wq_ref[...]).astype(BF16)
    half = P_QD // 2
    for h in range(P_H):
        for p in range(2):
            lo = (h * 2 + p) * half
            st_ref[lo:lo + half, :] = _dot_nt(keys_ref[p], qb[:, lo:lo + half])


def _merge(oa, ob, oc, z3, x, mod, w_a, w_b, w_c, w_o, norm_ffn, w_q, keys):
    bg, lg, _ = x.shape
    tm = _tile(lg, 256)
    nl = lg // tm
    t_all = bg * lg

    def tok(width):
        return pl.BlockSpec((None, tm, width), lambda b, i: (b, i, 0))

    def gate(k):
        return pl.BlockSpec((None, tm, D), lambda b, i: (b, i, OFF_GATE // D + k))

    def full(shape):
        return pl.BlockSpec(shape, lambda b, i: (0,) * len(shape))

    return pl.pallas_call(
        _merge_kernel,
        out_shape=(jax.ShapeDtypeStruct((bg, lg, D), F32),
                   jax.ShapeDtypeStruct((D, t_all), BF16),
                   jax.ShapeDtypeStruct((P_H * P_QD, t_all), F32)),
        grid=(bg, nl),
        in_specs=[tok(A_W), tok(B_W), tok(C_K), gate(0), gate(1), gate(2), tok(D),
                  _mod_spec(mod, tm, 2), _mod_spec(mod, tm, 3), _mod_spec(mod, tm, 4),
                  full((A_W, D)), full((B_W, D)), full((C_K, D)), full((D, D)), full((1, D)),
                  full((D, P_H * P_QD)), full((2, P_NK, P_QD // 2))],
        out_specs=(tok(D),
                   pl.BlockSpec((D, tm), lambda b, i: (0, b * nl + i)),
                   pl.BlockSpec((P_H * P_QD, tm), lambda b, i: (0, b * nl + i))),
        compiler_params=_params(("parallel", "parallel")),
        name="merge",
    )(oa, ob, oc, z3, z3, z3, x, mod, mod, mod, w_a, w_b, w_c, w_o, norm_ffn.reshape(1, D), w_q, keys)


def _topk_kernel(st_ref, o_ref):
    tt = st_ref.shape[1]

    def top_rows(s):
        rows = []
        cur = s
        for it in range(P_TOPK + 1):
            m = jnp.max(cur, axis=0, keepdims=True)
            rows.append(m)
            if it < P_TOPK:
                cur = jnp.where(cur == m, NEG_INF, cur)
        return rows

    rank = _iota((P_TOPK, tt), 0)
    rank8 = _iota((SUBLANES, tt), 0)
    thr_rows, nrm_rows = [], []
    for h in range(P_H):
        lo = h * P_QD
        v1 = top_rows(st_ref[lo:lo + P_NK, :])
        v2 = top_rows(st_ref[lo + P_NK:lo + 2 * P_NK, :])
        v1s = jnp.concatenate(v1[:P_TOPK], axis=0)
        v2s = jnp.concatenate(v2[:P_TOPK], axis=0)
        groups = [jnp.where(rank8 == 0, v1[P_TOPK] + v2[0], jnp.where(rank8 == 1, v1[0] + v2[P_TOPK], NEG_INF))]
        for b in range(3):
            groups.append(jnp.where(rank < P_TOPK // (b + 1), v1s + v2[b], NEG_INF))
        for a in range(4):
            nb = P_TOPK // (a + 1)
            n_rows = P_TOPK if nb > SUBLANES else SUBLANES
            rk = _iota((n_rows, tt), 0)
            ok = jnp.where(rk >= 3, rk, nb) < nb
            groups.append(jnp.where(ok, v2s[:n_rows] + v1[a], NEG_INF))
        m_top = None
        z = None
        for it in range(P_TOPK):
            m = functools.reduce(jnp.maximum, [jnp.max(gp, axis=0, keepdims=True) for gp in groups])
            if it == 0:
                m_top = m
                z = jnp.ones_like(m)
            else:
                z = z + jnp.exp(m - m_top)
            groups = [jnp.where(gp == m, NEG_INF, gp) for gp in groups]
        m_next = functools.reduce(jnp.maximum, [jnp.max(gp, axis=0, keepdims=True) for gp in groups])
        thr_rows.append(0.5 * m + 0.5 * m_next)
        nrm_rows.append(-(m_top + jnp.log(z)))
    o_ref[...] = jnp.concatenate(thr_rows + nrm_rows, axis=0)


def _topk(st):
    t_all = st.shape[1]
    tt = _tile(t_all, 256)
    return pl.pallas_call(
        _topk_kernel,
        out_shape=jax.ShapeDtypeStruct((2 * P_H, t_all), F32),
        grid=(t_all // tt,),
        in_specs=[pl.BlockSpec((P_H * P_QD, tt), lambda t: (0, t))],
        out_specs=pl.BlockSpec((2 * P_H, tt), lambda t: (0, t)),
        compiler_params=_params(("parallel",)),
        name="peer_topk",
    )(st)


def _peer_kernel(st_ref, stat_ref, h2t_ref, u_ref, vt_ref, x1_ref, gt2_ref, o_ref,
                 acc_scr, e1_scr, tau_scr, e2_scr, s2_scr, ht0_scr, ht1_scr, wa0_scr, wa1_scr, *, ti, n_tiles):
    s = pl.program_id(2)
    tt = st_ref.shape[1]

    @pl.when(s == 0)
    def _():
        acc_scr[...] = jnp.zeros_like(acc_scr)
        ht0_scr[...] = jnp.zeros_like(ht0_scr)
        ht1_scr[...] = jnp.zeros_like(ht1_scr)
        wa0_scr[...] = jnp.zeros_like(wa0_scr)
        wa1_scr[...] = jnp.zeros_like(wa1_scr)
        for h in range(P_H):
            lo = h * P_QD
            hr = slice(h * P_NK, (h + 1) * P_NK)
            for lt in range(tt // LANES):
                ls = slice(lt * LANES, (lt + 1) * LANES)
                s1 = st_ref[lo:lo + P_NK, ls]
                s2 = st_ref[lo + P_NK:lo + 2 * P_NK, ls]
                mx2 = jnp.max(s2, axis=0, keepdims=True)
                s2_scr[lt, hr, :] = s2
                e2_scr[lt, hr, :] = jnp.exp(s2 - mx2)
                e1_scr[hr, ls] = 0.5 * jnp.exp(s1 + (stat_ref[P_H + h:P_H + h + 1, ls] + mx2))
                tau_scr[hr, ls] = stat_ref[h:h + 1, ls] - s1

    c0 = math.sqrt(2.0 / math.pi)
    c1 = c0 * 0.044715
    tile_b = jnp.clip(s - 1, 0, n_tiles - 1)

    def stages(ht_w, ht_r, wa_w, wa_r):
        te = ti * P_NK
        n_lt = tt // LANES
        per_group = 2 if n_lt % 2 == 0 else 1
        n_p = ti * (n_lt // per_group)
        ka, kc = n_p // 2, 2
        ma, mc = te // 2, D // (n_p // 2)
        wka, wkc = D // ka, te // kc

        def matmul_pieces(pi):
            mh, kq = pi // ka, pi % ka
            part = _dot(u_ref[mh * ma:(mh + 1) * ma, kq * wka:(kq + 1) * wka], h2t_ref[kq * wka:(kq + 1) * wka, :])
            for lt in range(n_lt):
                piece = part[:, lt * LANES:(lt + 1) * LANES]
                if kq == 0:
                    ht_w[lt, mh * ma:(mh + 1) * ma, :] = piece
                else:
                    ht_w[lt, mh * ma:(mh + 1) * ma, :] += piece
            mq, kh = pi // kc, pi % kc
            acc_scr[mq * mc:(mq + 1) * mc, :] += _dot(vt_ref[mq * mc:(mq + 1) * mc, kh * wkc:(kh + 1) * wkc],
                                                     wa_r[kh * wkc:(kh + 1) * wkc, :])

        for ii in range(ti):
            i_row = tile_b * ti + ii
            tau_rows = [tau_scr[pl.ds(h * P_NK + i_row, 1), :] for h in range(P_H)]
            e1_rows = [e1_scr[pl.ds(h * P_NK + i_row, 1), :] for h in range(P_H)]
            for lt in range(n_lt):
                if lt % per_group == 0:
                    matmul_pieces(ii * (n_lt // per_group) + lt // per_group)
                ls = slice(lt * LANES, (lt + 1) * LANES)
                tau_b = [jnp.broadcast_to(tau_rows[h][:, ls], (PEER_JB, LANES)) for h in range(P_H)]
                e1_b = [jnp.broadcast_to(e1_rows[h][:, ls], (PEER_JB, LANES)) for h in range(P_H)]
                for jb in range(P_NK // PEER_JB):
                    j0 = jb * PEER_JB
                    hs = ht_r[lt, ii * P_NK + j0:ii * P_NK + j0 + PEER_JB, :]
                    act = hs * (1.0 + jnp.tanh(hs * (c0 + c1 * (hs * hs))))
                    w = None
                    for h in range(P_H):
                        jr = slice(h * P_NK + j0, h * P_NK + j0 + PEER_JB)
                        wh = jnp.where(s2_scr[lt, jr, :] >= tau_b[h], e2_scr[lt, jr, :] * e1_b[h], 0.0)
                        w = wh if w is None else w + wh
                    wa_w[ii * P_NK + j0:ii * P_NK + j0 + PEER_JB, ls] = (w * act).astype(BF16)

    @pl.when(s % 2 == 0)
    def _():
        stages(ht0_scr, ht1_scr, wa1_scr, wa0_scr)

    @pl.when(s % 2 == 1)
    def _():
        stages(ht1_scr, ht0_scr, wa0_scr, wa1_scr)

    @pl.when(s == pl.num_programs(2) - 1)
    def _():
        o_ref[...] = x1_ref[...] + gt2_ref[...] * acc_scr[...].T


def _peer(st, stats, h2t, u_b, vt_b, x1, mod):
    bg, lg, _ = x1.shape
    tt = _tile(lg, 512)
    nl = lg // tt
    ti = 4
    te = ti * P_NK
    n_tiles = N_EXP // te
    return pl.pallas_call(
        functools.partial(_peer_kernel, ti=ti, n_tiles=n_tiles),
        out_shape=jax.ShapeDtypeStruct((bg, lg, D), F32),
        grid=(bg, nl, n_tiles + 2),
        in_specs=[pl.BlockSpec((P_H * P_QD, tt), lambda b, i, s: (0, b * nl + i)),
                  pl.BlockSpec((2 * P_H, tt), lambda b, i, s: (0, b * nl + i)),
                  pl.BlockSpec((D, tt), lambda b, i, s: (0, b * nl + i)),
                  pl.BlockSpec((te, D), lambda b, i, s: (jnp.minimum(s, n_tiles - 1), 0)),
                  pl.BlockSpec((D, te), lambda b, i, s: (0, jnp.clip(s - 2, 0, n_tiles - 1))),
                  pl.BlockSpec((None, tt, D), lambda b, i, s: (b, i, 0)),
                  _mod_spec(mod, tt, 5)],
        out_specs=pl.BlockSpec((None, tt, D), lambda b, i, s: (b, i, 0)),
        scratch_shapes=[pltpu.VMEM((D, tt), F32),
                        pltpu.VMEM((P_H * P_NK, tt), F32), pltpu.VMEM((P_H * P_NK, tt), F32),
                        pltpu.VMEM((tt // LANES, P_H * P_NK, LANES), F32),
                        pltpu.VMEM((tt // LANES, P_H * P_NK, LANES), F32),
                        pltpu.VMEM((tt // LANES, te, LANES), F32), pltpu.VMEM((tt // LANES, te, LANES), F32),
                        pltpu.VMEM((te, tt), BF16), pltpu.VMEM((te, tt), BF16)],
        compiler_params=_params(("parallel", "parallel", "arbitrary")),
        name="peer_experts",
    )(st, stats, h2t, u_b, vt_b, x1, mod)


def _pack_w_in(w):
    pad = jnp.zeros((D, LANES - 8), w.dtype)
    o_fb = 2 * A_W + 2 * A_W + 3 * B_W
    o_c = o_fb + HB_H
    o_ab = o_c + 3 * C_K
    o_zc = o_ab + 2 * HC_H
    o_gate = o_zc + C_K
    packed = jnp.concatenate([
        w[:, 0:4 * A_W], w[:, o_gate:o_gate + 3 * D], w[:, 4 * A_W:o_fb], w[:, o_c:o_ab],
        w[:, o_zc:o_gate], w[:, o_fb:o_c], pad, w[:, o_ab:o_zc], pad], axis=1)
    assert packed.shape[1] == NP
    return packed.astype(BF16)


def _layer(l, x, mod, w, sample):
    bg, lg, _ = x.shape
    z = _in_proj(x, mod, w["norm_mix"][l], w["w_in"][l])
    if sample is None:
        z3 = z
        bs, l_seq = bg, lg
        chunk = LIN_C
        hgrn0 = jnp.zeros((bs, HA_H, HA_DK, HA_DV), F32)
        gdn0 = jnp.zeros((bs, HC_H, HC_DK, HC_DV), F32)
        conv0 = jnp.zeros((bs, SUBLANES, CONV_CH), F32)
    else:
        bs, l_seq = sample["db"], sample["t"]
        chunk = SUBLANES
        z3 = jnp.pad(z.reshape(bs, l_seq, NP), ((0, 0), (0, SUBLANES - l_seq), (0, 0)))
        hgrn0 = sample["state_hgrn"][l]
        gdn0 = sample["state_gdn"][l]
        conv0 = jnp.pad(sample["state_conv"][l], ((0, 0), (SUBLANES - (CONV_W - 1), 0), (0, 0)))

    o_a, s_hgrn = _hgrn(z3, w["hgrn_lb_logits"], w["hgrn_norm"][l], hgrn0, layer=l, chunk=chunk, l_valid=l_seq)
    o_c, s_gdn = _gdn(z3, w["gdn_conv_w"][l], w["gdn_a_log"][l], w["gdn_dt_bias"][l], w["gdn_norm"][l],
                      conv0, gdn0, chunk=chunk, l_valid=l_seq)
    zseq = z.reshape(bs, l_seq, NP)
    conv_new = zseq[:, l_seq - (CONV_W - 1):, OFF_QC:OFF_QC + CONV_CH]

    prep = _fox_prep(z, w["fox_q_norm"][l], w["fox_k_norm"][l], w["fox_b_f"][l], cumsum=sample is None)
    qn, kn, knb, vnb, lf = prep[:5]
    v_b = z[:, :, OFF_VB:OFF_VB + B_W]
    if sample is None:
        o_b = _fox_attn_prompt(qn, knb, vnb, prep[5])
    else:
        o_b = _fox_attn_sample_native(
            l, qn.astype(F32).reshape(bs, l_seq, B_W), kn.reshape(bs, l_seq, B_W), v_b.reshape(bs, l_seq, B_W),
            lf[0, :, :HB_H].reshape(bs, l_seq, HB_H), sample["cache_k"], sample["cache_v"], sample["cache_lf"],
            sample["page_table"])
        o_b = o_b.astype(BF16).reshape(bg, lg, B_W)
        o_a = o_a[:, :l_seq].reshape(bg, lg, A_W)
        o_c = o_c[:, :l_seq].reshape(bg, lg, C_K)

    x1, h2t, st = _merge(o_a, o_b, o_c, z, x, mod, w["w_br_a"][l], w["w_br_b"][l], w["w_br_c"][l],
                         w["w_out"][l], w["norm_ffn"][l], w["peer_w_q"][l], w["peer_keys"][l])
    stats = _topk(st)
    x2 = _peer(st, stats, h2t, w["peer_u"][l], w["peer_vt"][l], x1, mod)

    k_leaf = kn.reshape(bs, l_seq, HB_H, HB_D)
    v_leaf = v_b.reshape(bs, l_seq, HB_H, HB_D)
    lf_leaf = lf[:, :, :HB_H].reshape(bs, l_seq, HB_H)
    return x2, (k_leaf, v_leaf, lf_leaf, s_hgrn, s_gdn, conv_new)


def _trunk(x, mods, w, sample):
    leaves = [[] for _ in range(6)]
    for l in range(DEPTH):
        x, st = _layer(l, x, mods[l], w, sample)
        for lst, s in zip(leaves, st):
            lst.append(s)
    return x, [jnp.stack(v) for v in leaves]


def kernel(x_prompt, x_sample, c_prompt, c_sample, cache_fox_k, cache_fox_v, cache_fox_logf, page_table,
           state_hgrn, state_gdn, state_gdn_conv, w_ada, b_ada, norm_mix, norm_ffn, w_in, hgrn_lb_logits,
           hgrn_norm, fox_b_f, fox_q_norm, fox_k_norm, gdn_conv_w, gdn_a_log, gdn_dt_bias, gdn_norm,
           w_br_a, w_br_b, w_br_c, w_out, peer_w_q, peer_keys, peer_u, peer_v):
    bp = x_prompt.shape[0]
    db, t_new, _ = x_sample.shape
    n_pool = cache_fox_k.shape[1]

    w = {
        "norm_mix": norm_mix, "norm_ffn": norm_ffn, "hgrn_lb_logits": hgrn_lb_logits, "hgrn_norm": hgrn_norm,
        "fox_b_f": fox_b_f, "fox_q_norm": fox_q_norm, "fox_k_norm": fox_k_norm, "gdn_conv_w": gdn_conv_w,
        "gdn_a_log": gdn_a_log, "gdn_dt_bias": gdn_dt_bias, "gdn_norm": gdn_norm,
        "w_in": [_pack_w_in(w_in[l]) for l in range(DEPTH)],
        "w_br_a": w_br_a.astype(BF16), "w_br_b": w_br_b.astype(BF16), "w_br_c": w_br_c.astype(BF16),
        "w_out": w_out.astype(BF16), "peer_w_q": peer_w_q.astype(BF16), "peer_keys": peer_keys.astype(BF16),
        "peer_u": peer_u.astype(BF16), "peer_vt": peer_v.astype(BF16).transpose(0, 2, 1),
    }

    n_c = bp + db
    c_all = jnp.pad(jnp.concatenate([c_prompt, c_sample], axis=0), ((0, (-n_c) % SUBLANES), (0, 0)))
    mod = _ada(c_all, w_ada, b_ada)
    mods_p = [mod[l, :bp].reshape(bp, 1, 6 * D) for l in range(DEPTH)]
    mods_s = [jnp.repeat(mod[l, bp:n_c], t_new, axis=0).reshape(1, db * t_new, 6 * D) for l in range(DEPTH)]

    sample = {
        "db": db, "t": t_new, "page_table": page_table,
        "state_hgrn": state_hgrn, "state_gdn": state_gdn, "state_conv": state_gdn_conv,
        "cache_k": cache_fox_k, "cache_v": cache_fox_v, "cache_lf": cache_fox_logf,
    }

    y_p, leaves_p = _trunk(x_prompt, mods_p, w, None)
    y_s, leaves_s = _trunk(x_sample.reshape(1, db * t_new, D), mods_s, w, sample)
    return (y_p, y_s.reshape(db, t_new, D), *leaves_p, *leaves_s)
```

```python
import functools
import math

import jax
import jax.numpy as jnp
from jax import lax
from jax.experimental import pallas as pl
from jax.experimental.pallas import tpu as pltpu

F32 = jnp.float32
BF16 = jnp.bfloat16
HI = lax.Precision.HIGHEST
NEG_INF = float("-inf")

D = 1024
DEPTH = 2
HA_H, HA_DK, HA_DV = 4, 128, 128
HB_H, HB_D = 8, 64
HC_H, HC_DK, HC_DV = 4, 128, 128
CONV_W = 4
P_H, P_TOPK, P_NK, P_QD = 8, 16, 128, 256
N_EXP = P_NK * P_NK
EPS = 1e-6
A_W = HA_H * HA_DK
B_W = HB_H * HB_D
C_K = HC_H * HC_DK
CONV_CH = 3 * C_K
PAGE = 128

LANES = 128
SUBLANES = 8
VMEM_LIMIT = 52 * 1024 * 1024

OFF_FA, OFF_QA, OFF_IA, OFF_GA = 0, 512, 1024, 1536
OFF_GATE = 2048
OFF_QB, OFF_KB, OFF_VB = 5120, 5632, 6144
OFF_QC, OFF_KC, OFF_VC = 6656, 7168, 7680
OFF_ZC = 8192
OFF_FB = 8704
OFF_AB = 8832
NP = 8960

LIN_C = 64
HGRN_SUB = 16
PEER_JB = 32
PEER_TI = 4
PAGES_PER_STEP = 8


def _dot(a, b):
    return jnp.dot(a, b, preferred_element_type=F32)


def _dot_hi(a, b):
    return jnp.dot(a, b, preferred_element_type=F32, precision=HI)


def _dot_nt(a, b):
    return lax.dot_general(a, b, (((1,), (1,)), ((), ())), preferred_element_type=F32)


def _dot_tn(a, b, precision=None):
    return lax.dot_general(a, b, (((0,), (0,)), ((), ())), preferred_element_type=F32, precision=precision)


def _iota(shape, dim):
    return lax.broadcasted_iota(jnp.int32, shape, dim)


def _sigmoid(x):
    return jax.nn.sigmoid(x)


def _silu(x):
    return x * jax.nn.sigmoid(x)


def _softplus(x):
    return jnp.maximum(x, 0.0) + jnp.log1p(jnp.exp(-jnp.abs(x)))


def _log_sigmoid(x):
    return jnp.minimum(x, 0.0) - jnp.log1p(jnp.exp(-jnp.abs(x)))


def _params(sem):
    return pltpu.CompilerParams(dimension_semantics=sem, vmem_limit_bytes=VMEM_LIMIT)


def _chunk_start(c, c_len):
    return c * c_len if isinstance(c, int) else pl.multiple_of(c * c_len, c_len)


def _for_chunks(n_chunks, body, unroll=False):
    if n_chunks == 1 or unroll:
        for c in range(n_chunks):
            body(c, 0)
    else:
        lax.fori_loop(0, n_chunks, body, 0)


def _tile(n, pref):
    t = min(n, pref)
    assert n % t == 0, (n, pref)
    return t


def _ada_kernel(c_ref, w_ref, b_ref, o_ref):
    sc = _silu(c_ref[...]).astype(BF16)
    o_ref[...] = _dot(sc, w_ref[...].astype(BF16)) + b_ref[...]


def _ada(c_all, w_ada, b_ada):
    rows = c_all.shape[0]
    tn = 768
    return pl.pallas_call(
        _ada_kernel,
        out_shape=jax.ShapeDtypeStruct((DEPTH, rows, 6 * D), F32),
        grid=(DEPTH, 6 * D // tn),
        in_specs=[pl.BlockSpec((rows, D), lambda l, n: (0, 0)),
                  pl.BlockSpec((None, D, tn), lambda l, n: (l, 0, n)),
                  pl.BlockSpec((None, 1, tn), lambda l, n: (l, 0, n))],
        out_specs=pl.BlockSpec((None, rows, tn), lambda l, n: (l, 0, n)),
        compiler_params=_params(("parallel", "parallel")),
        name="ada",
    )(c_all, w_ada, b_ada.reshape(DEPTH, 1, 6 * D))


def _mod_spec(mod, tm, k):
    if mod.shape[1] == 1:
        return pl.BlockSpec((None, 1, D), lambda b, i, *_: (b, 0, k))
    return pl.BlockSpec((None, tm, D), lambda b, i, *_: (b, i, k))


def _in_kernel(x_ref, sh_ref, sc_ref, g_ref, w_ref, z_ref, h_scr):
    @pl.when(pl.program_id(2) == 0)
    def _():
        x = x_ref[...]
        r = lax.rsqrt(jnp.mean(x * x, axis=-1, keepdims=True) + EPS)
        h = (x * r) * g_ref[...] * (1.0 + sc_ref[...]) + sh_ref[...]
        h_scr[...] = h.astype(BF16)

    z_ref[...] = _dot(h_scr[...], w_ref[...])


def _in_proj(x, mod, gain, w_packed):
    bg, lg, _ = x.shape
    tm = _tile(lg, 512)
    tn = 1280
    return pl.pallas_call(
        _in_kernel,
        out_shape=jax.ShapeDtypeStruct((bg, lg, NP), F32),
        grid=(bg, lg // tm, NP // tn),
        in_specs=[pl.BlockSpec((None, tm, D), lambda b, i, n: (b, i, 0)),
                  _mod_spec(mod, tm, 0), _mod_spec(mod, tm, 1),
                  pl.BlockSpec((1, D), lambda b, i, n: (0, 0)),
                  pl.BlockSpec((D, tn), lambda b, i, n: (0, n))],
        out_specs=pl.BlockSpec((None, tm, tn), lambda b, i, n: (b, i, n)),
        scratch_shapes=[pltpu.VMEM((tm, D), BF16)],
        compiler_params=_params(("parallel", "parallel", "arbitrary")),
        name="in_proj",
    )(x, mod, mod, gain.reshape(1, D), w_packed)


def _hgrn_kernel(fa_ref, qa_ref, ia_ref, ga_ref, lbl_ref, nw_ref, s0_ref, o_ref, sout_ref, s_scr,
                 *, layer, chunk, n_chunks, l_valid, l_padded):
    i = pl.program_id(2)
    c_len = chunk

    @pl.when(i == 0)
    def _():
        s_scr[...] = s0_ref[...]

    lg = lbl_ref[...]
    e = jnp.exp(lg - jnp.max(lg, axis=0, keepdims=True))
    p = e / jnp.sum(e, axis=0, keepdims=True)
    cs = p[0:1]
    for j in range(1, layer + 1):
        cs = cs + p[j:j + 1]
    lb = jnp.maximum(cs - p[0:1], 0.0)
    log_lb = jnp.log(lb)
    log1m_lb = jnp.log1p(-lb)

    sub = min(HGRN_SUB, c_len)
    tri = (_iota((c_len, c_len), 1) <= _iota((c_len, c_len), 0)).astype(F32)
    lane = _iota((sub, c_len), 1)
    row1 = _iota((c_len, 1), 0)
    ones_cv = jnp.ones((c_len, HA_DV), F32)

    def chunk_body(c, carry):
        r = _chunk_start(c, c_len)
        fa = fa_ref[pl.ds(r, c_len), :]
        qa = qa_ref[pl.ds(r, c_len), :]
        v = ia_ref[pl.ds(r, c_len), :]
        ga = ga_ref[pl.ds(r, c_len), :]

        b_ = log1m_lb + _log_sigmoid(fa)
        log_f = jnp.maximum(log_lb, b_) + jnp.log1p(jnp.exp(-jnp.abs(log_lb - b_)))
        k = (1.0 - lb) * _sigmoid(-fa)
        if l_valid < l_padded:
            valid = (i * (n_chunks * c_len) + r + row1) < l_valid
            log_f = jnp.where(valid, log_f, 0.0)
            k = jnp.where(valid, k, 0.0)
        q = _silu(qa)
        a_cum = _dot_hi(tri, log_f)

        blocks = []
        for bi in range(c_len // sub):
            lo, hi = bi * sub, (bi + 1) * sub
            if bi == 0:
                att_b = jnp.zeros((sub, c_len), F32)
            else:
                a_ref = a_cum[lo - 1:lo]
                qs = q[lo:hi] * jnp.exp(a_cum[lo:hi] - a_ref)
                ks = jnp.where(row1 < lo, k * jnp.exp(jnp.minimum(a_ref - a_cum, 0.0)), 0.0)
                att_b = _dot_nt(qs.astype(BF16), ks.astype(BF16))
            for s in range(lo, hi):
                r0 = (s // SUBLANES) * SUBLANES
                rel = a_cum[r0:hi] - a_cum[s:s + 1]
                dec = jnp.exp(jnp.where(row1[r0:hi] >= s, rel, NEG_INF))
                col = jnp.sum(q[r0:hi] * k[s:s + 1] * dec, axis=-1, keepdims=True)
                if r0 > lo:
                    col = jnp.concatenate([jnp.zeros((r0 - lo, 1), F32), col], axis=0)
                att_b = jnp.where(lane == s, col, att_b)
            blocks.append(att_b)
        att = blocks[0] if len(blocks) == 1 else jnp.concatenate(blocks, axis=0)

        s_prev = s_scr[...]
        qd = q * jnp.exp(a_cum)
        o = _dot(qd.astype(BF16), s_prev.astype(BF16)) + _dot(att.astype(BF16), v.astype(BF16))
        a_last = a_cum[c_len - 1:c_len]
        kd = k * jnp.exp(a_last - a_cum)
        dec_s = jnp.exp(_dot_tn(log_f, ones_cv, precision=HI))
        s_scr[...] = dec_s * s_prev + _dot_tn(kd.astype(BF16), v.astype(BF16))

        rr = lax.rsqrt(jnp.mean(o * o, axis=-1, keepdims=True) + EPS)
        o_ref[pl.ds(r, c_len), :] = ((o * rr) * nw_ref[...] * _silu(ga)).astype(o_ref.dtype)
        return carry

    _for_chunks(n_chunks, chunk_body, unroll=True)

    @pl.when(i == pl.num_programs(2) - 1)
    def _():
        sout_ref[...] = s_scr[...]


def _hgrn(z3, lb_logits, norm_w, s0, *, layer, chunk, l_valid):
    bs, lp, _ = z3.shape
    tb = _tile(lp, 4 * chunk)
    kern = functools.partial(_hgrn_kernel, layer=layer, chunk=chunk, n_chunks=tb // chunk,
                             l_valid=l_valid, l_padded=lp)

    def col(off):
        return pl.BlockSpec((None, tb, LANES), lambda b, h, i: (b, i, off // LANES + h))

    return pl.pallas_call(
        kern,
        out_shape=(jax.ShapeDtypeStruct((bs, lp, A_W), BF16),
                   jax.ShapeDtypeStruct((bs, HA_H, HA_DK, HA_DV), F32)),
        grid=(bs, HA_H, lp // tb),
        in_specs=[col(OFF_FA), col(OFF_QA), col(OFF_IA), col(OFF_GA),
                  pl.BlockSpec((DEPTH, LANES), lambda b, h, i: (0, h)),
                  pl.BlockSpec((1, HA_DV), lambda b, h, i: (0, 0)),
                  pl.BlockSpec((None, None, HA_DK, HA_DV), lambda b, h, i: (b, h, 0, 0))],
        out_specs=(pl.BlockSpec((None, tb, LANES), lambda b, h, i: (b, i, h)),
                   pl.BlockSpec((None, None, HA_DK, HA_DV), lambda b, h, i: (b, h, 0, 0))),
        scratch_shapes=[pltpu.VMEM((HA_DK, HA_DV), F32)],
        compiler_params=_params(("parallel", "parallel", "arbitrary")),
        name="hgrn2",
    )(z3, z3, z3, z3, lb_logits, norm_w.reshape(1, HA_DV), s0)


def _gdn_kernel(q_ref, k_ref, v_ref, zc_ref, ab_ref, cw_ref, alog_ref, dt_ref, nw_ref, conv0_ref, s0_ref,
                o_ref, sout_ref, s_scr, prev_scr, act_scr, gb_scr, x_scr,
                *, chunk, n_chunks, l_valid, l_padded):
    i = pl.program_id(1)
    c_len = chunk
    tb = n_chunks * c_len

    @pl.when(i == 0)
    def _():
        s_scr[...] = s0_ref[...]
        prev_scr[...] = conv0_ref[...]

    row8 = _iota((SUBLANES, C_K), 0)

    def conv(x, prev, w):
        y = x * w[CONV_W - 1:CONV_W]
        for j in range(1, CONV_W):
            xr = pltpu.roll(x, j, 0)
            head = jnp.where(row8 < j, pltpu.roll(prev, j, 0), xr[:SUBLANES])
            xs = head if tb == SUBLANES else jnp.concatenate([head, xr[SUBLANES:]], axis=0)
            y = y + xs * w[CONV_W - 1 - j:CONV_W - j]
        return _silu(y)

    for n, ref in enumerate((q_ref, k_ref, v_ref)):
        x = ref[...]
        lo, hi = n * C_K, (n + 1) * C_K
        y = conv(x, prev_scr[:, lo:hi], cw_ref[:, lo:hi])
        for h in range(HC_H):
            act_scr[n * HC_H + h] = y[:, h * HC_DK:(h + 1) * HC_DK]
        prev_scr[:, lo:hi] = x[tb - SUBLANES:]

    ab = ab_ref[...]
    g_all = -jnp.exp(alog_ref[...]) * _softplus(ab + dt_ref[...])
    b_all = _sigmoid(ab)
    if l_valid < l_padded:
        valid = (i * tb + _iota((tb, 1), 0)) < l_valid
        g_all = jnp.where(valid, g_all, 0.0)
        b_all = jnp.where(valid, b_all, 0.0)
    gb_scr[:, 0:LANES] = g_all
    gb_scr[:, LANES:2 * LANES] = b_all

    ii = _iota((c_len, c_len), 0)
    jj = _iota((c_len, c_len), 1)
    tri = (jj <= ii).astype(F32)
    tri_u = (ii <= jj).astype(F32)

    def chunk_body(c, carry):
        r = _chunk_start(c, c_len)
        heads = []
        for h in range(HC_H):
            lo, hi = h * HC_DK, (h + 1) * HC_DK
            qh = act_scr[h, pl.ds(r, c_len), :]
            kh = act_scr[HC_H + h, pl.ds(r, c_len), :]
            vh = act_scr[2 * HC_H + h, pl.ds(r, c_len), :]
            qh = qh * lax.rsqrt(jnp.sum(qh * qh, axis=-1, keepdims=True) + EPS) * (HC_DK ** -0.5)
            kh = kh * lax.rsqrt(jnp.sum(kh * kh, axis=-1, keepdims=True) + EPS)
            g_col = gb_scr[pl.ds(r, c_len), h:h + 1]
            b_col = gb_scr[pl.ds(r, c_len), LANES + HC_H + h:LANES + HC_H + h + 1]
            g_b = jnp.broadcast_to(g_col, (c_len, LANES))
            g_cum = _dot_hi(tri, g_b)
            g_row = _dot_tn(g_b, tri_u, precision=HI)[:c_len]
            rel = g_cum[:, :c_len] - g_row
            d_causal = jnp.exp(jnp.where(jj <= ii, rel, NEG_INF))
            d_strict_t = jnp.exp(jnp.where(jj > ii, -rel, NEG_INF))
            kb = (b_col * kh).astype(BF16)
            khb = kh.astype(BF16)
            l_t = _dot_nt(khb, kb) * d_strict_t
            qk = _dot_nt(qh.astype(BF16), khb) * d_causal
            e_g = jnp.exp(g_cum)
            x_scr[h, 0] = b_col * e_g * kh
            x_scr[h, 1] = b_col * vh
            heads.append((qh, kh, l_t, qk, e_g, g_cum))

        for t in range(1, c_len):
            r1 = ((t + SUBLANES - 1) // SUBLANES) * SUBLANES
            for h in range(HC_H):
                col = heads[h][2][0:r1, t:t + 1]
                for part in range(2):
                    contrib = jnp.sum(x_scr[h, part, 0:r1, :] * col, axis=0, keepdims=True)
                    x_scr[h, part, t:t + 1, :] = x_scr[h, part, t:t + 1, :] - contrib

        for h in range(HC_H):
            qh, kh, _, qk, e_g, g_cum = heads[h]
            lo, hi = h * HC_DV, (h + 1) * HC_DV
            s_prev = s_scr[h]
            s_b = s_prev.astype(BF16)
            u = x_scr[h, 1] - _dot(x_scr[h, 0].astype(BF16), s_b)
            u_b = u.astype(BF16)
            o = e_g * _dot(qh.astype(BF16), s_b) + _dot(qk.astype(BF16), u_b)
            g_last = g_cum[c_len - 1:c_len]
            kd = kh * jnp.exp(g_last - g_cum)
            s_scr[h] = jnp.exp(g_last) * s_prev + _dot_tn(kd.astype(BF16), u_b)
            rr = lax.rsqrt(jnp.mean(o * o, axis=-1, keepdims=True) + EPS)
            zc = zc_ref[pl.ds(r, c_len), lo:hi]
            o_ref[pl.ds(r, c_len), lo:hi] = ((o * rr) * nw_ref[...] * _silu(zc)).astype(o_ref.dtype)
        return carry

    _for_chunks(n_chunks, chunk_body)

    @pl.when(i == pl.num_programs(1) - 1)
    def _():
        sout_ref[...] = s_scr[...]


def _gdn(z3, conv_w, a_log, dt_bias, norm_w, conv0, s0, *, chunk, l_valid):
    bs, lp, _ = z3.shape
    tb = _tile(lp, 4 * chunk)
    kern = functools.partial(_gdn_kernel, chunk=chunk, n_chunks=tb // chunk, l_valid=l_valid, l_padded=lp)
    pad = jnp.zeros((LANES - HC_H,), F32)
    alog_row = jnp.concatenate([a_log, pad]).reshape(1, LANES)
    dt_row = jnp.concatenate([dt_bias, pad]).reshape(1, LANES)

    def wide(off):
        return pl.BlockSpec((None, tb, C_K), lambda b, i: (b, i, off // C_K))

    return pl.pallas_call(
        kern,
        out_shape=(jax.ShapeDtypeStruct((bs, lp, C_K), BF16),
                   jax.ShapeDtypeStruct((bs, HC_H, HC_DK, HC_DV), F32)),
        grid=(bs, lp // tb),
        in_specs=[wide(OFF_QC), wide(OFF_KC), wide(OFF_VC), wide(OFF_ZC),
                  pl.BlockSpec((None, tb, LANES), lambda b, i: (b, i, OFF_AB // LANES)),
                  pl.BlockSpec((CONV_W, CONV_CH), lambda b, i: (0, 0)),
                  pl.BlockSpec((1, LANES), lambda b, i: (0, 0)),
                  pl.BlockSpec((1, LANES), lambda b, i: (0, 0)),
                  pl.BlockSpec((1, HC_DV), lambda b, i: (0, 0)),
                  pl.BlockSpec((None, SUBLANES, CONV_CH), lambda b, i: (b, 0, 0)),
                  pl.BlockSpec((None, HC_H, HC_DK, HC_DV), lambda b, i: (b, 0, 0, 0))],
        out_specs=(pl.BlockSpec((None, tb, C_K), lambda b, i: (b, i, 0)),
                   pl.BlockSpec((None, HC_H, HC_DK, HC_DV), lambda b, i: (b, 0, 0, 0))),
        scratch_shapes=[pltpu.VMEM((HC_H, HC_DK, HC_DV), F32),
                        pltpu.VMEM((SUBLANES, CONV_CH), F32),
                        pltpu.VMEM((3 * HC_H, tb, HC_DK), F32),
                        pltpu.VMEM((tb, 2 * LANES), F32),
                        pltpu.VMEM((HC_H, 2, chunk, HC_DK), F32)],
        compiler_params=_params(("parallel", "arbitrary")),
        name="gdn",
    )(z3, z3, z3, z3, z3, conv_w, alog_row, dt_row, norm_w.reshape(1, HC_DV), conv0, s0)


def _foxprep_kernel(q_ref, k_ref, v_ref, fb_ref, gq_ref, gk_ref, bf_ref, bd_ref,
                    qo_ref, ko_ref, kbo_ref, vbo_ref, lfo_ref, *rest, cumsum):
    bd = bd_ref[...]

    def head_rms(x, g):
        x2 = x * x
        hi = x2.astype(BF16)
        lo = (x2 - hi.astype(F32)).astype(BF16)
        ss = _dot(hi, bd) + _dot(lo, bd)
        return x * lax.rsqrt(ss * (1.0 / HB_D) + EPS) * g

    qn = head_rms(q_ref[...], gq_ref[...])
    kn = head_rms(k_ref[...], gk_ref[...])
    qo_ref[...] = (qn * (HB_D ** -0.5)).astype(BF16)
    ko_ref[...] = kn
    kbo_ref[...] = kn.astype(BF16)
    vbo_ref[...] = v_ref[...].astype(BF16)
    lf = _log_sigmoid(fb_ref[...] + bf_ref[...])
    lf = jnp.where(_iota(lf.shape, 1) < HB_H, lf, 0.0)
    lfo_ref[...] = lf
    if cumsum:
        ft_ref, carry = rest
        tm = lf.shape[0]

        @pl.when(pl.program_id(1) == 0)
        def _():
            carry[...] = jnp.zeros_like(carry)

        tri = (_iota((tm, tm), 1) <= _iota((tm, tm), 0)).astype(F32)
        f_cum = _dot_hi(tri, lf) + carry[...]
        carry[...] = f_cum[tm - 1:tm]
        ft_ref[...] = f_cum.T[:HB_H]


def _fox_prep(z3, gq, gk, b_f, *, cumsum):
    bg, lg, _ = z3.shape
    tm = _tile(lg, 256)
    gq_row = jnp.tile(gq, HB_H).reshape(1, B_W)
    gk_row = jnp.tile(gk, HB_H).reshape(1, B_W)
    bf_row = jnp.concatenate([b_f, jnp.zeros((LANES - HB_H,), F32)]).reshape(1, LANES)
    seg = jnp.arange(B_W) // HB_D
    bd = (seg[:, None] == seg[None, :]).astype(BF16)

    def wide(off):
        return pl.BlockSpec((None, tm, B_W), lambda b, i: (b, i, off // B_W))

    tok = pl.BlockSpec((None, tm, B_W), lambda b, i: (b, i, 0))
    out_shape = [jax.ShapeDtypeStruct((bg, lg, B_W), BF16), jax.ShapeDtypeStruct((bg, lg, B_W), F32),
                 jax.ShapeDtypeStruct((bg, lg, B_W), BF16), jax.ShapeDtypeStruct((bg, lg, B_W), BF16),
                 jax.ShapeDtypeStruct((bg, lg, LANES), F32)]
    out_specs = [tok, tok, tok, tok, pl.BlockSpec((None, tm, LANES), lambda b, i: (b, i, 0))]
    scratch = []
    if cumsum:
        out_shape.append(jax.ShapeDtypeStruct((bg, HB_H, lg), F32))
        out_specs.append(pl.BlockSpec((None, HB_H, tm), lambda b, i: (b, 0, i)))
        scratch.append(pltpu.VMEM((1, LANES), F32))
    return pl.pallas_call(
        functools.partial(_foxprep_kernel, cumsum=cumsum),
        out_shape=tuple(out_shape),
        grid=(bg, lg // tm),
        in_specs=[wide(OFF_QB), wide(OFF_KB), wide(OFF_VB),
                  pl.BlockSpec((None, tm, LANES), lambda b, i: (b, i, OFF_FB // LANES)),
                  pl.BlockSpec((1, B_W), lambda b, i: (0, 0)),
                  pl.BlockSpec((1, B_W), lambda b, i: (0, 0)),
                  pl.BlockSpec((1, LANES), lambda b, i: (0, 0)),
                  pl.BlockSpec((B_W, B_W), lambda b, i: (0, 0))],
        out_specs=tuple(out_specs),
        scratch_shapes=scratch,
        compiler_params=_params(("parallel", "arbitrary")),
        name="fox_prep",
    )(z3, z3, z3, z3, gq_row, gk_row, bf_row, bd)


def _foxattn_kernel(q_ref, k_ref, v_ref, f_ref, o_ref, *, tq):
    qi = pl.program_id(2)
    q = q_ref[...]
    lane_q = _iota(q.shape, 1)
    q_heads = (jnp.where(lane_q < HB_D, q, jnp.zeros_like(q)), jnp.where(lane_q >= HB_D, q, jnp.zeros_like(q)))
    row = _iota((tq, tq), 0)
    colm = _iota((tq, tq), 1)

    def step(j, carry, masked):
        r = pl.multiple_of(j * tq, tq)
        kj = k_ref[pl.ds(r, tq), :]
        vj = v_ref[pl.ds(r, tq), :]
        fj = f_ref[j]
        out = []
        for hh in range(2):
            m, l, acc = carry[hh]
            s = _dot_nt(q_heads[hh], kj) - fj[hh:hh + 1, :]
            if masked:
                s = jnp.where(colm <= row, s, NEG_INF)
            m_new = jnp.maximum(m, jnp.max(s, axis=-1, keepdims=True))
            alpha = jnp.exp(m - m_new)
            p = jnp.exp(s - m_new)
            l = alpha * l + jnp.sum(p, axis=-1, keepdims=True)
            acc = alpha * acc + _dot(p.astype(BF16), vj)
            out.append((m_new, l, acc))
        return tuple(out)

    init = tuple((jnp.full((tq, 1), -1e30, F32), jnp.zeros((tq, 1), F32), jnp.zeros((tq, LANES), F32))
                 for _ in range(2))
    carry = lax.fori_loop(0, qi, lambda j, c: step(j, c, False), init)
    (_, l0, a0), (_, l1, a1) = step(qi, carry, True)
    o = jnp.where(_iota((tq, LANES), 1) < HB_D, a0 / l0, a1 / l1)
    o_ref[...] = o.astype(o_ref.dtype)


def _fox_attn_prompt(qb, kb, vb, ft):
    bg, s_len, _ = qb.shape
    tq = _tile(s_len, 512)
    nk = s_len // tq
    pairs = HB_H // 2
    f5 = ft.reshape(bg, pairs, 2, nk, tq).transpose(0, 1, 3, 2, 4)
    return pl.pallas_call(
        functools.partial(_foxattn_kernel, tq=tq),
        out_shape=jax.ShapeDtypeStruct((bg, s_len, B_W), BF16),
        grid=(bg, pairs, nk),
        in_specs=[pl.BlockSpec((None, tq, LANES), lambda b, p, i: (b, i, p)),
                  pl.BlockSpec((None, s_len, LANES), lambda b, p, i: (b, 0, p)),
                  pl.BlockSpec((None, s_len, LANES), lambda b, p, i: (b, 0, p)),
                  pl.BlockSpec((None, None, nk, 2, tq), lambda b, p, i: (b, p, 0, 0, 0))],
        out_specs=pl.BlockSpec((None, tq, LANES), lambda b, p, i: (b, i, p)),
        compiler_params=_params(("parallel", "parallel", "arbitrary")),
        name="fox_attn_prompt",
    )(qb, kb, vb, f5)


def _foxsample_kernel(pt_ref, q_ref, kn_ref, vn_ref, lfn_ref, *rest, n_new):
    ps = PAGES_PER_STEP
    k_pages = rest[0:ps]
    v_pages = rest[ps:2 * ps]
    f_pages = rest[2 * ps:3 * ps]
    o_ref, q_scr, m_scr, l_scr, acc_scr, fc_scr = rest[3 * ps:]
    g = pl.program_id(1)
    rows = n_new * HB_H
    hmask = (_iota((HB_H, B_W), 1) // HB_D) == _iota((HB_H, B_W), 0)

    @pl.when(g == 0)
    def _():
        for t in range(n_new):
            qt = jnp.broadcast_to(q_ref[t:t + 1, :], (HB_H, B_W))
            q_scr[t * HB_H:(t + 1) * HB_H, :] = jnp.where(hmask, qt, 0.0)
        m_scr[...] = jnp.full_like(m_scr, -1e30)
        l_scr[...] = jnp.zeros_like(l_scr)
        acc_scr[...] = jnp.zeros_like(acc_scr)
        fc_scr[...] = jnp.zeros_like(fc_scr)

    tri_u = (_iota((PAGE, PAGE), 0) <= _iota((PAGE, PAGE), 1)).astype(F32)

    def update(s, v_b):
        m = m_scr[...]
        m_new = jnp.maximum(m, jnp.max(s, axis=-1, keepdims=True))
        alpha = jnp.exp(m - m_new)
        p = jnp.exp(s - m_new)
        l_scr[...] = alpha * l_scr[...] + jnp.sum(p, axis=-1, keepdims=True)
        acc_scr[...] = alpha * acc_scr[...] + _dot_nt(p.astype(BF16), v_b)
        m_scr[...] = m_new

    def cum_forget(lf_parts):
        f_loc = [_dot_hi(lf_t, tri_u) for lf_t in lf_parts]
        off = fc_scr[...]
        out = []
        for fl in f_loc:
            out.append(fl + off)
            off = off + fl[:, PAGE - 1:PAGE]
        fc_scr[...] = off
        return out

    f_cat = jnp.concatenate(cum_forget([f_pages[r][...] for r in range(ps)]), axis=1)
    bias = jnp.concatenate([f_cat] * n_new, axis=0)
    k_cat = jnp.concatenate([k_pages[r][...].astype(BF16) for r in range(ps)], axis=1)
    v_cat = jnp.concatenate([v_pages[r][...].astype(BF16) for r in range(ps)], axis=1)
    q_rows = q_scr[...].astype(BF16)
    update(_dot(q_rows, k_cat) - bias, v_cat)

    @pl.when(g == pl.num_programs(1) - 1)
    def _():
        f_new = cum_forget([lfn_ref[...]])[0]
        s = _dot(q_rows, kn_ref[...].astype(BF16)) - jnp.concatenate([f_new] * n_new, axis=0)
        visible = _iota((rows, PAGE), 1) <= (_iota((rows, PAGE), 0) // HB_H)
        update(jnp.where(visible, s, NEG_INF), vn_ref[...].astype(BF16))
        o = acc_scr[...] / l_scr[...]
        out = jnp.zeros((SUBLANES, B_W), F32)
        out_row = _iota((SUBLANES, B_W), 0)
        for t in range(n_new):
            ot = jnp.where(hmask, o[t * HB_H:(t + 1) * HB_H, :], 0.0)
            out = jnp.where(out_row == t, jnp.sum(ot, axis=0, keepdims=True), out)
        o_ref[...] = out.astype(o_ref.dtype)


def _fox_attn_sample(layer, q_new, k_new, v_new, lf_new, cache_kt, cache_vt, cache_lf_t, page_table):
    db, n_new, _ = q_new.shape
    n_pages = page_table.shape[1]
    n_pool = cache_kt.shape[0] // DEPTH
    ps = PAGES_PER_STEP
    assert n_pages % ps == 0 and n_new <= SUBLANES
    qp = jnp.pad(q_new, ((0, 0), (0, SUBLANES - n_new), (0, 0)))
    pad_pos = ((0, 0), (0, 0), (0, PAGE - n_new))
    knp = jnp.pad(k_new.transpose(0, 2, 1), pad_pos)
    vnp = jnp.pad(v_new.transpose(0, 2, 1), pad_pos)
    lfp = jnp.pad(lf_new.transpose(0, 2, 1), pad_pos)
    base = layer * n_pool

    def page_spec(r, rows, cols):
        return pl.BlockSpec((None, rows, cols), lambda b, g, pt: (base + pt[b, g * ps + r], 0, 0))

    in_specs = [pl.BlockSpec((None, SUBLANES, B_W), lambda b, g, pt: (b, 0, 0)),
                pl.BlockSpec((None, B_W, PAGE), lambda b, g, pt: (b, 0, 0)),
                pl.BlockSpec((None, B_W, PAGE), lambda b, g, pt: (b, 0, 0)),
                pl.BlockSpec((None, HB_H, PAGE), lambda b, g, pt: (b, 0, 0))]
    in_specs += [page_spec(r, B_W, PAGE) for r in range(ps)]
    in_specs += [page_spec(r, B_W, PAGE) for r in range(ps)]
    in_specs += [page_spec(r, HB_H, PAGE) for r in range(ps)]
    rows = n_new * HB_H
    return pl.pallas_call(
        functools.partial(_foxsample_kernel, n_new=n_new),
        out_shape=jax.ShapeDtypeStruct((db, SUBLANES, B_W), BF16),
        grid_spec=pltpu.PrefetchScalarGridSpec(
            num_scalar_prefetch=1,
            grid=(db, n_pages // ps),
            in_specs=in_specs,
            out_specs=pl.BlockSpec((None, SUBLANES, B_W), lambda b, g, pt: (b, 0, 0)),
            scratch_shapes=[pltpu.VMEM((rows, B_W), F32), pltpu.VMEM((rows, 1), F32),
                            pltpu.VMEM((rows, 1), F32), pltpu.VMEM((rows, B_W), F32),
                            pltpu.VMEM((HB_H, 1), F32)]),
        compiler_params=_params(("parallel", "arbitrary")),
        name="fox_attn_sample",
    )(page_table, qp, knp, vnp, lfp, *([cache_kt] * ps), *([cache_vt] * ps), *([cache_lf_t] * ps))


def _merge_kernel(oa_ref, ob_ref, oc_ref, ga_ref, gb_ref, gc_ref, x_ref, gt1_ref, sh2_ref, sc2_ref,
                  wa_ref, wb_ref, wc_ref, wo_ref, nf_ref, wq_ref, keys_ref,
                  x1_ref, h2t_ref, st_ref):
    merged = (_sigmoid(ga_ref[...]) * _dot(oa_ref[...], wa_ref[...])
              + _sigmoid(gb_ref[...]) * _dot(ob_ref[...], wb_ref[...])
              + _sigmoid(gc_ref[...]) * _dot(oc_ref[...], wc_ref[...]))
    x1 = x_ref[...] + gt1_ref[...] * _dot(merged.astype(BF16), wo_ref[...])
    x1_ref[...] = x1
    r = lax.rsqrt(jnp.mean(x1 * x1, axis=-1, keepdims=True) + EPS)
    h2 = (x1 * r) * nf_ref[...] * (1.0 + sc2_ref[...]) + sh2_ref[...]
    h2t_ref[...] = h2.T.astype(BF16)
    qb = _dot(h2.astype(BF16), wq_ref[...]).astype(BF16)
    half = P_QD // 2
    for h in range(P_H):
        for p in range(2):
            lo = (h * 2 + p) * half
            st_ref[lo:lo + half, :] = _dot_nt(keys_ref[p], qb[:, lo:lo + half])


def _merge(oa, ob, oc, z3, x, mod, w_a, w_b, w_c, w_o, norm_ffn, w_q, keys):
    bg, lg, _ = x.shape
    tm = _tile(lg, 256)
    nl = lg // tm
    t_all = bg * lg

    def tok(width):
        return pl.BlockSpec((None, tm, width), lambda b, i: (b, i, 0))

    def gate(k):
        return pl.BlockSpec((None, tm, D), lambda b, i: (b, i, OFF_GATE // D + k))

    def full(shape):
        return pl.BlockSpec(shape, lambda b, i: (0,) * len(shape))

    return pl.pallas_call(
        _merge_kernel,
        out_shape=(jax.ShapeDtypeStruct((bg, lg, D), F32),
                   jax.ShapeDtypeStruct((D, t_all), BF16),
                   jax.ShapeDtypeStruct((P_H * P_QD, t_all), F32)),
        grid=(bg, nl),
        in_specs=[tok(A_W), tok(B_W), tok(C_K), gate(0), gate(1), gate(2), tok(D),
                  _mod_spec(mod, tm, 2), _mod_spec(mod, tm, 3), _mod_spec(mod, tm, 4),
                  full((A_W, D)), full((B_W, D)), full((C_K, D)), full((D, D)), full((1, D)),
                  full((D, P_H * P_QD)), full((2, P_NK, P_QD // 2))],
        out_specs=(tok(D),
                   pl.BlockSpec((D, tm), lambda b, i: (0, b * nl + i)),
                   pl.BlockSpec((P_H * P_QD, tm), lambda b, i: (0, b * nl + i))),
        compiler_params=_params(("parallel", "parallel")),
        name="merge",
    )(oa, ob, oc, z3, z3, z3, x, mod, mod, mod, w_a, w_b, w_c, w_o, norm_ffn.reshape(1, D), w_q, keys)


def _topk_kernel(st_ref, o_ref):
    tt = st_ref.shape[1]

    def top_rows(s):
        rows = []
        cur = s
        for it in range(P_TOPK + 1):
            m = jnp.max(cur, axis=0, keepdims=True)
            rows.append(m)
            if it < P_TOPK:
                cur = jnp.where(cur == m, NEG_INF, cur)
        return rows

    rank = _iota((P_TOPK, tt), 0)
    rank8 = _iota((SUBLANES, tt), 0)
    thr_rows, nrm_rows = [], []
    for h in range(P_H):
        lo = h * P_QD
        v1 = top_rows(st_ref[lo:lo + P_NK, :])
        v2 = top_rows(st_ref[lo + P_NK:lo + 2 * P_NK, :])
        v1s = jnp.concatenate(v1[:P_TOPK], axis=0)
        v2s = jnp.concatenate(v2[:P_TOPK], axis=0)
        groups = [jnp.where(rank8 == 0, v1[P_TOPK] + v2[0], jnp.where(rank8 == 1, v1[0] + v2[P_TOPK], NEG_INF))]
        for b in range(3):
            groups.append(jnp.where(rank < P_TOPK // (b + 1), v1s + v2[b], NEG_INF))
        for a in range(4):
            nb = P_TOPK // (a + 1)
            n_rows = P_TOPK if nb > SUBLANES else SUBLANES
            rk = _iota((n_rows, tt), 0)
            ok = jnp.where(rk >= 3, rk, nb) < nb
            groups.append(jnp.where(ok, v2s[:n_rows] + v1[a], NEG_INF))
        m_top = None
        z = None
        for it in range(P_TOPK):
            m = functools.reduce(jnp.maximum, [jnp.max(gp, axis=0, keepdims=True) for gp in groups])
            if it == 0:
                m_top = m
                z = jnp.ones_like(m)
            else:
                z = z + jnp.exp(m - m_top)
            groups = [jnp.where(gp == m, NEG_INF, gp) for gp in groups]
        m_next = functools.reduce(jnp.maximum, [jnp.max(gp, axis=0, keepdims=True) for gp in groups])
        thr_rows.append(0.5 * m + 0.5 * m_next)
        nrm_rows.append(-(m_top + jnp.log(z)))
    o_ref[...] = jnp.concatenate(thr_rows + nrm_rows, axis=0)


def _topk(st):
    t_all = st.shape[1]
    tt = _tile(t_all, 256)
    return pl.pallas_call(
        _topk_kernel,
        out_shape=jax.ShapeDtypeStruct((2 * P_H, t_all), F32),
        grid=(t_all // tt,),
        in_specs=[pl.BlockSpec((P_H * P_QD, tt), lambda t: (0, t))],
        out_specs=pl.BlockSpec((2 * P_H, tt), lambda t: (0, t)),
        compiler_params=_params(("parallel",)),
        name="peer_topk",
    )(st)


def _peer_kernel(st_ref, stat_ref, h2t_ref, u_ref, vt_ref, x1_ref, gt2_ref, o_ref,
                 acc_scr, e1_scr, tau_scr, e2_scr, s2_scr, ht0_scr, ht1_scr, wa0_scr, wa1_scr, *, ti, n_tiles):
    s = pl.program_id(2)
    tt = st_ref.shape[1]

    @pl.when(s == 0)
    def _():
        acc_scr[...] = jnp.zeros_like(acc_scr)
        ht0_scr[...] = jnp.zeros_like(ht0_scr)
        ht1_scr[...] = jnp.zeros_like(ht1_scr)
        wa0_scr[...] = jnp.zeros_like(wa0_scr)
        wa1_scr[...] = jnp.zeros_like(wa1_scr)
        for h in range(P_H):
            lo = h * P_QD
            hr = slice(h * P_NK, (h + 1) * P_NK)
            for lt in range(tt // LANES):
                ls = slice(lt * LANES, (lt + 1) * LANES)
                s1 = st_ref[lo:lo + P_NK, ls]
                s2 = st_ref[lo + P_NK:lo + 2 * P_NK, ls]
                mx2 = jnp.max(s2, axis=0, keepdims=True)
                s2_scr[lt, hr, :] = s2
                e2_scr[lt, hr, :] = jnp.exp(s2 - mx2)
                e1_scr[hr, ls] = 0.5 * jnp.exp(s1 + (stat_ref[P_H + h:P_H + h + 1, ls] + mx2))
                tau_scr[hr, ls] = stat_ref[h:h + 1, ls] - s1

    c0 = math.sqrt(2.0 / math.pi)
    c1 = c0 * 0.044715
    tile_b = jnp.clip(s - 1, 0, n_tiles - 1)

    def stages(ht_w, ht_r, wa_w, wa_r):
        te = ti * P_NK
        n_lt = tt // LANES
        per_group = 2 if n_lt % 2 == 0 else 1
        n_p = ti * (n_lt // per_group)
        ka, kc = n_p // 2, 2
        ma, mc = te // 2, D // (n_p // 2)
        wka, wkc = D // ka, te // kc

        def matmul_pieces(pi):
            mh, kq = pi // ka, pi % ka
            part = _dot(u_ref[mh * ma:(mh + 1) * ma, kq * wka:(kq + 1) * wka], h2t_ref[kq * wka:(kq + 1) * wka, :])
            for lt in range(n_lt):
                piece = part[:, lt * LANES:(lt + 1) * LANES]
                if kq == 0:
                    ht_w[lt, mh * ma:(mh + 1) * ma, :] = piece
                else:
                    ht_w[lt, mh * ma:(mh + 1) * ma, :] += piece
            mq, kh = pi // kc, pi % kc
            acc_scr[mq * mc:(mq + 1) * mc, :] += _dot(vt_ref[mq * mc:(mq + 1) * mc, kh * wkc:(kh + 1) * wkc],
                                                     wa_r[kh * wkc:(kh + 1) * wkc, :])

        for ii in range(ti):
            i_row = tile_b * ti + ii
            tau_rows = [tau_scr[pl.ds(h * P_NK + i_row, 1), :] for h in range(P_H)]
            e1_rows = [e1_scr[pl.ds(h * P_NK + i_row, 1), :] for h in range(P_H)]
            for lt in range(n_lt):
                if lt % per_group == 0:
                    matmul_pieces(ii * (n_lt // per_group) + lt // per_group)
                ls = slice(lt * LANES, (lt + 1) * LANES)
                tau_b = [jnp.broadcast_to(tau_rows[h][:, ls], (PEER_JB, LANES)) for h in range(P_H)]
                e1_b = [jnp.broadcast_to(e1_rows[h][:, ls], (PEER_JB, LANES)) for h in range(P_H)]
                for jb in range(P_NK // PEER_JB):
                    j0 = jb * PEER_JB
                    hs = ht_r[lt, ii * P_NK + j0:ii * P_NK + j0 + PEER_JB, :]
                    act = hs * (1.0 + jnp.tanh(hs * (c0 + c1 * (hs * hs))))
                    w = None
                    for h in range(P_H):
                        jr = slice(h * P_NK + j0, h * P_NK + j0 + PEER_JB)
                        wh = jnp.where(s2_scr[lt, jr, :] >= tau_b[h], e2_scr[lt, jr, :] * e1_b[h], 0.0)
                        w = wh if w is None else w + wh
                    wa_w[ii * P_NK + j0:ii * P_NK + j0 + PEER_JB, ls] = (w * act).astype(BF16)

    @pl.when(s % 2 == 0)
    def _():
        stages(ht0_scr, ht1_scr, wa1_scr, wa0_scr)

    @pl.when(s % 2 == 1)
    def _():
        stages(ht1_scr, ht0_scr, wa0_scr, wa1_scr)

    @pl.when(s == pl.num_programs(2) - 1)
    def _():
        o_ref[...] = x1_ref[...] + gt2_ref[...] * acc_scr[...].T


def _peer(st, stats, h2t, u_b, vt_b, x1, mod):
    bg, lg, _ = x1.shape
    tt = _tile(lg, 512)
    nl = lg // tt
    ti = PEER_TI
    te = ti * P_NK
    assert vt_b.shape == (N_EXP // te, D, te)
    n_tiles = N_EXP // te
    return pl.pallas_call(
        functools.partial(_peer_kernel, ti=ti, n_tiles=n_tiles),
        out_shape=jax.ShapeDtypeStruct((bg, lg, D), F32),
        grid=(bg, nl, n_tiles + 2),
        in_specs=[pl.BlockSpec((P_H * P_QD, tt), lambda b, i, s: (0, b * nl + i)),
                  pl.BlockSpec((2 * P_H, tt), lambda b, i, s: (0, b * nl + i)),
                  pl.BlockSpec((D, tt), lambda b, i, s: (0, b * nl + i)),
                  pl.BlockSpec((te, D), lambda b, i, s: (jnp.minimum(s, n_tiles - 1), 0)),
                  pl.BlockSpec((None, D, te), lambda b, i, s: (jnp.clip(s - 2, 0, n_tiles - 1), 0, 0)),
                  pl.BlockSpec((None, tt, D), lambda b, i, s: (b, i, 0)),
                  _mod_spec(mod, tt, 5)],
        out_specs=pl.BlockSpec((None, tt, D), lambda b, i, s: (b, i, 0)),
        scratch_shapes=[pltpu.VMEM((D, tt), F32),
                        pltpu.VMEM((P_H * P_NK, tt), F32), pltpu.VMEM((P_H * P_NK, tt), F32),
                        pltpu.VMEM((tt // LANES, P_H * P_NK, LANES), F32),
                        pltpu.VMEM((tt // LANES, P_H * P_NK, LANES), F32),
                        pltpu.VMEM((tt // LANES, te, LANES), F32), pltpu.VMEM((tt // LANES, te, LANES), F32),
                        pltpu.VMEM((te, tt), BF16), pltpu.VMEM((te, tt), BF16)],
        compiler_params=_params(("parallel", "parallel", "arbitrary")),
        name="peer_experts",
    )(st, stats, h2t, u_b, vt_b, x1, mod)


def _pack_w_in(w):
    pad = jnp.zeros((D, LANES - 8), w.dtype)
    o_fb = 2 * A_W + 2 * A_W + 3 * B_W
    o_c = o_fb + HB_H
    o_ab = o_c + 3 * C_K
    o_zc = o_ab + 2 * HC_H
    o_gate = o_zc + C_K
    packed = jnp.concatenate([
        w[:, 0:4 * A_W], w[:, o_gate:o_gate + 3 * D], w[:, 4 * A_W:o_fb], w[:, o_c:o_ab],
        w[:, o_zc:o_gate], w[:, o_fb:o_c], pad, w[:, o_ab:o_zc], pad], axis=1)
    assert packed.shape[1] == NP
    return packed.astype(BF16)


def _layer(l, x, mod, w, sample):
    bg, lg, _ = x.shape
    z = _in_proj(x, mod, w["norm_mix"][l], w["w_in"][l])
    if sample is None:
        z3 = z
        bs, l_seq = bg, lg
        chunk = LIN_C
        hgrn0 = jnp.zeros((bs, HA_H, HA_DK, HA_DV), F32)
        gdn0 = jnp.zeros((bs, HC_H, HC_DK, HC_DV), F32)
        conv0 = jnp.zeros((bs, SUBLANES, CONV_CH), F32)
    else:
        bs, l_seq = sample["db"], sample["t"]
        chunk = SUBLANES
        z3 = jnp.pad(z.reshape(bs, l_seq, NP), ((0, 0), (0, SUBLANES - l_seq), (0, 0)))
        hgrn0 = sample["state_hgrn"][l]
        gdn0 = sample["state_gdn"][l]
        conv0 = jnp.pad(sample["state_conv"][l], ((0, 0), (SUBLANES - (CONV_W - 1), 0), (0, 0)))

    o_a, s_hgrn = _hgrn(z3, w["hgrn_lb_logits"], w["hgrn_norm"][l], hgrn0, layer=l, chunk=chunk, l_valid=l_seq)
    o_c, s_gdn = _gdn(z3, w["gdn_conv_w"][l], w["gdn_a_log"][l], w["gdn_dt_bias"][l], w["gdn_norm"][l],
                      conv0, gdn0, chunk=chunk, l_valid=l_seq)
    zseq = z.reshape(bs, l_seq, NP)
    conv_new = zseq[:, l_seq - (CONV_W - 1):, OFF_QC:OFF_QC + CONV_CH]

    prep = _fox_prep(z, w["fox_q_norm"][l], w["fox_k_norm"][l], w["fox_b_f"][l], cumsum=sample is None)
    qn, kn, knb, vnb, lf = prep[:5]
    v_b = z[:, :, OFF_VB:OFF_VB + B_W]
    if sample is None:
        o_b = _fox_attn_prompt(qn, knb, vnb, prep[5])
    else:
        o_b = _fox_attn_sample(
            l, qn.astype(F32).reshape(bs, l_seq, B_W), kn.reshape(bs, l_seq, B_W), v_b.reshape(bs, l_seq, B_W),
            lf[0, :, :HB_H].reshape(bs, l_seq, HB_H), sample["cache_kt"], sample["cache_vt"], sample["cache_lf_t"],
            sample["page_table"])
        o_b = o_b[:, :l_seq].reshape(bg, lg, B_W)
        o_a = o_a[:, :l_seq].reshape(bg, lg, A_W)
        o_c = o_c[:, :l_seq].reshape(bg, lg, C_K)

    x1, h2t, st = _merge(o_a, o_b, o_c, z, x, mod, w["w_br_a"][l], w["w_br_b"][l], w["w_br_c"][l],
                         w["w_out"][l], w["norm_ffn"][l], w["peer_w_q"][l], w["peer_keys"][l])
    stats = _topk(st)
    x2 = _peer(st, stats, h2t, w["peer_u"][l], w["peer_vt"][l], x1, mod)

    k_leaf = kn.reshape(bs, l_seq, HB_H, HB_D)
    v_leaf = v_b.reshape(bs, l_seq, HB_H, HB_D)
    lf_leaf = lf[:, :, :HB_H].reshape(bs, l_seq, HB_H)
    return x2, (k_leaf, v_leaf, lf_leaf, s_hgrn, s_gdn, conv_new)


def _trunk(x, mods, w, sample):
    leaves = [[] for _ in range(6)]
    for l in range(DEPTH):
        x, st = _layer(l, x, mods[l], w, sample)
        for lst, s in zip(leaves, st):
            lst.append(s)
    return x, [jnp.stack(v) for v in leaves]


def kernel(x_prompt, x_sample, c_prompt, c_sample, cache_fox_k, cache_fox_v, cache_fox_logf, page_table,
           state_hgrn, state_gdn, state_gdn_conv, w_ada, b_ada, norm_mix, norm_ffn, w_in, hgrn_lb_logits,
           hgrn_norm, fox_b_f, fox_q_norm, fox_k_norm, gdn_conv_w, gdn_a_log, gdn_dt_bias, gdn_norm,
           w_br_a, w_br_b, w_br_c, w_out, peer_w_q, peer_keys, peer_u, peer_v):
    bp = x_prompt.shape[0]
    db, t_new, _ = x_sample.shape
    n_pool = cache_fox_k.shape[1]

    w = {
        "norm_mix": norm_mix, "norm_ffn": norm_ffn, "hgrn_lb_logits": hgrn_lb_logits, "hgrn_norm": hgrn_norm,
        "fox_b_f": fox_b_f, "fox_q_norm": fox_q_norm, "fox_k_norm": fox_k_norm, "gdn_conv_w": gdn_conv_w,
        "gdn_a_log": gdn_a_log, "gdn_dt_bias": gdn_dt_bias, "gdn_norm": gdn_norm,
        "w_in": [_pack_w_in(w_in[l]) for l in range(DEPTH)],
        "w_br_a": w_br_a.astype(BF16), "w_br_b": w_br_b.astype(BF16), "w_br_c": w_br_c.astype(BF16),
        "w_out": w_out.astype(BF16), "peer_w_q": peer_w_q.astype(BF16), "peer_keys": peer_keys.astype(BF16),
        "peer_u": peer_u.astype(BF16),
        "peer_vt": peer_v.astype(BF16).reshape(DEPTH, N_EXP // (PEER_TI * P_NK), PEER_TI * P_NK, D).transpose(0, 1, 3, 2),
    }

    n_c = bp + db
    c_all = jnp.pad(jnp.concatenate([c_prompt, c_sample], axis=0), ((0, (-n_c) % SUBLANES), (0, 0)))
    mod = _ada(c_all, w_ada, b_ada)
    mods_p = [mod[l, :bp].reshape(bp, 1, 6 * D) for l in range(DEPTH)]
    mods_s = [jnp.repeat(mod[l, bp:n_c], t_new, axis=0).reshape(1, db * t_new, 6 * D) for l in range(DEPTH)]

    sample = {
        "db": db, "t": t_new, "page_table": page_table,
        "state_hgrn": state_hgrn, "state_gdn": state_gdn, "state_conv": state_gdn_conv,
        "cache_kt": cache_fox_k.transpose(0, 1, 3, 4, 2).reshape(DEPTH * n_pool, B_W, PAGE),
        "cache_vt": cache_fox_v.transpose(0, 1, 3, 4, 2).reshape(DEPTH * n_pool, B_W, PAGE),
        "cache_lf_t": cache_fox_logf.transpose(0, 1, 3, 2).reshape(DEPTH * n_pool, HB_H, PAGE),
    }

    y_p, leaves_p = _trunk(x_prompt, mods_p, w, None)
    y_s, leaves_s = _trunk(x_sample.reshape(1, db * t_new, D), mods_s, w, sample)
    return (y_p, y_s.reshape(db, t_new, D), *leaves_p, *leaves_s)
```

```python
import functools
import math

import jax
import jax.numpy as jnp
from jax import lax
from jax.experimental import pallas as pl
from jax.experimental.pallas import tpu as pltpu

F32 = jnp.float32
BF16 = jnp.bfloat16
HI = lax.Precision.HIGHEST
NEG_INF = float("-inf")

D = 1024
DEPTH = 2
HA_H, HA_DK, HA_DV = 4, 128, 128
HB_H, HB_D = 8, 64
HC_H, HC_DK, HC_DV = 4, 128, 128
CONV_W = 4
P_H, P_TOPK, P_NK, P_QD = 8, 16, 128, 256
N_EXP = P_NK * P_NK
EPS = 1e-6
A_W = HA_H * HA_DK
B_W = HB_H * HB_D
C_K = HC_H * HC_DK
CONV_CH = 3 * C_K
PAGE = 128

LANES = 128
SUBLANES = 8
VMEM_LIMIT = 52 * 1024 * 1024

OFF_FA, OFF_QA, OFF_IA, OFF_GA = 0, 512, 1024, 1536
OFF_GATE = 2048
OFF_QB, OFF_KB, OFF_VB = 5120, 5632, 6144
OFF_QC, OFF_KC, OFF_VC = 6656, 7168, 7680
OFF_ZC = 8192
OFF_FB = 8704
OFF_AB = 8832
NP = 8960

LIN_C = 64
HGRN_SUB = 16
PEER_JB = 32
PEER_TI = 4
PAGES_PER_STEP = 16


def _dot(a, b):
    return jnp.dot(a, b, preferred_element_type=F32)


def _dot_hi(a, b):
    return jnp.dot(a, b, preferred_element_type=F32, precision=HI)


def _dot_nt(a, b):
    return lax.dot_general(a, b, (((1,), (1,)), ((), ())), preferred_element_type=F32)


def _dot_tn(a, b, precision=None):
    return lax.dot_general(a, b, (((0,), (0,)), ((), ())), preferred_element_type=F32, precision=precision)


def _iota(shape, dim):
    return lax.broadcasted_iota(jnp.int32, shape, dim)


def _sigmoid(x):
    return jax.nn.sigmoid(x)


def _silu(x):
    return x * jax.nn.sigmoid(x)


def _softplus(x):
    return jnp.maximum(x, 0.0) + jnp.log1p(jnp.exp(-jnp.abs(x)))


def _log_sigmoid(x):
    return jnp.minimum(x, 0.0) - jnp.log1p(jnp.exp(-jnp.abs(x)))


def _params(sem):
    return pltpu.CompilerParams(dimension_semantics=sem, vmem_limit_bytes=VMEM_LIMIT)


def _chunk_start(c, c_len):
    return c * c_len if isinstance(c, int) else pl.multiple_of(c * c_len, c_len)


def _for_chunks(n_chunks, body, unroll=False):
    if n_chunks == 1 or unroll:
        for c in range(n_chunks):
            body(c, 0)
    else:
        lax.fori_loop(0, n_chunks, body, 0)


def _tile(n, pref):
    t = min(n, pref)
    assert n % t == 0, (n, pref)
    return t


def _ada_kernel(c_ref, w_ref, b_ref, o_ref):
    sc = _silu(c_ref[...]).astype(BF16)
    o_ref[...] = _dot(sc, w_ref[...].astype(BF16)) + b_ref[...]


def _ada(c_all, w_ada, b_ada):
    rows = c_all.shape[0]
    tn = 768
    return pl.pallas_call(
        _ada_kernel,
        out_shape=jax.ShapeDtypeStruct((DEPTH, rows, 6 * D), F32),
        grid=(DEPTH, 6 * D // tn),
        in_specs=[pl.BlockSpec((rows, D), lambda l, n: (0, 0)),
                  pl.BlockSpec((None, D, tn), lambda l, n: (l, 0, n)),
                  pl.BlockSpec((None, 1, tn), lambda l, n: (l, 0, n))],
        out_specs=pl.BlockSpec((None, rows, tn), lambda l, n: (l, 0, n)),
        compiler_params=_params(("parallel", "parallel")),
        name="ada",
    )(c_all, w_ada, b_ada.reshape(DEPTH, 1, 6 * D))


def _mod_spec(mod, tm, k):
    if mod.shape[1] == 1:
        return pl.BlockSpec((None, 1, D), lambda b, i, *_: (b, 0, k))
    return pl.BlockSpec((None, tm, D), lambda b, i, *_: (b, i, k))


def _in_kernel(x_ref, sh_ref, sc_ref, g_ref, w_ref, z_ref, h_scr):
    @pl.when(pl.program_id(2) == 0)
    def _():
        x = x_ref[...]
        r = lax.rsqrt(jnp.mean(x * x, axis=-1, keepdims=True) + EPS)
        h = (x * r) * g_ref[...] * (1.0 + sc_ref[...]) + sh_ref[...]
        h_scr[...] = h.astype(BF16)

    z_ref[...] = _dot(h_scr[...], w_ref[...])


def _in_proj(x, mod, gain, w_packed):
    bg, lg, _ = x.shape
    tm = _tile(lg, 512)
    tn = 1280
    return pl.pallas_call(
        _in_kernel,
        out_shape=jax.ShapeDtypeStruct((bg, lg, NP), F32),
        grid=(bg, lg // tm, NP // tn),
        in_specs=[pl.BlockSpec((None, tm, D), lambda b, i, n: (b, i, 0)),
                  _mod_spec(mod, tm, 0), _mod_spec(mod, tm, 1),
                  pl.BlockSpec((1, D), lambda b, i, n: (0, 0)),
                  pl.BlockSpec((D, tn), lambda b, i, n: (0, n))],
        out_specs=pl.BlockSpec((None, tm, tn), lambda b, i, n: (b, i, n)),
        scratch_shapes=[pltpu.VMEM((tm, D), BF16)],
        compiler_params=_params(("parallel", "parallel", "arbitrary")),
        name="in_proj",
    )(x, mod, mod, gain.reshape(1, D), w_packed)


def _hgrn_kernel(fa_ref, qa_ref, ia_ref, ga_ref, lbl_ref, nw_ref, s0_ref, o_ref, sout_ref, s_scr,
                 *, layer, chunk, n_chunks, l_valid, l_padded):
    i = pl.program_id(2)
    c_len = chunk

    @pl.when(i == 0)
    def _():
        s_scr[...] = s0_ref[...]

    lg = lbl_ref[...]
    e = jnp.exp(lg - jnp.max(lg, axis=0, keepdims=True))
    p = e / jnp.sum(e, axis=0, keepdims=True)
    cs = p[0:1]
    for j in range(1, layer + 1):
        cs = cs + p[j:j + 1]
    lb = jnp.maximum(cs - p[0:1], 0.0)
    log_lb = jnp.log(lb)
    log1m_lb = jnp.log1p(-lb)

    sub = min(HGRN_SUB, c_len)
    tri = (_iota((c_len, c_len), 1) <= _iota((c_len, c_len), 0)).astype(F32)
    lane = _iota((sub, c_len), 1)
    row1 = _iota((c_len, 1), 0)
    ones_cv = jnp.ones((c_len, HA_DV), F32)

    def chunk_body(c, carry):
        r = _chunk_start(c, c_len)
        fa = fa_ref[pl.ds(r, c_len), :]
        qa = qa_ref[pl.ds(r, c_len), :]
        v = ia_ref[pl.ds(r, c_len), :]
        ga = ga_ref[pl.ds(r, c_len), :]

        b_ = log1m_lb + _log_sigmoid(fa)
        log_f = jnp.maximum(log_lb, b_) + jnp.log1p(jnp.exp(-jnp.abs(log_lb - b_)))
        k = (1.0 - lb) * _sigmoid(-fa)
        if l_valid < l_padded:
            valid = (i * (n_chunks * c_len) + r + row1) < l_valid
            log_f = jnp.where(valid, log_f, 0.0)
            k = jnp.where(valid, k, 0.0)
        q = _silu(qa)
        a_cum = _dot_hi(tri, log_f)

        blocks = []
        for bi in range(c_len // sub):
            lo, hi = bi * sub, (bi + 1) * sub
            if bi == 0:
                att_b = jnp.zeros((sub, c_len), F32)
            else:
                a_ref = a_cum[lo - 1:lo]
                qs = q[lo:hi] * jnp.exp(a_cum[lo:hi] - a_ref)
                ks = jnp.where(row1 < lo, k * jnp.exp(jnp.minimum(a_ref - a_cum, 0.0)), 0.0)
                att_b = _dot_nt(qs.astype(BF16), ks.astype(BF16))
            for s in range(lo, hi):
                r0 = (s // SUBLANES) * SUBLANES
                rel = a_cum[r0:hi] - a_cum[s:s + 1]
                dec = jnp.exp(jnp.where(row1[r0:hi] >= s, rel, NEG_INF))
                col = jnp.sum(q[r0:hi] * k[s:s + 1] * dec, axis=-1, keepdims=True)
                if r0 > lo:
                    col = jnp.concatenate([jnp.zeros((r0 - lo, 1), F32), col], axis=0)
                att_b = jnp.where(lane == s, col, att_b)
            blocks.append(att_b)
        att = blocks[0] if len(blocks) == 1 else jnp.concatenate(blocks, axis=0)

        s_prev = s_scr[...]
        qd = q * jnp.exp(a_cum)
        o = _dot(qd.astype(BF16), s_prev.astype(BF16)) + _dot(att.astype(BF16), v.astype(BF16))
        a_last = a_cum[c_len - 1:c_len]
        kd = k * jnp.exp(a_last - a_cum)
        dec_s = jnp.exp(_dot_tn(log_f, ones_cv, precision=HI))
        s_scr[...] = dec_s * s_prev + _dot_tn(kd.astype(BF16), v.astype(BF16))

        rr = lax.rsqrt(jnp.mean(o * o, axis=-1, keepdims=True) + EPS)
        o_ref[pl.ds(r, c_len), :] = ((o * rr) * nw_ref[...] * _silu(ga)).astype(o_ref.dtype)
        return carry

    _for_chunks(n_chunks, chunk_body, unroll=True)

    @pl.when(i == pl.num_programs(2) - 1)
    def _():
        sout_ref[...] = s_scr[...]


def _hgrn(z3, lb_logits, norm_w, s0, *, layer, chunk, l_valid):
    bs, lp, _ = z3.shape
    tb = _tile(lp, 4 * chunk)
    kern = functools.partial(_hgrn_kernel, layer=layer, chunk=chunk, n_chunks=tb // chunk,
                             l_valid=l_valid, l_padded=lp)

    def col(off):
        return pl.BlockSpec((None, tb, LANES), lambda b, h, i: (b, i, off // LANES + h))

    return pl.pallas_call(
        kern,
        out_shape=(jax.ShapeDtypeStruct((bs, lp, A_W), BF16),
                   jax.ShapeDtypeStruct((bs, HA_H, HA_DK, HA_DV), F32)),
        grid=(bs, HA_H, lp // tb),
        in_specs=[col(OFF_FA), col(OFF_QA), col(OFF_IA), col(OFF_GA),
                  pl.BlockSpec((DEPTH, LANES), lambda b, h, i: (0, h)),
                  pl.BlockSpec((1, HA_DV), lambda b, h, i: (0, 0)),
                  pl.BlockSpec((None, None, HA_DK, HA_DV), lambda b, h, i: (b, h, 0, 0))],
        out_specs=(pl.BlockSpec((None, tb, LANES), lambda b, h, i: (b, i, h)),
                   pl.BlockSpec((None, None, HA_DK, HA_DV), lambda b, h, i: (b, h, 0, 0))),
        scratch_shapes=[pltpu.VMEM((HA_DK, HA_DV), F32)],
        compiler_params=_params(("parallel", "parallel", "arbitrary")),
        name="hgrn2",
    )(z3, z3, z3, z3, lb_logits, norm_w.reshape(1, HA_DV), s0)


def _gdn_kernel(q_ref, k_ref, v_ref, zc_ref, ab_ref, cw_ref, alog_ref, dt_ref, nw_ref, conv0_ref, s0_ref,
                o_ref, sout_ref, s_scr, prev_scr, act_scr, gb_scr, x_scr,
                *, chunk, n_chunks, l_valid, l_padded):
    i = pl.program_id(1)
    c_len = chunk
    tb = n_chunks * c_len

    @pl.when(i == 0)
    def _():
        s_scr[...] = s0_ref[...]
        prev_scr[...] = conv0_ref[...]

    row8 = _iota((SUBLANES, C_K), 0)

    def conv(x, prev, w):
        y = x * w[CONV_W - 1:CONV_W]
        for j in range(1, CONV_W):
            xr = pltpu.roll(x, j, 0)
            head = jnp.where(row8 < j, pltpu.roll(prev, j, 0), xr[:SUBLANES])
            xs = head if tb == SUBLANES else jnp.concatenate([head, xr[SUBLANES:]], axis=0)
            y = y + xs * w[CONV_W - 1 - j:CONV_W - j]
        return _silu(y)

    for n, ref in enumerate((q_ref, k_ref, v_ref)):
        x = ref[...]
        lo, hi = n * C_K, (n + 1) * C_K
        y = conv(x, prev_scr[:, lo:hi], cw_ref[:, lo:hi])
        for h in range(HC_H):
            act_scr[n * HC_H + h] = y[:, h * HC_DK:(h + 1) * HC_DK]
        prev_scr[:, lo:hi] = x[tb - SUBLANES:]

    ab = ab_ref[...]
    g_all = -jnp.exp(alog_ref[...]) * _softplus(ab + dt_ref[...])
    b_all = _sigmoid(ab)
    if l_valid < l_padded:
        valid = (i * tb + _iota((tb, 1), 0)) < l_valid
        g_all = jnp.where(valid, g_all, 0.0)
        b_all = jnp.where(valid, b_all, 0.0)
    gb_scr[:, 0:LANES] = g_all
    gb_scr[:, LANES:2 * LANES] = b_all

    ii = _iota((c_len, c_len), 0)
    jj = _iota((c_len, c_len), 1)
    tri = (jj <= ii).astype(F32)
    tri_u = (ii <= jj).astype(F32)

    def chunk_body(c, carry):
        r = _chunk_start(c, c_len)
        heads = []
        for h in range(HC_H):
            lo, hi = h * HC_DK, (h + 1) * HC_DK
            qh = act_scr[h, pl.ds(r, c_len), :]
            kh = act_scr[HC_H + h, pl.ds(r, c_len), :]
            vh = act_scr[2 * HC_H + h, pl.ds(r, c_len), :]
            qh = qh * lax.rsqrt(jnp.sum(qh * qh, axis=-1, keepdims=True) + EPS) * (HC_DK ** -0.5)
            kh = kh * lax.rsqrt(jnp.sum(kh * kh, axis=-1, keepdims=True) + EPS)
            g_col = gb_scr[pl.ds(r, c_len), h:h + 1]
            b_col = gb_scr[pl.ds(r, c_len), LANES + HC_H + h:LANES + HC_H + h + 1]
            g_b = jnp.broadcast_to(g_col, (c_len, LANES))
            g_cum = _dot_hi(tri, g_b)
            g_row = _dot_tn(g_b, tri_u, precision=HI)[:c_len]
            rel = g_cum[:, :c_len] - g_row
            d_causal = jnp.exp(jnp.where(jj <= ii, rel, NEG_INF))
            d_strict_t = jnp.exp(jnp.where(jj > ii, -rel, NEG_INF))
            kb = (b_col * kh).astype(BF16)
            khb = kh.astype(BF16)
            l_t = _dot_nt(khb, kb) * d_strict_t
            qk = _dot_nt(qh.astype(BF16), khb) * d_causal
            e_g = jnp.exp(g_cum)
            x_scr[h, 0] = b_col * e_g * kh
            x_scr[h, 1] = b_col * vh
            heads.append((qh, kh, l_t, qk, e_g, g_cum))

        for t in range(1, c_len):
            r1 = ((t + SUBLANES - 1) // SUBLANES) * SUBLANES
            for h in range(HC_H):
                col = heads[h][2][0:r1, t:t + 1]
                for part in range(2):
                    contrib = jnp.sum(x_scr[h, part, 0:r1, :] * col, axis=0, keepdims=True)
                    x_scr[h, part, t:t + 1, :] = x_scr[h, part, t:t + 1, :] - contrib

        for h in range(HC_H):
            qh, kh, _, qk, e_g, g_cum = heads[h]
            lo, hi = h * HC_DV, (h + 1) * HC_DV
            s_prev = s_scr[h]
            s_b = s_prev.astype(BF16)
            u = x_scr[h, 1] - _dot(x_scr[h, 0].astype(BF16), s_b)
            u_b = u.astype(BF16)
            o = e_g * _dot(qh.astype(BF16), s_b) + _dot(qk.astype(BF16), u_b)
            g_last = g_cum[c_len - 1:c_len]
            kd = kh * jnp.exp(g_last - g_cum)
            s_scr[h] = jnp.exp(g_last) * s_prev + _dot_tn(kd.astype(BF16), u_b)
            rr = lax.rsqrt(jnp.mean(o * o, axis=-1, keepdims=True) + EPS)
            zc = zc_ref[pl.ds(r, c_len), lo:hi]
            o_ref[pl.ds(r, c_len), lo:hi] = ((o * rr) * nw_ref[...] * _silu(zc)).astype(o_ref.dtype)
        return carry

    _for_chunks(n_chunks, chunk_body)

    @pl.when(i == pl.num_programs(1) - 1)
    def _():
        sout_ref[...] = s_scr[...]


def _gdn(z3, conv_w, a_log, dt_bias, norm_w, conv0, s0, *, chunk, l_valid):
    bs, lp, _ = z3.shape
    tb = _tile(lp, 4 * chunk)
    kern = functools.partial(_gdn_kernel, chunk=chunk, n_chunks=tb // chunk, l_valid=l_valid, l_padded=lp)
    pad = jnp.zeros((LANES - HC_H,), F32)
    alog_row = jnp.concatenate([a_log, pad]).reshape(1, LANES)
    dt_row = jnp.concatenate([dt_bias, pad]).reshape(1, LANES)

    def wide(off):
        return pl.BlockSpec((None, tb, C_K), lambda b, i: (b, i, off // C_K))

    return pl.pallas_call(
        kern,
        out_shape=(jax.ShapeDtypeStruct((bs, lp, C_K), BF16),
                   jax.ShapeDtypeStruct((bs, HC_H, HC_DK, HC_DV), F32)),
        grid=(bs, lp // tb),
        in_specs=[wide(OFF_QC), wide(OFF_KC), wide(OFF_VC), wide(OFF_ZC),
                  pl.BlockSpec((None, tb, LANES), lambda b, i: (b, i, OFF_AB // LANES)),
                  pl.BlockSpec((CONV_W, CONV_CH), lambda b, i: (0, 0)),
                  pl.BlockSpec((1, LANES), lambda b, i: (0, 0)),
                  pl.BlockSpec((1, LANES), lambda b, i: (0, 0)),
                  pl.BlockSpec((1, HC_DV), lambda b, i: (0, 0)),
                  pl.BlockSpec((None, SUBLANES, CONV_CH), lambda b, i: (b, 0, 0)),
                  pl.BlockSpec((None, HC_H, HC_DK, HC_DV), lambda b, i: (b, 0, 0, 0))],
        out_specs=(pl.BlockSpec((None, tb, C_K), lambda b, i: (b, i, 0)),
                   pl.BlockSpec((None, HC_H, HC_DK, HC_DV), lambda b, i: (b, 0, 0, 0))),
        scratch_shapes=[pltpu.VMEM((HC_H, HC_DK, HC_DV), F32),
                        pltpu.VMEM((SUBLANES, CONV_CH), F32),
                        pltpu.VMEM((3 * HC_H, tb, HC_DK), F32),
                        pltpu.VMEM((tb, 2 * LANES), F32),
                        pltpu.VMEM((HC_H, 2, chunk, HC_DK), F32)],
        compiler_params=_params(("parallel", "arbitrary")),
        name="gdn",
    )(z3, z3, z3, z3, z3, conv_w, alog_row, dt_row, norm_w.reshape(1, HC_DV), conv0, s0)


def _foxprep_kernel(q_ref, k_ref, v_ref, fb_ref, gq_ref, gk_ref, bf_ref, bd_ref,
                    qo_ref, ko_ref, kbo_ref, vbo_ref, lfo_ref, *rest, cumsum):
    bd = bd_ref[...]

    def head_rms(x, g):
        x2 = x * x
        hi = x2.astype(BF16)
        lo = (x2 - hi.astype(F32)).astype(BF16)
        ss = _dot(hi, bd) + _dot(lo, bd)
        return x * lax.rsqrt(ss * (1.0 / HB_D) + EPS) * g

    qn = head_rms(q_ref[...], gq_ref[...])
    kn = head_rms(k_ref[...], gk_ref[...])
    qo_ref[...] = (qn * (HB_D ** -0.5)).astype(BF16)
    ko_ref[...] = kn
    kbo_ref[...] = kn.astype(BF16)
    vbo_ref[...] = v_ref[...].astype(BF16)
    lf = _log_sigmoid(fb_ref[...] + bf_ref[...])
    lf = jnp.where(_iota(lf.shape, 1) < HB_H, lf, 0.0)
    lfo_ref[...] = lf
    if cumsum:
        ft_ref, carry = rest
        tm = lf.shape[0]

        @pl.when(pl.program_id(1) == 0)
        def _():
            carry[...] = jnp.zeros_like(carry)

        tri = (_iota((tm, tm), 1) <= _iota((tm, tm), 0)).astype(F32)
        f_cum = _dot_hi(tri, lf) + carry[...]
        carry[...] = f_cum[tm - 1:tm]
        ft_ref[...] = f_cum.T[:HB_H]


def _fox_prep(z3, gq, gk, b_f, *, cumsum):
    bg, lg, _ = z3.shape
    tm = _tile(lg, 256)
    gq_row = jnp.tile(gq, HB_H).reshape(1, B_W)
    gk_row = jnp.tile(gk, HB_H).reshape(1, B_W)
    bf_row = jnp.concatenate([b_f, jnp.zeros((LANES - HB_H,), F32)]).reshape(1, LANES)
    seg = jnp.arange(B_W) // HB_D
    bd = (seg[:, None] == seg[None, :]).astype(BF16)

    def wide(off):
        return pl.BlockSpec((None, tm, B_W), lambda b, i: (b, i, off // B_W))

    tok = pl.BlockSpec((None, tm, B_W), lambda b, i: (b, i, 0))
    out_shape = [jax.ShapeDtypeStruct((bg, lg, B_W), BF16), jax.ShapeDtypeStruct((bg, lg, B_W), F32),
                 jax.ShapeDtypeStruct((bg, lg, B_W), BF16), jax.ShapeDtypeStruct((bg, lg, B_W), BF16),
                 jax.ShapeDtypeStruct((bg, lg, LANES), F32)]
    out_specs = [tok, tok, tok, tok, pl.BlockSpec((None, tm, LANES), lambda b, i: (b, i, 0))]
    scratch = []
    if cumsum:
        out_shape.append(jax.ShapeDtypeStruct((bg, HB_H, lg), F32))
        out_specs.append(pl.BlockSpec((None, HB_H, tm), lambda b, i: (b, 0, i)))
        scratch.append(pltpu.VMEM((1, LANES), F32))
    return pl.pallas_call(
        functools.partial(_foxprep_kernel, cumsum=cumsum),
        out_shape=tuple(out_shape),
        grid=(bg, lg // tm),
        in_specs=[wide(OFF_QB), wide(OFF_KB), wide(OFF_VB),
                  pl.BlockSpec((None, tm, LANES), lambda b, i: (b, i, OFF_FB // LANES)),
                  pl.BlockSpec((1, B_W), lambda b, i: (0, 0)),
                  pl.BlockSpec((1, B_W), lambda b, i: (0, 0)),
                  pl.BlockSpec((1, LANES), lambda b, i: (0, 0)),
                  pl.BlockSpec((B_W, B_W), lambda b, i: (0, 0))],
        out_specs=tuple(out_specs),
        scratch_shapes=scratch,
        compiler_params=_params(("parallel", "arbitrary")),
        name="fox_prep",
    )(z3, z3, z3, z3, gq_row, gk_row, bf_row, bd)


def _foxattn_kernel(q_ref, k_ref, v_ref, f_ref, o_ref, *, tq):
    qi = pl.program_id(2)
    q = q_ref[...]
    lane_q = _iota(q.shape, 1)
    q_heads = (jnp.where(lane_q < HB_D, q, jnp.zeros_like(q)), jnp.where(lane_q >= HB_D, q, jnp.zeros_like(q)))
    row = _iota((tq, tq), 0)
    colm = _iota((tq, tq), 1)

    def step(j, carry, masked):
        r = pl.multiple_of(j * tq, tq)
        kj = k_ref[pl.ds(r, tq), :]
        vj = v_ref[pl.ds(r, tq), :]
        fj = f_ref[j]
        out = []
        for hh in range(2):
            m, l, acc = carry[hh]
            s = _dot_nt(q_heads[hh], kj) - fj[hh:hh + 1, :]
            if masked:
                s = jnp.where(colm <= row, s, NEG_INF)
            m_new = jnp.maximum(m, jnp.max(s, axis=-1, keepdims=True))
            alpha = jnp.exp(m - m_new)
            p = jnp.exp(s - m_new)
            l = alpha * l + jnp.sum(p, axis=-1, keepdims=True)
            acc = alpha * acc + _dot(p.astype(BF16), vj)
            out.append((m_new, l, acc))
        return tuple(out)

    init = tuple((jnp.full((tq, 1), -1e30, F32), jnp.zeros((tq, 1), F32), jnp.zeros((tq, LANES), F32))
                 for _ in range(2))
    carry = lax.fori_loop(0, qi, lambda j, c: step(j, c, False), init)
    (_, l0, a0), (_, l1, a1) = step(qi, carry, True)
    o = jnp.where(_iota((tq, LANES), 1) < HB_D, a0 / l0, a1 / l1)
    o_ref[...] = o.astype(o_ref.dtype)


def _fox_attn_prompt(qb, kb, vb, ft):
    bg, s_len, _ = qb.shape
    tq = _tile(s_len, 512)
    nk = s_len // tq
    pairs = HB_H // 2
    f5 = ft.reshape(bg, pairs, 2, nk, tq).transpose(0, 1, 3, 2, 4)
    return pl.pallas_call(
        functools.partial(_foxattn_kernel, tq=tq),
        out_shape=jax.ShapeDtypeStruct((bg, s_len, B_W), BF16),
        grid=(bg, pairs, nk),
        in_specs=[pl.BlockSpec((None, tq, LANES), lambda b, p, i: (b, i, p)),
                  pl.BlockSpec((None, s_len, LANES), lambda b, p, i: (b, 0, p)),
                  pl.BlockSpec((None, s_len, LANES), lambda b, p, i: (b, 0, p)),
                  pl.BlockSpec((None, None, nk, 2, tq), lambda b, p, i: (b, p, 0, 0, 0))],
        out_specs=pl.BlockSpec((None, tq, LANES), lambda b, p, i: (b, i, p)),
        compiler_params=_params(("parallel", "parallel", "arbitrary")),
        name="fox_attn_prompt",
    )(qb, kb, vb, f5)


def _foxsample_kernel(pt_ref, q_ref, kn_ref, vn_ref, lfn_ref, *rest, n_new, ps):
    k_pages = rest[0:ps]
    v_pages = rest[ps:2 * ps]
    f_pages = rest[2 * ps:3 * ps]
    o_ref, q_scr, m_scr, l_scr, acc_scr, fc_scr = rest[3 * ps:]
    g = pl.program_id(1)
    rows = n_new * HB_H
    hmask = (_iota((HB_H, B_W), 1) // HB_D) == _iota((HB_H, B_W), 0)

    @pl.when(g == 0)
    def _():
        for t in range(n_new):
            qt = jnp.broadcast_to(q_ref[t:t + 1, :], (HB_H, B_W))
            q_scr[t * HB_H:(t + 1) * HB_H, :] = jnp.where(hmask, qt, 0.0)
        m_scr[...] = jnp.full_like(m_scr, -1e30)
        l_scr[...] = jnp.zeros_like(l_scr)
        acc_scr[...] = jnp.zeros_like(acc_scr)
        fc_scr[...] = jnp.zeros_like(fc_scr)

    tri_u = (_iota((PAGE, PAGE), 0) <= _iota((PAGE, PAGE), 1)).astype(F32)

    def update(s, v_b):
        m = m_scr[...]
        m_new = jnp.maximum(m, jnp.max(s, axis=-1, keepdims=True))
        alpha = jnp.exp(m - m_new)
        p = jnp.exp(s - m_new)
        l_scr[...] = alpha * l_scr[...] + jnp.sum(p, axis=-1, keepdims=True)
        acc_scr[...] = alpha * acc_scr[...] + _dot_nt(p.astype(BF16), v_b)
        m_scr[...] = m_new

    def cum_forget(lf_parts):
        f_loc = [_dot_hi(lf_t, tri_u) for lf_t in lf_parts]
        off = fc_scr[...]
        out = []
        for fl in f_loc:
            out.append(fl + off)
            off = off + fl[:, PAGE - 1:PAGE]
        fc_scr[...] = off
        return out

    f_cat = jnp.concatenate(cum_forget([f_pages[r][...] for r in range(ps)]), axis=1)
    bias = jnp.concatenate([f_cat] * n_new, axis=0)
    k_cat = jnp.concatenate([k_pages[r][...].astype(BF16) for r in range(ps)], axis=1)
    v_cat = jnp.concatenate([v_pages[r][...].astype(BF16) for r in range(ps)], axis=1)
    q_rows = q_scr[...].astype(BF16)
    update(_dot(q_rows, k_cat) - bias, v_cat)

    @pl.when(g == pl.num_programs(1) - 1)
    def _():
        f_new = cum_forget([lfn_ref[...]])[0]
        s = _dot(q_rows, kn_ref[...].astype(BF16)) - jnp.concatenate([f_new] * n_new, axis=0)
        visible = _iota((rows, PAGE), 1) <= (_iota((rows, PAGE), 0) // HB_H)
        update(jnp.where(visible, s, NEG_INF), vn_ref[...].astype(BF16))
        o = acc_scr[...] / l_scr[...]
        out = jnp.zeros((SUBLANES, B_W), F32)
        out_row = _iota((SUBLANES, B_W), 0)
        for t in range(n_new):
            ot = jnp.where(hmask, o[t * HB_H:(t + 1) * HB_H, :], 0.0)
            out = jnp.where(out_row == t, jnp.sum(ot, axis=0, keepdims=True), out)
        o_ref[...] = out.astype(o_ref.dtype)


def _fox_attn_sample(layer, q_new, k_new, v_new, lf_new, cache_kt, cache_vt, cache_lf_t, page_table):
    db, n_new, _ = q_new.shape
    n_pages = page_table.shape[1]
    n_pool = cache_kt.shape[0] // DEPTH
    ps = min(PAGES_PER_STEP, n_pages)
    assert n_pages % ps == 0 and n_new <= SUBLANES
    qp = jnp.pad(q_new, ((0, 0), (0, SUBLANES - n_new), (0, 0)))
    pad_pos = ((0, 0), (0, 0), (0, PAGE - n_new))
    knp = jnp.pad(k_new.transpose(0, 2, 1), pad_pos)
    vnp = jnp.pad(v_new.transpose(0, 2, 1), pad_pos)
    lfp = jnp.pad(lf_new.transpose(0, 2, 1), pad_pos)
    base = layer * n_pool

    def page_spec(r, rows, cols):
        return pl.BlockSpec((None, rows, cols), lambda b, g, pt: (base + pt[b, g * ps + r], 0, 0))

    in_specs = [pl.BlockSpec((None, SUBLANES, B_W), lambda b, g, pt: (b, 0, 0)),
                pl.BlockSpec((None, B_W, PAGE), lambda b, g, pt: (b, 0, 0)),
                pl.BlockSpec((None, B_W, PAGE), lambda b, g, pt: (b, 0, 0)),
                pl.BlockSpec((None, HB_H, PAGE), lambda b, g, pt: (b, 0, 0))]
    in_specs += [page_spec(r, B_W, PAGE) for r in range(ps)]
    in_specs += [page_spec(r, B_W, PAGE) for r in range(ps)]
    in_specs += [page_spec(r, HB_H, PAGE) for r in range(ps)]
    rows = n_new * HB_H
    return pl.pallas_call(
        functools.partial(_foxsample_kernel, n_new=n_new, ps=ps),
        out_shape=jax.ShapeDtypeStruct((db, SUBLANES, B_W), BF16),
        grid_spec=pltpu.PrefetchScalarGridSpec(
            num_scalar_prefetch=1,
            grid=(db, n_pages // ps),
            in_specs=in_specs,
            out_specs=pl.BlockSpec((None, SUBLANES, B_W), lambda b, g, pt: (b, 0, 0)),
            scratch_shapes=[pltpu.VMEM((rows, B_W), F32), pltpu.VMEM((rows, 1), F32),
                            pltpu.VMEM((rows, 1), F32), pltpu.VMEM((rows, B_W), F32),
                            pltpu.VMEM((HB_H, 1), F32)]),
        compiler_params=_params(("parallel", "arbitrary")),
        name="fox_attn_sample",
    )(page_table, qp, knp, vnp, lfp, *([cache_kt] * ps), *([cache_vt] * ps), *([cache_lf_t] * ps))


def _merge_kernel(oa_ref, ob_ref, oc_ref, ga_ref, gb_ref, gc_ref, x_ref, gt1_ref, sh2_ref, sc2_ref,
                  wa_ref, wb_ref, wc_ref, wo_ref, nf_ref, wq_ref, keys_ref,
                  x1_ref, h2t_ref, st_ref):
    merged = (_sigmoid(ga_ref[...]) * _dot(oa_ref[...], wa_ref[...])
              + _sigmoid(gb_ref[...]) * _dot(ob_ref[...], wb_ref[...])
              + _sigmoid(gc_ref[...]) * _dot(oc_ref[...], wc_ref[...]))
    x1 = x_ref[...] + gt1_ref[...] * _dot(merged.astype(BF16), wo_ref[...])
    x1_ref[...] = x1
    r = lax.rsqrt(jnp.mean(x1 * x1, axis=-1, keepdims=True) + EPS)
    h2 = (x1 * r) * nf_ref[...] * (1.0 + sc2_ref[...]) + sh2_ref[...]
    h2t_ref[...] = h2.T.astype(BF16)
    qb = _dot(h2.astype(BF16), wq_ref[...]).astype(BF16)
    half = P_QD // 2
    for h in range(P_H):
        for p in range(2):
            lo = (h * 2 + p) * half
            st_ref[lo:lo + half, :] = _dot_nt(keys_ref[p], qb[:, lo:lo + half])


def _merge(oa, ob, oc, z3, x, mod, w_a, w_b, w_c, w_o, norm_ffn, w_q, keys):
    bg, lg, _ = x.shape
    tm = _tile(lg, 256)
    nl = lg // tm
    t_all = bg * lg

    def tok(width):
        return pl.BlockSpec((None, tm, width), lambda b, i: (b, i, 0))

    def gate(k):
        return pl.BlockSpec((None, tm, D), lambda b, i: (b, i, OFF_GATE // D + k))

    def full(shape):
        return pl.BlockSpec(shape, lambda b, i: (0,) * len(shape))

    return pl.pallas_call(
        _merge_kernel,
        out_shape=(jax.ShapeDtypeStruct((bg, lg, D), F32),
                   jax.ShapeDtypeStruct((D, t_all), BF16),
                   jax.ShapeDtypeStruct((P_H * P_QD, t_all), F32)),
        grid=(bg, nl),
        in_specs=[tok(A_W), tok(B_W), tok(C_K), gate(0), gate(1), gate(2), tok(D),
                  _mod_spec(mod, tm, 2), _mod_spec(mod, tm, 3), _mod_spec(mod, tm, 4),
                  full((A_W, D)), full((B_W, D)), full((C_K, D)), full((D, D)), full((1, D)),
                  full((D, P_H * P_QD)), full((2, P_NK, P_QD // 2))],
        out_specs=(tok(D),
                   pl.BlockSpec((D, tm), lambda b, i: (0, b * nl + i)),
                   pl.BlockSpec((P_H * P_QD, tm), lambda b, i: (0, b * nl + i))),
        compiler_params=_params(("parallel", "parallel")),
        name="merge",
    )(oa, ob, oc, z3, z3, z3, x, mod, mod, mod, w_a, w_b, w_c, w_o, norm_ffn.reshape(1, D), w_q, keys)


def _topk_kernel(st_ref, o_ref):
    tt = st_ref.shape[1]

    def top_rows(s_ref, lo):
        n_v = P_NK // SUBLANES
        v = [s_ref[lo + k * SUBLANES:lo + (k + 1) * SUBLANES, :] for k in range(n_v)]
        k = 2
        while k <= n_v:
            j = k // 2
            while j >= 1:
                for a in range(n_v):
                    b = a ^ j
                    if b > a:
                        hi, lo_v = jnp.maximum(v[a], v[b]), jnp.minimum(v[a], v[b])
                        v[a], v[b] = (hi, lo_v) if (a & k) == 0 else (lo_v, hi)
                j //= 2
            k *= 2
        rows = []
        for it in range(P_TOPK + 1):
            m = jnp.max(v[0], axis=0, keepdims=True)
            rows.append(m)
            popped = v[0] == m
            for a in range(P_TOPK - it):
                v[a] = jnp.where(popped, v[a + 1] if a + 1 < n_v else NEG_INF, v[a])
        return rows

    rank = _iota((P_TOPK, tt), 0)
    rank8 = _iota((SUBLANES, tt), 0)
    thr_rows, nrm_rows = [], []
    for h in range(P_H):
        lo = h * P_QD
        v1 = top_rows(st_ref, lo)
        v2 = top_rows(st_ref, lo + P_NK)
        v1s = jnp.concatenate(v1[:P_TOPK], axis=0)
        v2s = jnp.concatenate(v2[:P_TOPK], axis=0)
        groups = [jnp.where(rank8 == 0, v1[P_TOPK] + v2[0], jnp.where(rank8 == 1, v1[0] + v2[P_TOPK], NEG_INF))]
        for b in range(3):
            groups.append(jnp.where(rank < P_TOPK // (b + 1), v1s + v2[b], NEG_INF))
        for a in range(4):
            nb = P_TOPK // (a + 1)
            n_rows = P_TOPK if nb > SUBLANES else SUBLANES
            rk = _iota((n_rows, tt), 0)
            ok = jnp.where(rk >= 3, rk, nb) < nb
            groups.append(jnp.where(ok, v2s[:n_rows] + v1[a], NEG_INF))
        m_top = None
        z = None
        for it in range(P_TOPK):
            m = functools.reduce(jnp.maximum, [jnp.max(gp, axis=0, keepdims=True) for gp in groups])
            if it == 0:
                m_top = m
                z = jnp.ones_like(m)
            else:
                z = z + jnp.exp(m - m_top)
            groups = [jnp.where(gp == m, NEG_INF, gp) for gp in groups]
        m_next = functools.reduce(jnp.maximum, [jnp.max(gp, axis=0, keepdims=True) for gp in groups])
        thr_rows.append(0.5 * m + 0.5 * m_next)
        nrm_rows.append(-(m_top + jnp.log(z)))
    o_ref[...] = jnp.concatenate(thr_rows + nrm_rows, axis=0)


def _topk(st):
    t_all = st.shape[1]
    tt = _tile(t_all, 256)
    return pl.pallas_call(
        _topk_kernel,
        out_shape=jax.ShapeDtypeStruct((2 * P_H, t_all), F32),
        grid=(t_all // tt,),
        in_specs=[pl.BlockSpec((P_H * P_QD, tt), lambda t: (0, t))],
        out_specs=pl.BlockSpec((2 * P_H, tt), lambda t: (0, t)),
        compiler_params=_params(("parallel",)),
        name="peer_topk",
    )(st)


def _peer_kernel(st_ref, stat_ref, h2t_ref, u0_ref, u1_ref, vt0_ref, vt1_ref, x1_ref, gt2_ref, o_ref,
                 acc_scr, e1_scr, tau_scr, e2_scr, s2_scr, ht0_scr, ht1_scr, wa0_scr, wa1_scr, *, ti, n_tiles):
    s = pl.program_id(2)
    tt = st_ref.shape[1]

    @pl.when(s == 0)
    def _():
        acc_scr[...] = jnp.zeros_like(acc_scr)
        ht0_scr[...] = jnp.zeros_like(ht0_scr)
        ht1_scr[...] = jnp.zeros_like(ht1_scr)
        wa0_scr[...] = jnp.zeros_like(wa0_scr)
        wa1_scr[...] = jnp.zeros_like(wa1_scr)
        for h in range(P_H):
            lo = h * P_QD
            hr = slice(h * P_NK, (h + 1) * P_NK)
            for lt in range(tt // LANES):
                ls = slice(lt * LANES, (lt + 1) * LANES)
                s1 = st_ref[lo:lo + P_NK, ls]
                s2 = st_ref[lo + P_NK:lo + 2 * P_NK, ls]
                mx2 = jnp.max(s2, axis=0, keepdims=True)
                s2_scr[lt, hr, :] = s2
                e2_scr[lt, hr, :] = jnp.exp(s2 - mx2)
                e1_scr[hr, ls] = 0.5 * jnp.exp(s1 + (stat_ref[P_H + h:P_H + h + 1, ls] + mx2))
                tau_scr[hr, ls] = stat_ref[h:h + 1, ls] - s1

    c0 = math.sqrt(2.0 / math.pi)
    c1 = c0 * 0.044715
    tile_b = jnp.clip(s - 1, 0, n_tiles - 1)

    def stages(ht_w, ht_r, wa_w, wa_r):
        te = ti * P_NK
        n_lt = tt // LANES
        per_group = 2 if n_lt % 2 == 0 else 1
        n_p = ti * (n_lt // per_group)
        ka, kc = n_p // 2, 2
        ma, mc = te // 2, D // (n_p // 2)
        wka, wkc = D // ka, te // kc

        def matmul_pieces(pi):
            mh, kq = pi // ka, pi % ka
            u_half = (u0_ref, u1_ref)[mh]
            part = _dot(u_half[:, kq * wka:(kq + 1) * wka], h2t_ref[kq * wka:(kq + 1) * wka, :])
            for lt in range(n_lt):
                piece = part[:, lt * LANES:(lt + 1) * LANES]
                if kq == 0:
                    ht_w[lt, mh * ma:(mh + 1) * ma, :] = piece
                else:
                    ht_w[lt, mh * ma:(mh + 1) * ma, :] += piece
            mq, kh = pi // kc, pi % kc
            vt_half = (vt0_ref, vt1_ref)[mq * mc // (D // 2)]
            v0 = (mq * mc) % (D // 2)
            acc_scr[mq * mc:(mq + 1) * mc, :] += _dot(vt_half[v0:v0 + mc, kh * wkc:(kh + 1) * wkc],
                                                     wa_r[kh * wkc:(kh + 1) * wkc, :])

        for ii in range(ti):
            i_row = tile_b * ti + ii
            tau_rows = [tau_scr[pl.ds(h * P_NK + i_row, 1), :] for h in range(P_H)]
            e1_rows = [e1_scr[pl.ds(h * P_NK + i_row, 1), :] for h in range(P_H)]
            for lt in range(n_lt):
                if lt % per_group == 0:
                    matmul_pieces(ii * (n_lt // per_group) + lt // per_group)
                ls = slice(lt * LANES, (lt + 1) * LANES)
                tau_b = [jnp.broadcast_to(tau_rows[h][:, ls], (PEER_JB, LANES)) for h in range(P_H)]
                e1_b = [jnp.broadcast_to(e1_rows[h][:, ls], (PEER_JB, LANES)) for h in range(P_H)]
                for jb in range(P_NK // PEER_JB):
                    j0 = jb * PEER_JB
                    hs = ht_r[lt, ii * P_NK + j0:ii * P_NK + j0 + PEER_JB, :]
                    act = hs * (1.0 + jnp.tanh(hs * (c0 + c1 * (hs * hs))))
                    w = None
                    for h in range(P_H):
                        jr = slice(h * P_NK + j0, h * P_NK + j0 + PEER_JB)
                        wh = jnp.where(s2_scr[lt, jr, :] >= tau_b[h], e2_scr[lt, jr, :] * e1_b[h], 0.0)
                        w = wh if w is None else w + wh
                    wa_w[ii * P_NK + j0:ii * P_NK + j0 + PEER_JB, ls] = (w * act).astype(BF16)

    @pl.when(s % 2 == 0)
    def _():
        stages(ht0_scr, ht1_scr, wa1_scr, wa0_scr)

    @pl.when(s % 2 == 1)
    def _():
        stages(ht1_scr, ht0_scr, wa0_scr, wa1_scr)

    @pl.when(s == pl.num_programs(2) - 1)
    def _():
        o_ref[...] = x1_ref[...] + gt2_ref[...] * acc_scr[...].T


def _peer(st, stats, h2t, u_b, vt_b, x1, mod):
    bg, lg, _ = x1.shape
    tt = _tile(lg, 512)
    nl = lg // tt
    ti = PEER_TI
    te = ti * P_NK
    assert vt_b.shape == (N_EXP // te, D, te)
    n_tiles = N_EXP // te
    return pl.pallas_call(
        functools.partial(_peer_kernel, ti=ti, n_tiles=n_tiles),
        out_shape=jax.ShapeDtypeStruct((bg, lg, D), F32),
        grid=(bg, nl, n_tiles + 2),
        in_specs=[pl.BlockSpec((P_H * P_QD, tt), lambda b, i, s: (0, b * nl + i)),
                  pl.BlockSpec((2 * P_H, tt), lambda b, i, s: (0, b * nl + i)),
                  pl.BlockSpec((D, tt), lambda b, i, s: (0, b * nl + i)),
                  pl.BlockSpec((te // 2, D), lambda b, i, s: (2 * jnp.minimum(s, n_tiles - 1), 0)),
                  pl.BlockSpec((te // 2, D), lambda b, i, s: (2 * jnp.minimum(s, n_tiles - 1) + 1, 0)),
                  pl.BlockSpec((None, D // 2, te), lambda b, i, s: (jnp.clip(s - 2, 0, n_tiles - 1), 0, 0)),
                  pl.BlockSpec((None, D // 2, te), lambda b, i, s: (jnp.clip(s - 2, 0, n_tiles - 1), 1, 0)),
                  pl.BlockSpec((None, tt, D), lambda b, i, s: (b, i, 0)),
                  _mod_spec(mod, tt, 5)],
        out_specs=pl.BlockSpec((None, tt, D), lambda b, i, s: (b, i, 0)),
        scratch_shapes=[pltpu.VMEM((D, tt), F32),
                        pltpu.VMEM((P_H * P_NK, tt), F32), pltpu.VMEM((P_H * P_NK, tt), F32),
                        pltpu.VMEM((tt // LANES, P_H * P_NK, LANES), F32),
                        pltpu.VMEM((tt // LANES, P_H * P_NK, LANES), F32),
                        pltpu.VMEM((tt // LANES, te, LANES), F32), pltpu.VMEM((tt // LANES, te, LANES), F32),
                        pltpu.VMEM((te, tt), BF16), pltpu.VMEM((te, tt), BF16)],
        compiler_params=_params(("parallel", "parallel", "arbitrary")),
        name="peer_experts",
    )(st, stats, h2t, u_b, u_b, vt_b, vt_b, x1, mod)


def _pack_w_in(w):
    pad = jnp.zeros((D, LANES - 8), w.dtype)
    o_fb = 2 * A_W + 2 * A_W + 3 * B_W
    o_c = o_fb + HB_H
    o_ab = o_c + 3 * C_K
    o_zc = o_ab + 2 * HC_H
    o_gate = o_zc + C_K
    packed = jnp.concatenate([
        w[:, 0:4 * A_W], w[:, o_gate:o_gate + 3 * D], w[:, 4 * A_W:o_fb], w[:, o_c:o_ab],
        w[:, o_zc:o_gate], w[:, o_fb:o_c], pad, w[:, o_ab:o_zc], pad], axis=1)
    assert packed.shape[1] == NP
    return packed.astype(BF16)


def _layer(l, x, mod, w, sample):
    bg, lg, _ = x.shape
    z = _in_proj(x, mod, w["norm_mix"][l], w["w_in"][l])
    if sample is None:
        z3 = z
        bs, l_seq = bg, lg
        chunk = LIN_C
        hgrn0 = jnp.zeros((bs, HA_H, HA_DK, HA_DV), F32)
        gdn0 = jnp.zeros((bs, HC_H, HC_DK, HC_DV), F32)
        conv0 = jnp.zeros((bs, SUBLANES, CONV_CH), F32)
    else:
        bs, l_seq = sample["db"], sample["t"]
        chunk = SUBLANES
        z3 = jnp.pad(z.reshape(bs, l_seq, NP), ((0, 0), (0, SUBLANES - l_seq), (0, 0)))
        hgrn0 = sample["state_hgrn"][l]
        gdn0 = sample["state_gdn"][l]
        conv0 = jnp.pad(sample["state_conv"][l], ((0, 0), (SUBLANES - (CONV_W - 1), 0), (0, 0)))

    o_a, s_hgrn = _hgrn(z3, w["hgrn_lb_logits"], w["hgrn_norm"][l], hgrn0, layer=l, chunk=chunk, l_valid=l_seq)
    o_c, s_gdn = _gdn(z3, w["gdn_conv_w"][l], w["gdn_a_log"][l], w["gdn_dt_bias"][l], w["gdn_norm"][l],
                      conv0, gdn0, chunk=chunk, l_valid=l_seq)
    zseq = z.reshape(bs, l_seq, NP)
    conv_new = zseq[:, l_seq - (CONV_W - 1):, OFF_QC:OFF_QC + CONV_CH]

    prep = _fox_prep(z, w["fox_q_norm"][l], w["fox_k_norm"][l], w["fox_b_f"][l], cumsum=sample is None)
    qn, kn, knb, vnb, lf = prep[:5]
    v_b = z[:, :, OFF_VB:OFF_VB + B_W]
    if sample is None:
        o_b = _fox_attn_prompt(qn, knb, vnb, prep[5])
    else:
        o_b = _fox_attn_sample(
            l, qn.astype(F32).reshape(bs, l_seq, B_W), kn.reshape(bs, l_seq, B_W), v_b.reshape(bs, l_seq, B_W),
            lf[0, :, :HB_H].reshape(bs, l_seq, HB_H), sample["cache_kt"], sample["cache_vt"], sample["cache_lf_t"],
            sample["page_table"])
        o_b = o_b[:, :l_seq].reshape(bg, lg, B_W)
        o_a = o_a[:, :l_seq].reshape(bg, lg, A_W)
        o_c = o_c[:, :l_seq].reshape(bg, lg, C_K)

    x1, h2t, st = _merge(o_a, o_b, o_c, z, x, mod, w["w_br_a"][l], w["w_br_b"][l], w["w_br_c"][l],
                         w["w_out"][l], w["norm_ffn"][l], w["peer_w_q"][l], w["peer_keys"][l])
    stats = _topk(st)
    x2 = _peer(st, stats, h2t, w["peer_u"][l], w["peer_vt"][l], x1, mod)

    k_leaf = kn.reshape(bs, l_seq, HB_H, HB_D)
    v_leaf = v_b.reshape(bs, l_seq, HB_H, HB_D)
    lf_leaf = lf[:, :, :HB_H].reshape(bs, l_seq, HB_H)
    return x2, (k_leaf, v_leaf, lf_leaf, s_hgrn, s_gdn, conv_new)


def _trunk(x, mods, w, sample):
    leaves = [[] for _ in range(6)]
    for l in range(DEPTH):
        x, st = _layer(l, x, mods[l], w, sample)
        for lst, s in zip(leaves, st):
            lst.append(s)
    return x, [jnp.stack(v) for v in leaves]


def kernel(x_prompt, x_sample, c_prompt, c_sample, cache_fox_k, cache_fox_v, cache_fox_logf, page_table,
           state_hgrn, state_gdn, state_gdn_conv, w_ada, b_ada, norm_mix, norm_ffn, w_in, hgrn_lb_logits,
           hgrn_norm, fox_b_f, fox_q_norm, fox_k_norm, gdn_conv_w, gdn_a_log, gdn_dt_bias, gdn_norm,
           w_br_a, w_br_b, w_br_c, w_out, peer_w_q, peer_keys, peer_u, peer_v):
    bp = x_prompt.shape[0]
    db, t_new, _ = x_sample.shape
    n_pool = cache_fox_k.shape[1]

    w = {
        "norm_mix": norm_mix, "norm_ffn": norm_ffn, "hgrn_lb_logits": hgrn_lb_logits, "hgrn_norm": hgrn_norm,
        "fox_b_f": fox_b_f, "fox_q_norm": fox_q_norm, "fox_k_norm": fox_k_norm, "gdn_conv_w": gdn_conv_w,
        "gdn_a_log": gdn_a_log, "gdn_dt_bias": gdn_dt_bias, "gdn_norm": gdn_norm,
        "w_in": [_pack_w_in(w_in[l]) for l in range(DEPTH)],
        "w_br_a": w_br_a.astype(BF16), "w_br_b": w_br_b.astype(BF16), "w_br_c": w_br_c.astype(BF16),
        "w_out": w_out.astype(BF16), "peer_w_q": peer_w_q.astype(BF16), "peer_keys": peer_keys.astype(BF16),
        "peer_u": peer_u.astype(BF16),
        "peer_vt": peer_v.astype(BF16).reshape(DEPTH, N_EXP // (PEER_TI * P_NK), PEER_TI * P_NK, D).transpose(0, 1, 3, 2),
    }

    n_c = bp + db
    c_all = jnp.pad(jnp.concatenate([c_prompt, c_sample], axis=0), ((0, (-n_c) % SUBLANES), (0, 0)))
    mod = _ada(c_all, w_ada, b_ada)
    mods_p = [mod[l, :bp].reshape(bp, 1, 6 * D) for l in range(DEPTH)]
    mods_s = [jnp.repeat(mod[l, bp:n_c], t_new, axis=0).reshape(1, db * t_new, 6 * D) for l in range(DEPTH)]

    sample = {
        "db": db, "t": t_new, "page_table": page_table,
        "state_hgrn": state_hgrn, "state_gdn": state_gdn, "state_conv": state_gdn_conv,
        "cache_kt": cache_fox_k.transpose(0, 1, 3, 4, 2).reshape(DEPTH * n_pool, B_W, PAGE),
        "cache_vt": cache_fox_v.transpose(0, 1, 3, 4, 2).reshape(DEPTH * n_pool, B_W, PAGE),
        "cache_lf_t": cache_fox_logf.transpose(0, 1, 3, 2).reshape(DEPTH * n_pool, HB_H, PAGE),
    }

    y_p, leaves_p = _trunk(x_prompt, mods_p, w, None)
    y_s, leaves_s = _trunk(x_sample.reshape(1, db * t_new, D), mods_s, w, sample)
    return (y_p, y_s.reshape(db, t_new, D), *leaves_p, *leaves_s)
```

```python
import functools
import math

import jax
import jax.numpy as jnp
from jax import lax
from jax.experimental import pallas as pl
from jax.experimental.pallas import tpu as pltpu

F32 = jnp.float32
BF16 = jnp.bfloat16
HI = lax.Precision.HIGHEST
NEG_INF = float("-inf")

D = 1024
DEPTH = 2
HA_H, HA_DK, HA_DV = 4, 128, 128
HB_H, HB_D = 8, 64
HC_H, HC_DK, HC_DV = 4, 128, 128
CONV_W = 4
P_H, P_TOPK, P_NK, P_QD = 8, 16, 128, 256
N_EXP = P_NK * P_NK
EPS = 1e-6
A_W = HA_H * HA_DK
B_W = HB_H * HB_D
C_K = HC_H * HC_DK
CONV_CH = 3 * C_K
PAGE = 128

LANES = 128
SUBLANES = 8
VMEM_LIMIT = 52 * 1024 * 1024

OFF_FA, OFF_QA, OFF_IA, OFF_GA = 0, 512, 1024, 1536
OFF_GATE = 2048
OFF_QB, OFF_KB, OFF_VB = 5120, 5632, 6144
OFF_QC, OFF_KC, OFF_VC = 6656, 7168, 7680
OFF_ZC = 8192
OFF_FB = 8704
OFF_AB = 8832
NP = 8960

LIN_C = 64
HGRN_SUB = 16
HGRN_HEADS_PER_STEP = 2
PEER_JB = 32
PEER_TI = 4
PAGES_PER_STEP = 16


def _dot(a, b):
    return jnp.dot(a, b, preferred_element_type=F32)


def _dot_hi(a, b):
    return jnp.dot(a, b, preferred_element_type=F32, precision=HI)


def _dot_nt(a, b):
    return lax.dot_general(a, b, (((1,), (1,)), ((), ())), preferred_element_type=F32)


def _dot_tn(a, b, precision=None):
    return lax.dot_general(a, b, (((0,), (0,)), ((), ())), preferred_element_type=F32, precision=precision)


def _iota(shape, dim):
    return lax.broadcasted_iota(jnp.int32, shape, dim)


def _sigmoid(x):
    return jax.nn.sigmoid(x)


def _silu(x):
    return x * jax.nn.sigmoid(x)


def _softplus(x):
    return jnp.maximum(x, 0.0) + jnp.log1p(jnp.exp(-jnp.abs(x)))


def _log_sigmoid(x):
    return jnp.minimum(x, 0.0) - jnp.log1p(jnp.exp(-jnp.abs(x)))


def _params(sem):
    return pltpu.CompilerParams(dimension_semantics=sem, vmem_limit_bytes=VMEM_LIMIT)


def _chunk_start(c, c_len):
    return c * c_len if isinstance(c, int) else pl.multiple_of(c * c_len, c_len)


def _for_chunks(n_chunks, body, unroll=False):
    if n_chunks == 1 or unroll:
        for c in range(n_chunks):
            body(c, 0)
    else:
        lax.fori_loop(0, n_chunks, body, 0)


def _tile(n, pref):
    t = min(n, pref)
    assert n % t == 0, (n, pref)
    return t


def _ada_kernel(c_ref, w_ref, b_ref, o_ref):
    sc = _silu(c_ref[...]).astype(BF16)
    o_ref[...] = _dot(sc, w_ref[...].astype(BF16)) + b_ref[...]


def _ada(c_all, w_ada, b_ada):
    rows = c_all.shape[0]
    tn = 768
    return pl.pallas_call(
        _ada_kernel,
        out_shape=jax.ShapeDtypeStruct((DEPTH, rows, 6 * D), F32),
        grid=(DEPTH, 6 * D // tn),
        in_specs=[pl.BlockSpec((rows, D), lambda l, n: (0, 0)),
                  pl.BlockSpec((None, D, tn), lambda l, n: (l, 0, n)),
                  pl.BlockSpec((None, 1, tn), lambda l, n: (l, 0, n))],
        out_specs=pl.BlockSpec((None, rows, tn), lambda l, n: (l, 0, n)),
        compiler_params=_params(("parallel", "parallel")),
        name="ada",
    )(c_all, w_ada, b_ada.reshape(DEPTH, 1, 6 * D))


def _mod_spec(mod, tm, k):
    if mod.shape[1] == 1:
        return pl.BlockSpec((None, 1, D), lambda b, i, *_: (b, 0, k))
    return pl.BlockSpec((None, tm, D), lambda b, i, *_: (b, i, k))


def _in_kernel(x_ref, sh_ref, sc_ref, g_ref, w_ref, z_ref, h_scr):
    @pl.when(pl.program_id(2) == 0)
    def _():
        x = x_ref[...]
        r = lax.rsqrt(jnp.mean(x * x, axis=-1, keepdims=True) + EPS)
        h = (x * r) * g_ref[...] * (1.0 + sc_ref[...]) + sh_ref[...]
        h_scr[...] = h.astype(BF16)

    z_ref[...] = _dot(h_scr[...], w_ref[...])


def _in_proj(x, mod, gain, w_packed):
    bg, lg, _ = x.shape
    tm = _tile(lg, 512)
    tn = 1280
    return pl.pallas_call(
        _in_kernel,
        out_shape=jax.ShapeDtypeStruct((bg, lg, NP), F32),
        grid=(bg, lg // tm, NP // tn),
        in_specs=[pl.BlockSpec((None, tm, D), lambda b, i, n: (b, i, 0)),
                  _mod_spec(mod, tm, 0), _mod_spec(mod, tm, 1),
                  pl.BlockSpec((1, D), lambda b, i, n: (0, 0)),
                  pl.BlockSpec((D, tn), lambda b, i, n: (0, n))],
        out_specs=pl.BlockSpec((None, tm, tn), lambda b, i, n: (b, i, n)),
        scratch_shapes=[pltpu.VMEM((tm, D), BF16)],
        compiler_params=_params(("parallel", "parallel", "arbitrary")),
        name="in_proj",
    )(x, mod, mod, gain.reshape(1, D), w_packed)


def _hgrn_kernel(fa_ref, qa_ref, ia_ref, ga_ref, lbl_ref, nw_ref, s0_ref, o_ref, sout_ref, s_scr,
                 *, layer, chunk, n_chunks, l_valid, l_padded):
    i = pl.program_id(2)
    c_len = chunk

    @pl.when(i == 0)
    def _():
        s_scr[...] = s0_ref[...]

    lg = lbl_ref[...]
    e = jnp.exp(lg - jnp.max(lg, axis=0, keepdims=True))
    p = e / jnp.sum(e, axis=0, keepdims=True)
    cs = p[0:1]
    for j in range(1, layer + 1):
        cs = cs + p[j:j + 1]
    lb = jnp.maximum(cs - p[0:1], 0.0)
    log_lb = jnp.log(lb)
    log1m_lb = jnp.log1p(-lb)

    sub = min(HGRN_SUB, c_len)
    tri = (_iota((c_len, c_len), 1) <= _iota((c_len, c_len), 0)).astype(F32)
    lane = _iota((sub, c_len), 1)
    row1 = _iota((c_len, 1), 0)
    ones_cv = jnp.ones((c_len, HA_DV), F32)

    def chunk_body(c, carry):
        for hh in range(HGRN_HEADS_PER_STEP):
            head_chunk(c, hh)
        return carry

    def head_chunk(c, hh):
        r = _chunk_start(c, c_len)
        hl = slice(hh * HA_DK, (hh + 1) * HA_DK)
        fa = fa_ref[pl.ds(r, c_len), hl]
        qa = qa_ref[pl.ds(r, c_len), hl]
        v = ia_ref[pl.ds(r, c_len), hl]
        ga = ga_ref[pl.ds(r, c_len), hl]

        b_ = log1m_lb[:, hl] + _log_sigmoid(fa)
        log_f = jnp.maximum(log_lb[:, hl], b_) + jnp.log1p(jnp.exp(-jnp.abs(log_lb[:, hl] - b_)))
        k = (1.0 - lb[:, hl]) * _sigmoid(-fa)
        if l_valid < l_padded:
            valid = (i * (n_chunks * c_len) + r + row1) < l_valid
            log_f = jnp.where(valid, log_f, 0.0)
            k = jnp.where(valid, k, 0.0)
        q = _silu(qa)
        a_cum = _dot_hi(tri, log_f)

        blocks = []
        for bi in range(c_len // sub):
            lo, hi = bi * sub, (bi + 1) * sub
            if bi == 0:
                att_b = jnp.zeros((sub, c_len), F32)
            else:
                a_ref = a_cum[lo - 1:lo]
                qs = q[lo:hi] * jnp.exp(a_cum[lo:hi] - a_ref)
                ks = jnp.where(row1 < lo, k * jnp.exp(jnp.minimum(a_ref - a_cum, 0.0)), 0.0)
                att_b = _dot_nt(qs.astype(BF16), ks.astype(BF16))
            for s in range(lo, hi):
                r0 = (s // SUBLANES) * SUBLANES
                rel = a_cum[r0:hi] - a_cum[s:s + 1]
                dec = jnp.exp(jnp.where(row1[r0:hi] >= s, rel, NEG_INF))
                col = jnp.sum(q[r0:hi] * k[s:s + 1] * dec, axis=-1, keepdims=True)
                if r0 > lo:
                    col = jnp.concatenate([jnp.zeros((r0 - lo, 1), F32), col], axis=0)
                att_b = jnp.where(lane == s, col, att_b)
            blocks.append(att_b)
        att = blocks[0] if len(blocks) == 1 else jnp.concatenate(blocks, axis=0)

        s_prev = s_scr[hh]
        qd = q * jnp.exp(a_cum)
        o = _dot(qd.astype(BF16), s_prev.astype(BF16)) + _dot(att.astype(BF16), v.astype(BF16))
        a_last = a_cum[c_len - 1:c_len]
        kd = k * jnp.exp(a_last - a_cum)
        dec_s = jnp.exp(_dot_tn(log_f, ones_cv, precision=HI))
        s_scr[hh] = dec_s * s_prev + _dot_tn(kd.astype(BF16), v.astype(BF16))

        rr = lax.rsqrt(jnp.mean(o * o, axis=-1, keepdims=True) + EPS)
        o_ref[pl.ds(r, c_len), hl] = ((o * rr) * nw_ref[...] * _silu(ga)).astype(o_ref.dtype)

    _for_chunks(n_chunks, chunk_body, unroll=True)

    @pl.when(i == pl.num_programs(2) - 1)
    def _():
        sout_ref[...] = s_scr[...]


def _hgrn(z3, lb_logits, norm_w, s0, *, layer, chunk, l_valid):
    bs, lp, _ = z3.shape
    tb = _tile(lp, 4 * chunk)
    kern = functools.partial(_hgrn_kernel, layer=layer, chunk=chunk, n_chunks=tb // chunk,
                             l_valid=l_valid, l_padded=lp)

    hg = HGRN_HEADS_PER_STEP
    wide = hg * HA_DK

    def col(off):
        return pl.BlockSpec((None, tb, wide), lambda b, h, i: (b, i, off // wide + h))

    return pl.pallas_call(
        kern,
        out_shape=(jax.ShapeDtypeStruct((bs, lp, A_W), BF16),
                   jax.ShapeDtypeStruct((bs, HA_H, HA_DK, HA_DV), F32)),
        grid=(bs, HA_H // hg, lp // tb),
        in_specs=[col(OFF_FA), col(OFF_QA), col(OFF_IA), col(OFF_GA),
                  pl.BlockSpec((DEPTH, wide), lambda b, h, i: (0, h)),
                  pl.BlockSpec((1, HA_DV), lambda b, h, i: (0, 0)),
                  pl.BlockSpec((None, hg, HA_DK, HA_DV), lambda b, h, i: (b, h, 0, 0))],
        out_specs=(pl.BlockSpec((None, tb, wide), lambda b, h, i: (b, i, h)),
                   pl.BlockSpec((None, hg, HA_DK, HA_DV), lambda b, h, i: (b, h, 0, 0))),
        scratch_shapes=[pltpu.VMEM((hg, HA_DK, HA_DV), F32)],
        compiler_params=_params(("parallel", "parallel", "arbitrary")),
        name="hgrn2",
    )(z3, z3, z3, z3, lb_logits, norm_w.reshape(1, HA_DV), s0)


def _gdn_kernel(q_ref, k_ref, v_ref, zc_ref, ab_ref, cw_ref, alog_ref, dt_ref, nw_ref, conv0_ref, s0_ref,
                o_ref, sout_ref, s_scr, prev_scr, act_scr, gb_scr, x_scr,
                *, chunk, n_chunks, l_valid, l_padded):
    i = pl.program_id(1)
    c_len = chunk
    tb = n_chunks * c_len

    @pl.when(i == 0)
    def _():
        s_scr[...] = s0_ref[...]
        prev_scr[...] = conv0_ref[...]

    row8 = _iota((SUBLANES, C_K), 0)

    def conv(x, prev, w):
        y = x * w[CONV_W - 1:CONV_W]
        for j in range(1, CONV_W):
            xr = pltpu.roll(x, j, 0)
            head = jnp.where(row8 < j, pltpu.roll(prev, j, 0), xr[:SUBLANES])
            xs = head if tb == SUBLANES else jnp.concatenate([head, xr[SUBLANES:]], axis=0)
            y = y + xs * w[CONV_W - 1 - j:CONV_W - j]
        return _silu(y)

    for n, ref in enumerate((q_ref, k_ref, v_ref)):
        x = ref[...]
        lo, hi = n * C_K, (n + 1) * C_K
        y = conv(x, prev_scr[:, lo:hi], cw_ref[:, lo:hi])
        for h in range(HC_H):
            act_scr[n * HC_H + h] = y[:, h * HC_DK:(h + 1) * HC_DK]
        prev_scr[:, lo:hi] = x[tb - SUBLANES:]

    ab = ab_ref[...]
    g_all = -jnp.exp(alog_ref[...]) * _softplus(ab + dt_ref[...])
    b_all = _sigmoid(ab)
    if l_valid < l_padded:
        valid = (i * tb + _iota((tb, 1), 0)) < l_valid
        g_all = jnp.where(valid, g_all, 0.0)
        b_all = jnp.where(valid, b_all, 0.0)
    gb_scr[:, 0:LANES] = g_all
    gb_scr[:, LANES:2 * LANES] = b_all

    ii = _iota((c_len, c_len), 0)
    jj = _iota((c_len, c_len), 1)
    tri = (jj <= ii).astype(F32)
    tri_u = (ii <= jj).astype(F32)

    def chunk_body(c, carry):
        r = _chunk_start(c, c_len)
        heads = []
        for h in range(HC_H):
            lo, hi = h * HC_DK, (h + 1) * HC_DK
            qh = act_scr[h, pl.ds(r, c_len), :]
            kh = act_scr[HC_H + h, pl.ds(r, c_len), :]
            vh = act_scr[2 * HC_H + h, pl.ds(r, c_len), :]
            qh = qh * lax.rsqrt(jnp.sum(qh * qh, axis=-1, keepdims=True) + EPS) * (HC_DK ** -0.5)
            kh = kh * lax.rsqrt(jnp.sum(kh * kh, axis=-1, keepdims=True) + EPS)
            g_col = gb_scr[pl.ds(r, c_len), h:h + 1]
            b_col = gb_scr[pl.ds(r, c_len), LANES + HC_H + h:LANES + HC_H + h + 1]
            g_b = jnp.broadcast_to(g_col, (c_len, LANES))
            g_cum = _dot_hi(tri, g_b)
            g_row = _dot_tn(g_b, tri_u, precision=HI)[:c_len]
            rel = g_cum[:, :c_len] - g_row
            d_causal = jnp.exp(jnp.where(jj <= ii, rel, NEG_INF))
            d_strict_t = jnp.exp(jnp.where(jj > ii, -rel, NEG_INF))
            kb = (b_col * kh).astype(BF16)
            khb = kh.astype(BF16)
            l_t = _dot_nt(khb, kb) * d_strict_t
            qk = _dot_nt(qh.astype(BF16), khb) * d_causal
            e_g = jnp.exp(g_cum)
            x_scr[h, 0] = b_col * e_g * kh
            x_scr[h, 1] = b_col * vh
            heads.append((qh, kh, l_t, qk, e_g, g_cum))

        for t in range(1, c_len):
            r1 = ((t + SUBLANES - 1) // SUBLANES) * SUBLANES
            for h in range(HC_H):
                col = heads[h][2][0:r1, t:t + 1]
                for part in range(2):
                    contrib = jnp.sum(x_scr[h, part, 0:r1, :] * col, axis=0, keepdims=True)
                    x_scr[h, part, t:t + 1, :] = x_scr[h, part, t:t + 1, :] - contrib

        for h in range(HC_H):
            qh, kh, _, qk, e_g, g_cum = heads[h]
            lo, hi = h * HC_DV, (h + 1) * HC_DV
            s_prev = s_scr[h]
            s_b = s_prev.astype(BF16)
            u = x_scr[h, 1] - _dot(x_scr[h, 0].astype(BF16), s_b)
            u_b = u.astype(BF16)
            o = e_g * _dot(qh.astype(BF16), s_b) + _dot(qk.astype(BF16), u_b)
            g_last = g_cum[c_len - 1:c_len]
            kd = kh * jnp.exp(g_last - g_cum)
            s_scr[h] = jnp.exp(g_last) * s_prev + _dot_tn(kd.astype(BF16), u_b)
            rr = lax.rsqrt(jnp.mean(o * o, axis=-1, keepdims=True) + EPS)
            zc = zc_ref[pl.ds(r, c_len), lo:hi]
            o_ref[pl.ds(r, c_len), lo:hi] = ((o * rr) * nw_ref[...] * _silu(zc)).astype(o_ref.dtype)
        return carry

    _for_chunks(n_chunks, chunk_body)

    @pl.when(i == pl.num_programs(1) - 1)
    def _():
        sout_ref[...] = s_scr[...]


def _gdn(z3, conv_w, a_log, dt_bias, norm_w, conv0, s0, *, chunk, l_valid):
    bs, lp, _ = z3.shape
    tb = _tile(lp, 4 * chunk)
    kern = functools.partial(_gdn_kernel, chunk=chunk, n_chunks=tb // chunk, l_valid=l_valid, l_padded=lp)
    pad = jnp.zeros((LANES - HC_H,), F32)
    alog_row = jnp.concatenate([a_log, pad]).reshape(1, LANES)
    dt_row = jnp.concatenate([dt_bias, pad]).reshape(1, LANES)

    def wide(off):
        return pl.BlockSpec((None, tb, C_K), lambda b, i: (b, i, off // C_K))

    return pl.pallas_call(
        kern,
        out_shape=(jax.ShapeDtypeStruct((bs, lp, C_K), BF16),
                   jax.ShapeDtypeStruct((bs, HC_H, HC_DK, HC_DV), F32)),
        grid=(bs, lp // tb),
        in_specs=[wide(OFF_QC), wide(OFF_KC), wide(OFF_VC), wide(OFF_ZC),
                  pl.BlockSpec((None, tb, LANES), lambda b, i: (b, i, OFF_AB // LANES)),
                  pl.BlockSpec((CONV_W, CONV_CH), lambda b, i: (0, 0)),
                  pl.BlockSpec((1, LANES), lambda b, i: (0, 0)),
                  pl.BlockSpec((1, LANES), lambda b, i: (0, 0)),
                  pl.BlockSpec((1, HC_DV), lambda b, i: (0, 0)),
                  pl.BlockSpec((None, SUBLANES, CONV_CH), lambda b, i: (b, 0, 0)),
                  pl.BlockSpec((None, HC_H, HC_DK, HC_DV), lambda b, i: (b, 0, 0, 0))],
        out_specs=(pl.BlockSpec((None, tb, C_K), lambda b, i: (b, i, 0)),
                   pl.BlockSpec((None, HC_H, HC_DK, HC_DV), lambda b, i: (b, 0, 0, 0))),
        scratch_shapes=[pltpu.VMEM((HC_H, HC_DK, HC_DV), F32),
                        pltpu.VMEM((SUBLANES, CONV_CH), F32),
                        pltpu.VMEM((3 * HC_H, tb, HC_DK), F32),
                        pltpu.VMEM((tb, 2 * LANES), F32),
                        pltpu.VMEM((HC_H, 2, chunk, HC_DK), F32)],
        compiler_params=_params(("parallel", "arbitrary")),
        name="gdn",
    )(z3, z3, z3, z3, z3, conv_w, alog_row, dt_row, norm_w.reshape(1, HC_DV), conv0, s0)


def _foxprep_kernel(q_ref, k_ref, v_ref, fb_ref, gq_ref, gk_ref, bf_ref, bd_ref,
                    qo_ref, ko_ref, kbo_ref, vbo_ref, lfo_ref, *rest, cumsum):
    bd = bd_ref[...]

    def head_rms(x, g):
        x2 = x * x
        hi = x2.astype(BF16)
        lo = (x2 - hi.astype(F32)).astype(BF16)
        ss = _dot(hi, bd) + _dot(lo, bd)
        return x * lax.rsqrt(ss * (1.0 / HB_D) + EPS) * g

    qn = head_rms(q_ref[...], gq_ref[...])
    kn = head_rms(k_ref[...], gk_ref[...])
    qo_ref[...] = (qn * (HB_D ** -0.5)).astype(BF16)
    ko_ref[...] = kn
    kbo_ref[...] = kn.astype(BF16)
    vbo_ref[...] = v_ref[...].astype(BF16)
    lf = _log_sigmoid(fb_ref[...] + bf_ref[...])
    lf = jnp.where(_iota(lf.shape, 1) < HB_H, lf, 0.0)
    lfo_ref[...] = lf
    if cumsum:
        ft_ref, carry = rest
        tm = lf.shape[0]

        @pl.when(pl.program_id(1) == 0)
        def _():
            carry[...] = jnp.zeros_like(carry)

        tri = (_iota((tm, tm), 1) <= _iota((tm, tm), 0)).astype(F32)
        f_cum = _dot_hi(tri, lf) + carry[...]
        carry[...] = f_cum[tm - 1:tm]
        ft_ref[...] = f_cum.T[:HB_H]


def _fox_prep(z3, gq, gk, b_f, *, cumsum):
    bg, lg, _ = z3.shape
    tm = _tile(lg, 256)
    gq_row = jnp.tile(gq, HB_H).reshape(1, B_W)
    gk_row = jnp.tile(gk, HB_H).reshape(1, B_W)
    bf_row = jnp.concatenate([b_f, jnp.zeros((LANES - HB_H,), F32)]).reshape(1, LANES)
    seg = jnp.arange(B_W) // HB_D
    bd = (seg[:, None] == seg[None, :]).astype(BF16)

    def wide(off):
        return pl.BlockSpec((None, tm, B_W), lambda b, i: (b, i, off // B_W))

    tok = pl.BlockSpec((None, tm, B_W), lambda b, i: (b, i, 0))
    out_shape = [jax.ShapeDtypeStruct((bg, lg, B_W), BF16), jax.ShapeDtypeStruct((bg, lg, B_W), F32),
                 jax.ShapeDtypeStruct((bg, lg, B_W), BF16), jax.ShapeDtypeStruct((bg, lg, B_W), BF16),
                 jax.ShapeDtypeStruct((bg, lg, LANES), F32)]
    out_specs = [tok, tok, tok, tok, pl.BlockSpec((None, tm, LANES), lambda b, i: (b, i, 0))]
    scratch = []
    if cumsum:
        out_shape.append(jax.ShapeDtypeStruct((bg, HB_H, lg), F32))
        out_specs.append(pl.BlockSpec((None, HB_H, tm), lambda b, i: (b, 0, i)))
        scratch.append(pltpu.VMEM((1, LANES), F32))
    return pl.pallas_call(
        functools.partial(_foxprep_kernel, cumsum=cumsum),
        out_shape=tuple(out_shape),
        grid=(bg, lg // tm),
        in_specs=[wide(OFF_QB), wide(OFF_KB), wide(OFF_VB),
                  pl.BlockSpec((None, tm, LANES), lambda b, i: (b, i, OFF_FB // LANES)),
                  pl.BlockSpec((1, B_W), lambda b, i: (0, 0)),
                  pl.BlockSpec((1, B_W), lambda b, i: (0, 0)),
                  pl.BlockSpec((1, LANES), lambda b, i: (0, 0)),
                  pl.BlockSpec((B_W, B_W), lambda b, i: (0, 0))],
        out_specs=tuple(out_specs),
        scratch_shapes=scratch,
        compiler_params=_params(("parallel", "arbitrary")),
        name="fox_prep",
    )(z3, z3, z3, z3, gq_row, gk_row, bf_row, bd)


def _foxattn_kernel(q_ref, k_ref, v_ref, f_ref, o_ref, *, tq):
    qi = pl.program_id(2)
    q = q_ref[...]
    lane_q = _iota(q.shape, 1)
    q_heads = (jnp.where(lane_q < HB_D, q, jnp.zeros_like(q)), jnp.where(lane_q >= HB_D, q, jnp.zeros_like(q)))
    row = _iota((tq, tq), 0)
    colm = _iota((tq, tq), 1)

    def step(j, carry, masked):
        r = pl.multiple_of(j * tq, tq)
        kj = k_ref[pl.ds(r, tq), :]
        vj = v_ref[pl.ds(r, tq), :]
        fj = f_ref[j]
        out = []
        for hh in range(2):
            m, l, acc = carry[hh]
            s = _dot_nt(q_heads[hh], kj) - fj[hh:hh + 1, :]
            if masked:
                s = jnp.where(colm <= row, s, NEG_INF)
            m_new = jnp.maximum(m, jnp.max(s, axis=-1, keepdims=True))
            alpha = jnp.exp(m - m_new)
            p = jnp.exp(s - m_new)
            l = alpha * l + jnp.sum(p, axis=-1, keepdims=True)
            acc = alpha * acc + _dot(p.astype(BF16), vj)
            out.append((m_new, l, acc))
        return tuple(out)

    init = tuple((jnp.full((tq, 1), -1e30, F32), jnp.zeros((tq, 1), F32), jnp.zeros((tq, LANES), F32))
                 for _ in range(2))
    carry = lax.fori_loop(0, qi, lambda j, c: step(j, c, False), init)
    (_, l0, a0), (_, l1, a1) = step(qi, carry, True)
    o = jnp.where(_iota((tq, LANES), 1) < HB_D, a0 / l0, a1 / l1)
    o_ref[...] = o.astype(o_ref.dtype)


def _fox_attn_prompt(qb, kb, vb, ft):
    bg, s_len, _ = qb.shape
    tq = _tile(s_len, 512)
    nk = s_len // tq
    pairs = HB_H // 2
    f5 = ft.reshape(bg, pairs, 2, nk, tq).transpose(0, 1, 3, 2, 4)
    return pl.pallas_call(
        functools.partial(_foxattn_kernel, tq=tq),
        out_shape=jax.ShapeDtypeStruct((bg, s_len, B_W), BF16),
        grid=(bg, pairs, nk),
        in_specs=[pl.BlockSpec((None, tq, LANES), lambda b, p, i: (b, i, p)),
                  pl.BlockSpec((None, s_len, LANES), lambda b, p, i: (b, 0, p)),
                  pl.BlockSpec((None, s_len, LANES), lambda b, p, i: (b, 0, p)),
                  pl.BlockSpec((None, None, nk, 2, tq), lambda b, p, i: (b, p, 0, 0, 0))],
        out_specs=pl.BlockSpec((None, tq, LANES), lambda b, p, i: (b, i, p)),
        compiler_params=_params(("parallel", "parallel", "arbitrary")),
        name="fox_attn_prompt",
    )(qb, kb, vb, f5)


def _foxsample_kernel(pt_ref, q_ref, kn_ref, vn_ref, lfn_ref, *rest, n_new, ps):
    k_pages = rest[0:ps]
    v_pages = rest[ps:2 * ps]
    f_pages = rest[2 * ps:3 * ps]
    o_ref, q_scr, m_scr, l_scr, acc_scr, fc_scr = rest[3 * ps:]
    g = pl.program_id(1)
    rows = n_new * HB_H
    hmask = (_iota((HB_H, B_W), 1) // HB_D) == _iota((HB_H, B_W), 0)

    @pl.when(g == 0)
    def _():
        for t in range(n_new):
            qt = jnp.broadcast_to(q_ref[t:t + 1, :], (HB_H, B_W))
            q_scr[t * HB_H:(t + 1) * HB_H, :] = jnp.where(hmask, qt, 0.0)
        m_scr[...] = jnp.full_like(m_scr, -1e30)
        l_scr[...] = jnp.zeros_like(l_scr)
        acc_scr[...] = jnp.zeros_like(acc_scr)
        fc_scr[...] = jnp.zeros_like(fc_scr)

    tri_u = (_iota((PAGE, PAGE), 0) <= _iota((PAGE, PAGE), 1)).astype(F32)

    def update(s, v_b):
        m = m_scr[...]
        m_new = jnp.maximum(m, jnp.max(s, axis=-1, keepdims=True))
        alpha = jnp.exp(m - m_new)
        p = jnp.exp(s - m_new)
        l_scr[...] = alpha * l_scr[...] + jnp.sum(p, axis=-1, keepdims=True)
        acc_scr[...] = alpha * acc_scr[...] + _dot_nt(p.astype(BF16), v_b)
        m_scr[...] = m_new

    def cum_forget(lf_parts):
        f_loc = [_dot_hi(lf_t, tri_u) for lf_t in lf_parts]
        off = fc_scr[...]
        out = []
        for fl in f_loc:
            out.append(fl + off)
            off = off + fl[:, PAGE - 1:PAGE]
        fc_scr[...] = off
        return out

    f_cat = jnp.concatenate(cum_forget([f_pages[r][...] for r in range(ps)]), axis=1)
    bias = jnp.concatenate([f_cat] * n_new, axis=0)
    k_cat = jnp.concatenate([k_pages[r][...].astype(BF16) for r in range(ps)], axis=1)
    v_cat = jnp.concatenate([v_pages[r][...].astype(BF16) for r in range(ps)], axis=1)
    q_rows = q_scr[...].astype(BF16)
    update(_dot(q_rows, k_cat) - bias, v_cat)

    @pl.when(g == pl.num_programs(1) - 1)
    def _():
        f_new = cum_forget([lfn_ref[...]])[0]
        s = _dot(q_rows, kn_ref[...].astype(BF16)) - jnp.concatenate([f_new] * n_new, axis=0)
        visible = _iota((rows, PAGE), 1) <= (_iota((rows, PAGE), 0) // HB_H)
        update(jnp.where(visible, s, NEG_INF), vn_ref[...].astype(BF16))
        o = acc_scr[...] / l_scr[...]
        out = jnp.zeros((SUBLANES, B_W), F32)
        out_row = _iota((SUBLANES, B_W), 0)
        for t in range(n_new):
            ot = jnp.where(hmask, o[t * HB_H:(t + 1) * HB_H, :], 0.0)
            out = jnp.where(out_row == t, jnp.sum(ot, axis=0, keepdims=True), out)
        o_ref[...] = out.astype(o_ref.dtype)


def _fox_attn_sample(layer, q_new, k_new, v_new, lf_new, cache_kt, cache_vt, cache_lf_t, page_table):
    db, n_new, _ = q_new.shape
    n_pages = page_table.shape[1]
    n_pool = cache_kt.shape[0] // DEPTH
    ps = min(PAGES_PER_STEP, n_pages)
    assert n_pages % ps == 0 and n_new <= SUBLANES
    qp = jnp.pad(q_new, ((0, 0), (0, SUBLANES - n_new), (0, 0)))
    pad_pos = ((0, 0), (0, 0), (0, PAGE - n_new))
    knp = jnp.pad(k_new.transpose(0, 2, 1), pad_pos)
    vnp = jnp.pad(v_new.transpose(0, 2, 1), pad_pos)
    lfp = jnp.pad(lf_new.transpose(0, 2, 1), pad_pos)
    base = layer * n_pool

    def page_spec(r, rows, cols):
        return pl.BlockSpec((None, rows, cols), lambda b, g, pt: (base + pt[b, g * ps + r], 0, 0))

    in_specs = [pl.BlockSpec((None, SUBLANES, B_W), lambda b, g, pt: (b, 0, 0)),
                pl.BlockSpec((None, B_W, PAGE), lambda b, g, pt: (b, 0, 0)),
                pl.BlockSpec((None, B_W, PAGE), lambda b, g, pt: (b, 0, 0)),
                pl.BlockSpec((None, HB_H, PAGE), lambda b, g, pt: (b, 0, 0))]
    in_specs += [page_spec(r, B_W, PAGE) for r in range(ps)]
    in_specs += [page_spec(r, B_W, PAGE) for r in range(ps)]
    in_specs += [page_spec(r, HB_H, PAGE) for r in range(ps)]
    rows = n_new * HB_H
    return pl.pallas_call(
        functools.partial(_foxsample_kernel, n_new=n_new, ps=ps),
        out_shape=jax.ShapeDtypeStruct((db, SUBLANES, B_W), BF16),
        grid_spec=pltpu.PrefetchScalarGridSpec(
            num_scalar_prefetch=1,
            grid=(db, n_pages // ps),
            in_specs=in_specs,
            out_specs=pl.BlockSpec((None, SUBLANES, B_W), lambda b, g, pt: (b, 0, 0)),
            scratch_shapes=[pltpu.VMEM((rows, B_W), F32), pltpu.VMEM((rows, 1), F32),
                            pltpu.VMEM((rows, 1), F32), pltpu.VMEM((rows, B_W), F32),
                            pltpu.VMEM((HB_H, 1), F32)]),
        compiler_params=_params(("parallel", "arbitrary")),
        name="fox_attn_sample",
    )(page_table, qp, knp, vnp, lfp, *([cache_kt] * ps), *([cache_vt] * ps), *([cache_lf_t] * ps))


def _merge_kernel(oa_ref, ob_ref, oc_ref, ga_ref, gb_ref, gc_ref, x_ref, gt1_ref, sh2_ref, sc2_ref,
                  wa_ref, wb_ref, wc_ref, wo_ref, nf_ref, wq_ref, keys_ref,
                  x1_ref, h2t_ref, st_ref):
    merged = (_sigmoid(ga_ref[...]) * _dot(oa_ref[...], wa_ref[...])
              + _sigmoid(gb_ref[...]) * _dot(ob_ref[...], wb_ref[...])
              + _sigmoid(gc_ref[...]) * _dot(oc_ref[...], wc_ref[...]))
    x1 = x_ref[...] + gt1_ref[...] * _dot(merged.astype(BF16), wo_ref[...])
    x1_ref[...] = x1
    r = lax.rsqrt(jnp.mean(x1 * x1, axis=-1, keepdims=True) + EPS)
    h2 = (x1 * r) * nf_ref[...] * (1.0 + sc2_ref[...]) + sh2_ref[...]
    h2t_ref[...] = h2.T.astype(BF16)
    qb = _dot(h2.astype(BF16), wq_ref[...]).astype(BF16)
    half = P_QD // 2
    for h in range(P_H):
        for p in range(2):
            lo = (h * 2 + p) * half
            st_ref[lo:lo + half, :] = _dot_nt(keys_ref[p], qb[:, lo:lo + half])


def _merge(oa, ob, oc, z3, x, mod, w_a, w_b, w_c, w_o, norm_ffn, w_q, keys):
    bg, lg, _ = x.shape
    tm = _tile(lg, 256)
    nl = lg // tm
    t_all = bg * lg

    def tok(width):
        return pl.BlockSpec((None, tm, width), lambda b, i: (b, i, 0))

    def gate(k):
        return pl.BlockSpec((None, tm, D), lambda b, i: (b, i, OFF_GATE // D + k))

    def full(shape):
        return pl.BlockSpec(shape, lambda b, i: (0,) * len(shape))

    return pl.pallas_call(
        _merge_kernel,
        out_shape=(jax.ShapeDtypeStruct((bg, lg, D), F32),
                   jax.ShapeDtypeStruct((D, t_all), BF16),
                   jax.ShapeDtypeStruct((P_H * P_QD, t_all), F32)),
        grid=(bg, nl),
        in_specs=[tok(A_W), tok(B_W), tok(C_K), gate(0), gate(1), gate(2), tok(D),
                  _mod_spec(mod, tm, 2), _mod_spec(mod, tm, 3), _mod_spec(mod, tm, 4),
                  full((A_W, D)), full((B_W, D)), full((C_K, D)), full((D, D)), full((1, D)),
                  full((D, P_H * P_QD)), full((2, P_NK, P_QD // 2))],
        out_specs=(tok(D),
                   pl.BlockSpec((D, tm), lambda b, i: (0, b * nl + i)),
                   pl.BlockSpec((P_H * P_QD, tm), lambda b, i: (0, b * nl + i))),
        compiler_params=_params(("parallel", "parallel")),
        name="merge",
    )(oa, ob, oc, z3, z3, z3, x, mod, mod, mod, w_a, w_b, w_c, w_o, norm_ffn.reshape(1, D), w_q, keys)


def _topk_kernel(st_ref, o_ref):
    tt = st_ref.shape[1]

    def top_rows(s_ref, lo):
        n_v = P_NK // SUBLANES
        v = [s_ref[lo + k * SUBLANES:lo + (k + 1) * SUBLANES, :] for k in range(n_v)]
        k = 2
        while k <= n_v:
            j = k // 2
            while j >= 1:
                for a in range(n_v):
                    b = a ^ j
                    if b > a:
                        hi, lo_v = jnp.maximum(v[a], v[b]), jnp.minimum(v[a], v[b])
                        v[a], v[b] = (hi, lo_v) if (a & k) == 0 else (lo_v, hi)
                j //= 2
            k *= 2
        rows = []
        for it in range(P_TOPK + 1):
            m = jnp.max(v[0], axis=0, keepdims=True)
            rows.append(m)
            popped = v[0] == m
            for a in range(P_TOPK - it):
                v[a] = jnp.where(popped, v[a + 1] if a + 1 < n_v else NEG_INF, v[a])
        return rows

    rank = _iota((P_TOPK, tt), 0)
    rank8 = _iota((SUBLANES, tt), 0)
    thr_rows, nrm_rows = [], []
    for h in range(P_H):
        lo = h * P_QD
        v1 = top_rows(st_ref, lo)
        v2 = top_rows(st_ref, lo + P_NK)
        v1s = jnp.concatenate(v1[:P_TOPK], axis=0)
        v2s = jnp.concatenate(v2[:P_TOPK], axis=0)
        groups = [jnp.where(rank8 == 0, v1[P_TOPK] + v2[0], jnp.where(rank8 == 1, v1[0] + v2[P_TOPK], NEG_INF))]
        for b in range(3):
            groups.append(jnp.where(rank < P_TOPK // (b + 1), v1s + v2[b], NEG_INF))
        for a in range(4):
            nb = P_TOPK // (a + 1)
            n_rows = P_TOPK if nb > SUBLANES else SUBLANES
            rk = _iota((n_rows, tt), 0)
            ok = jnp.where(rk >= 3, rk, nb) < nb
            groups.append(jnp.where(ok, v2s[:n_rows] + v1[a], NEG_INF))
        m_top = None
        z = None
        for it in range(P_TOPK):
            m = functools.reduce(jnp.maximum, [jnp.max(gp, axis=0, keepdims=True) for gp in groups])
            if it == 0:
                m_top = m
                z = jnp.ones_like(m)
            else:
                z = z + jnp.exp(m - m_top)
            groups = [jnp.where(gp == m, NEG_INF, gp) for gp in groups]
        m_next = functools.reduce(jnp.maximum, [jnp.max(gp, axis=0, keepdims=True) for gp in groups])
        thr_rows.append(0.5 * m + 0.5 * m_next)
        nrm_rows.append(-(m_top + jnp.log(z)))
    o_ref[...] = jnp.concatenate(thr_rows + nrm_rows, axis=0)


def _topk(st):
    t_all = st.shape[1]
    tt = _tile(t_all, 256)
    return pl.pallas_call(
        _topk_kernel,
        out_shape=jax.ShapeDtypeStruct((2 * P_H, t_all), F32),
        grid=(t_all // tt,),
        in_specs=[pl.BlockSpec((P_H * P_QD, tt), lambda t: (0, t))],
        out_specs=pl.BlockSpec((2 * P_H, tt), lambda t: (0, t)),
        compiler_params=_params(("parallel",)),
        name="peer_topk",
    )(st)


def _peer_kernel(st_ref, stat_ref, h2t_ref, u0_ref, u1_ref, vt0_ref, vt1_ref, x1_ref, gt2_ref, o_ref,
                 acc_scr, e1_scr, tau_scr, e2_scr, s2_scr, ht0_scr, ht1_scr, wa0_scr, wa1_scr, *, ti, n_tiles):
    s = pl.program_id(2)
    tt = st_ref.shape[1]

    @pl.when(s == 0)
    def _():
        acc_scr[...] = jnp.zeros_like(acc_scr)
        ht0_scr[...] = jnp.zeros_like(ht0_scr)
        ht1_scr[...] = jnp.zeros_like(ht1_scr)
        wa0_scr[...] = jnp.zeros_like(wa0_scr)
        wa1_scr[...] = jnp.zeros_like(wa1_scr)
        for h in range(P_H):
            lo = h * P_QD
            hr = slice(h * P_NK, (h + 1) * P_NK)
            for lt in range(tt // LANES):
                ls = slice(lt * LANES, (lt + 1) * LANES)
                s1 = st_ref[lo:lo + P_NK, ls]
                s2 = st_ref[lo + P_NK:lo + 2 * P_NK, ls]
                mx2 = jnp.max(s2, axis=0, keepdims=True)
                s2_scr[lt, hr, :] = s2
                e2_scr[lt, hr, :] = jnp.exp(s2 - mx2)
                e1_scr[hr, ls] = 0.5 * jnp.exp(s1 + (stat_ref[P_H + h:P_H + h + 1, ls] + mx2))
                tau_scr[hr, ls] = stat_ref[h:h + 1, ls] - s1

    c0 = math.sqrt(2.0 / math.pi)
    c1 = c0 * 0.044715
    tile_b = jnp.clip(s - 1, 0, n_tiles - 1)

    def stages(ht_w, ht_r, wa_w, wa_r):
        te = ti * P_NK
        n_lt = tt // LANES
        per_group = 2 if n_lt % 2 == 0 else 1
        n_p = ti * (n_lt // per_group)
        ka, kc = n_p // 2, 2
        ma, mc = te // 2, D // (n_p // 2)
        wka, wkc = D // ka, te // kc

        def matmul_pieces(pi):
            mh, kq = pi // ka, pi % ka
            u_half = (u0_ref, u1_ref)[mh]
            part = _dot(u_half[:, kq * wka:(kq + 1) * wka], h2t_ref[kq * wka:(kq + 1) * wka, :])
            for lt in range(n_lt):
                piece = part[:, lt * LANES:(lt + 1) * LANES]
                if kq == 0:
                    ht_w[lt, mh * ma:(mh + 1) * ma, :] = piece
                else:
                    ht_w[lt, mh * ma:(mh + 1) * ma, :] += piece
            mq, kh = pi // kc, pi % kc
            vt_half = (vt0_ref, vt1_ref)[mq * mc // (D // 2)]
            v0 = (mq * mc) % (D // 2)
            acc_scr[mq * mc:(mq + 1) * mc, :] += _dot(vt_half[v0:v0 + mc, kh * wkc:(kh + 1) * wkc],
                                                     wa_r[kh * wkc:(kh + 1) * wkc, :])

        for ii in range(ti):
            i_row = tile_b * ti + ii
            tau_rows = [tau_scr[pl.ds(h * P_NK + i_row, 1), :] for h in range(P_H)]
            e1_rows = [e1_scr[pl.ds(h * P_NK + i_row, 1), :] for h in range(P_H)]
            for lt in range(n_lt):
                if lt % per_group == 0:
                    matmul_pieces(ii * (n_lt // per_group) + lt // per_group)
                ls = slice(lt * LANES, (lt + 1) * LANES)
                tau_b = [jnp.broadcast_to(tau_rows[h][:, ls], (PEER_JB, LANES)) for h in range(P_H)]
                e1_b = [jnp.broadcast_to(e1_rows[h][:, ls], (PEER_JB, LANES)) for h in range(P_H)]
                for jb in range(P_NK // PEER_JB):
                    j0 = jb * PEER_JB
                    hs = ht_r[lt, ii * P_NK + j0:ii * P_NK + j0 + PEER_JB, :]
                    act = hs * (1.0 + jnp.tanh(hs * (c0 + c1 * (hs * hs))))
                    w = None
                    for h in range(P_H):
                        jr = slice(h * P_NK + j0, h * P_NK + j0 + PEER_JB)
                        wh = jnp.where(s2_scr[lt, jr, :] >= tau_b[h], e2_scr[lt, jr, :] * e1_b[h], 0.0)
                        w = wh if w is None else w + wh
                    wa_w[ii * P_NK + j0:ii * P_NK + j0 + PEER_JB, ls] = (w * act).astype(BF16)

    @pl.when(s % 2 == 0)
    def _():
        stages(ht0_scr, ht1_scr, wa1_scr, wa0_scr)

    @pl.when(s % 2 == 1)
    def _():
        stages(ht1_scr, ht0_scr, wa0_scr, wa1_scr)

    @pl.when(s == pl.num_programs(2) - 1)
    def _():
        gt2 = gt2_ref[...] if gt2_ref.shape[0] == tt else gt2_ref[0:1, :]
        o_ref[...] = x1_ref[...] + gt2 * acc_scr[...].T


def _peer(st, stats, h2t, u_b, vt_b, x1, mod):
    bg, lg, _ = x1.shape
    tt = _tile(lg, 512)
    nl = lg // tt
    ti = PEER_TI
    te = ti * P_NK
    assert vt_b.shape == (N_EXP // te, D, te)
    n_tiles = N_EXP // te
    if mod.shape[1] == 1:
        mod = jnp.broadcast_to(mod, (bg, SUBLANES, 6 * D))
        gate_spec = pl.BlockSpec((None, SUBLANES, D), lambda b, i, s: (b, 0, 5))
    else:
        gate_spec = _mod_spec(mod, tt, 5)
    return pl.pallas_call(
        functools.partial(_peer_kernel, ti=ti, n_tiles=n_tiles),
        out_shape=jax.ShapeDtypeStruct((bg, lg, D), F32),
        grid=(bg, nl, n_tiles + 2),
        in_specs=[pl.BlockSpec((P_H * P_QD, tt), lambda b, i, s: (0, b * nl + i)),
                  pl.BlockSpec((2 * P_H, tt), lambda b, i, s: (0, b * nl + i)),
                  pl.BlockSpec((D, tt), lambda b, i, s: (0, b * nl + i)),
                  pl.BlockSpec((te // 2, D), lambda b, i, s: (2 * jnp.minimum(s, n_tiles - 1), 0)),
                  pl.BlockSpec((te // 2, D), lambda b, i, s: (2 * jnp.minimum(s, n_tiles - 1) + 1, 0)),
                  pl.BlockSpec((None, D // 2, te), lambda b, i, s: (jnp.clip(s - 2, 0, n_tiles - 1), 0, 0)),
                  pl.BlockSpec((None, D // 2, te), lambda b, i, s: (jnp.clip(s - 2, 0, n_tiles - 1), 1, 0)),
                  pl.BlockSpec((None, tt, D), lambda b, i, s: (b, i, 0)),
                  gate_spec],
        out_specs=pl.BlockSpec((None, tt, D), lambda b, i, s: (b, i, 0)),
        scratch_shapes=[pltpu.VMEM((D, tt), F32),
                        pltpu.VMEM((P_H * P_NK, tt), F32), pltpu.VMEM((P_H * P_NK, tt), F32),
                        pltpu.VMEM((tt // LANES, P_H * P_NK, LANES), F32),
                        pltpu.VMEM((tt // LANES, P_H * P_NK, LANES), F32),
                        pltpu.VMEM((tt // LANES, te, LANES), F32), pltpu.VMEM((tt // LANES, te, LANES), F32),
                        pltpu.VMEM((te, tt), BF16), pltpu.VMEM((te, tt), BF16)],
        compiler_params=_params(("parallel", "parallel", "arbitrary")),
        name="peer_experts",
    )(st, stats, h2t, u_b, u_b, vt_b, vt_b, x1, mod)


def _pack_w_in(w):
    pad = jnp.zeros((D, LANES - 8), w.dtype)
    o_fb = 2 * A_W + 2 * A_W + 3 * B_W
    o_c = o_fb + HB_H
    o_ab = o_c + 3 * C_K
    o_zc = o_ab + 2 * HC_H
    o_gate = o_zc + C_K
    packed = jnp.concatenate([
        w[:, 0:4 * A_W], w[:, o_gate:o_gate + 3 * D], w[:, 4 * A_W:o_fb], w[:, o_c:o_ab],
        w[:, o_zc:o_gate], w[:, o_fb:o_c], pad, w[:, o_ab:o_zc], pad], axis=1)
    assert packed.shape[1] == NP
    return packed.astype(BF16)


def _layer(l, x, mod, w, sample):
    bg, lg, _ = x.shape
    z = _in_proj(x, mod, w["norm_mix"][l], w["w_in"][l])
    if sample is None:
        z3 = z
        bs, l_seq = bg, lg
        chunk = LIN_C
        hgrn0 = jnp.zeros((bs, HA_H, HA_DK, HA_DV), F32)
        gdn0 = jnp.zeros((bs, HC_H, HC_DK, HC_DV), F32)
        conv0 = jnp.zeros((bs, SUBLANES, CONV_CH), F32)
    else:
        bs, l_seq = sample["db"], sample["t"]
        chunk = SUBLANES
        z3 = jnp.pad(z.reshape(bs, l_seq, NP), ((0, 0), (0, SUBLANES - l_seq), (0, 0)))
        hgrn0 = sample["state_hgrn"][l]
        gdn0 = sample["state_gdn"][l]
        conv0 = jnp.pad(sample["state_conv"][l], ((0, 0), (SUBLANES - (CONV_W - 1), 0), (0, 0)))

    o_a, s_hgrn = _hgrn(z3, w["hgrn_lb_logits"], w["hgrn_norm"][l], hgrn0, layer=l, chunk=chunk, l_valid=l_seq)
    o_c, s_gdn = _gdn(z3, w["gdn_conv_w"][l], w["gdn_a_log"][l], w["gdn_dt_bias"][l], w["gdn_norm"][l],
                      conv0, gdn0, chunk=chunk, l_valid=l_seq)
    zseq = z.reshape(bs, l_seq, NP)
    conv_new = zseq[:, l_seq - (CONV_W - 1):, OFF_QC:OFF_QC + CONV_CH]

    prep = _fox_prep(z, w["fox_q_norm"][l], w["fox_k_norm"][l], w["fox_b_f"][l], cumsum=sample is None)
    qn, kn, knb, vnb, lf = prep[:5]
    v_b = z[:, :, OFF_VB:OFF_VB + B_W]
    if sample is None:
        o_b = _fox_attn_prompt(qn, knb, vnb, prep[5])
    else:
        o_b = _fox_attn_sample(
            l, qn.astype(F32).reshape(bs, l_seq, B_W), kn.reshape(bs, l_seq, B_W), v_b.reshape(bs, l_seq, B_W),
            lf[0, :, :HB_H].reshape(bs, l_seq, HB_H), sample["cache_kt"], sample["cache_vt"], sample["cache_lf_t"],
            sample["page_table"])
        o_b = o_b[:, :l_seq].reshape(bg, lg, B_W)
        o_a = o_a[:, :l_seq].reshape(bg, lg, A_W)
        o_c = o_c[:, :l_seq].reshape(bg, lg, C_K)

    x1, h2t, st = _merge(o_a, o_b, o_c, z, x, mod, w["w_br_a"][l], w["w_br_b"][l], w["w_br_c"][l],
                         w["w_out"][l], w["norm_ffn"][l], w["peer_w_q"][l], w["peer_keys"][l])
    stats = _topk(st)
    x2 = _peer(st, stats, h2t, w["peer_u"][l], w["peer_vt"][l], x1, mod)

    k_leaf = kn.reshape(bs, l_seq, HB_H, HB_D)
    v_leaf = v_b.reshape(bs, l_seq, HB_H, HB_D)
    lf_leaf = lf[:, :, :HB_H].reshape(bs, l_seq, HB_H)
    return x2, (k_leaf, v_leaf, lf_leaf, s_hgrn, s_gdn, conv_new)


def _trunk(x, mods, w, sample):
    leaves = [[] for _ in range(6)]
    for l in range(DEPTH):
        x, st = _layer(l, x, mods[l], w, sample)
        for lst, s in zip(leaves, st):
            lst.append(s)
    return x, [jnp.stack(v) for v in leaves]


def kernel(x_prompt, x_sample, c_prompt, c_sample, cache_fox_k, cache_fox_v, cache_fox_logf, page_table,
           state_hgrn, state_gdn, state_gdn_conv, w_ada, b_ada, norm_mix, norm_ffn, w_in, hgrn_lb_logits,
           hgrn_norm, fox_b_f, fox_q_norm, fox_k_norm, gdn_conv_w, gdn_a_log, gdn_dt_bias, gdn_norm,
           w_br_a, w_br_b, w_br_c, w_out, peer_w_q, peer_keys, peer_u, peer_v):
    bp = x_prompt.shape[0]
    db, t_new, _ = x_sample.shape
    n_pool = cache_fox_k.shape[1]

    w = {
        "norm_mix": norm_mix, "norm_ffn": norm_ffn, "hgrn_lb_logits": hgrn_lb_logits, "hgrn_norm": hgrn_norm,
        "fox_b_f": fox_b_f, "fox_q_norm": fox_q_norm, "fox_k_norm": fox_k_norm, "gdn_conv_w": gdn_conv_w,
        "gdn_a_log": gdn_a_log, "gdn_dt_bias": gdn_dt_bias, "gdn_norm": gdn_norm,
        "w_in": [_pack_w_in(w_in[l]) for l in range(DEPTH)],
        "w_br_a": w_br_a.astype(BF16), "w_br_b": w_br_b.astype(BF16), "w_br_c": w_br_c.astype(BF16),
        "w_out": w_out.astype(BF16), "peer_w_q": peer_w_q.astype(BF16), "peer_keys": peer_keys.astype(BF16),
        "peer_u": peer_u.astype(BF16),
        "peer_vt": peer_v.astype(BF16).reshape(DEPTH, N_EXP // (PEER_TI * P_NK), PEER_TI * P_NK, D).transpose(0, 1, 3, 2),
    }

    n_c = bp + db
    c_all = jnp.pad(jnp.concatenate([c_prompt, c_sample], axis=0), ((0, (-n_c) % SUBLANES), (0, 0)))
    mod = _ada(c_all, w_ada, b_ada)
    mods_p = [mod[l, :bp].reshape(bp, 1, 6 * D) for l in range(DEPTH)]
    mods_s = [jnp.repeat(mod[l, bp:n_c], t_new, axis=0).reshape(1, db * t_new, 6 * D) for l in range(DEPTH)]

    sample = {
        "db": db, "t": t_new, "page_table": page_table,
        "state_hgrn": state_hgrn, "state_gdn": state_gdn, "state_conv": state_gdn_conv,
        "cache_kt": cache_fox_k.transpose(0, 1, 3, 4, 2).reshape(DEPTH * n_pool, B_W, PAGE),
        "cache_vt": cache_fox_v.transpose(0, 1, 3, 4, 2).reshape(DEPTH * n_pool, B_W, PAGE),
        "cache_lf_t": cache_fox_logf.transpose(0, 1, 3, 2).reshape(DEPTH * n_pool, HB_H, PAGE),
    }

    y_p, leaves_p = _trunk(x_prompt, mods_p, w, None)
    y_s, leaves_s = _trunk(x_sample.reshape(1, db * t_new, D), mods_s, w, sample)
    return (y_p, y_s.reshape(db, t_new, D), *leaves_p, *leaves_s)
```

```python
import functools
import math

import jax
import jax.numpy as jnp
from jax import lax
from jax.experimental import pallas as pl
from jax.experimental.pallas import tpu as pltpu

F32 = jnp.float32
BF16 = jnp.bfloat16
HI = lax.Precision.HIGHEST
NEG_INF = float("-inf")

D = 1024
DEPTH = 2
HA_H, HA_DK, HA_DV = 4, 128, 128
HB_H, HB_D = 8, 64
HC_H, HC_DK, HC_DV = 4, 128, 128
CONV_W = 4
P_H, P_TOPK, P_NK, P_QD = 8, 16, 128, 256
N_EXP = P_NK * P_NK
EPS = 1e-6
A_W = HA_H * HA_DK
B_W = HB_H * HB_D
C_K = HC_H * HC_DK
CONV_CH = 3 * C_K
PAGE = 128

LANES = 128
SUBLANES = 8
VMEM_LIMIT = 52 * 1024 * 1024

OFF_FA, OFF_QA, OFF_IA, OFF_GA = 0, 512, 1024, 1536
OFF_GATE = 2048
OFF_QB, OFF_KB, OFF_VB = 5120, 5632, 6144
OFF_QC, OFF_KC, OFF_VC = 6656, 7168, 7680
OFF_ZC = 8192
OFF_FB = 8704
OFF_AB = 8832
NP = 8960

LIN_C = 64
HGRN_SUB = 16
HGRN_HEADS_PER_STEP = 2
PEER_JB = 32
PEER_TI = 4
PAGES_PER_STEP = 16


def _dot(a, b):
    return jnp.dot(a, b, preferred_element_type=F32)


def _dot_hi(a, b):
    return jnp.dot(a, b, preferred_element_type=F32, precision=HI)


def _dot_nt(a, b):
    return lax.dot_general(a, b, (((1,), (1,)), ((), ())), preferred_element_type=F32)


def _dot_tn(a, b, precision=None):
    return lax.dot_general(a, b, (((0,), (0,)), ((), ())), preferred_element_type=F32, precision=precision)


def _iota(shape, dim):
    return lax.broadcasted_iota(jnp.int32, shape, dim)


def _sigmoid(x):
    return jax.nn.sigmoid(x)


def _silu(x):
    return x * jax.nn.sigmoid(x)


def _softplus(x):
    return jnp.maximum(x, 0.0) + jnp.log1p(jnp.exp(-jnp.abs(x)))


def _log_sigmoid(x):
    return jnp.minimum(x, 0.0) - jnp.log1p(jnp.exp(-jnp.abs(x)))


def _params(sem):
    return pltpu.CompilerParams(dimension_semantics=sem, vmem_limit_bytes=VMEM_LIMIT)


def _chunk_start(c, c_len):
    return c * c_len if isinstance(c, int) else pl.multiple_of(c * c_len, c_len)


def _for_chunks(n_chunks, body, unroll=False):
    if n_chunks == 1 or unroll:
        for c in range(n_chunks):
            body(c, 0)
    else:
        lax.fori_loop(0, n_chunks, body, 0)


def _tile(n, pref):
    t = min(n, pref)
    assert n % t == 0, (n, pref)
    return t


def _ada_kernel(c_ref, w_ref, b_ref, o_ref):
    sc = _silu(c_ref[...]).astype(BF16)
    o_ref[...] = _dot(sc, w_ref[...].astype(BF16)) + b_ref[...]


def _ada(c_all, w_ada, b_ada):
    rows = c_all.shape[0]
    tn = 768
    return pl.pallas_call(
        _ada_kernel,
        out_shape=jax.ShapeDtypeStruct((DEPTH, rows, 6 * D), F32),
        grid=(DEPTH, 6 * D // tn),
        in_specs=[pl.BlockSpec((rows, D), lambda l, n: (0, 0)),
                  pl.BlockSpec((None, D, tn), lambda l, n: (l, 0, n)),
                  pl.BlockSpec((None, 1, tn), lambda l, n: (l, 0, n))],
        out_specs=pl.BlockSpec((None, rows, tn), lambda l, n: (l, 0, n)),
        compiler_params=_params(("parallel", "parallel")),
        name="ada",
    )(c_all, w_ada, b_ada.reshape(DEPTH, 1, 6 * D))


def _mod_spec(mod, tm, k):
    if mod.shape[1] == 1:
        return pl.BlockSpec((None, 1, D), lambda b, i, *_: (b, 0, k))
    return pl.BlockSpec((None, tm, D), lambda b, i, *_: (b, i, k))


def _in_kernel(x_ref, sh_ref, sc_ref, g_ref, w_ref, z_ref, h_scr):
    @pl.when(pl.program_id(2) == 0)
    def _():
        x = x_ref[...]
        r = lax.rsqrt(jnp.mean(x * x, axis=-1, keepdims=True) + EPS)
        h = (x * r) * g_ref[...] * (1.0 + sc_ref[...]) + sh_ref[...]
        h_scr[...] = h.astype(BF16)

    z_ref[...] = _dot(h_scr[...], w_ref[...])


def _in_proj(x, mod, gain, w_packed):
    bg, lg, _ = x.shape
    tm = _tile(lg, 512)
    tn = 1280
    return pl.pallas_call(
        _in_kernel,
        out_shape=jax.ShapeDtypeStruct((bg, lg, NP), F32),
        grid=(bg, lg // tm, NP // tn),
        in_specs=[pl.BlockSpec((None, tm, D), lambda b, i, n: (b, i, 0)),
                  _mod_spec(mod, tm, 0), _mod_spec(mod, tm, 1),
                  pl.BlockSpec((1, D), lambda b, i, n: (0, 0)),
                  pl.BlockSpec((D, tn), lambda b, i, n: (0, n))],
        out_specs=pl.BlockSpec((None, tm, tn), lambda b, i, n: (b, i, n)),
        scratch_shapes=[pltpu.VMEM((tm, D), BF16)],
        compiler_params=_params(("parallel", "parallel", "arbitrary")),
        name="in_proj",
    )(x, mod, mod, gain.reshape(1, D), w_packed)


def _hgrn_kernel(fa_ref, qa_ref, ia_ref, ga_ref, lbl_ref, nw_ref, s0_ref, o_ref, sout_ref, s_scr,
                 *, layer, chunk, n_chunks, l_valid, l_padded):
    i = pl.program_id(2)
    c_len = chunk

    @pl.when(i == 0)
    def _():
        s_scr[...] = s0_ref[...]

    lg = lbl_ref[...]
    e = jnp.exp(lg - jnp.max(lg, axis=0, keepdims=True))
    p = e / jnp.sum(e, axis=0, keepdims=True)
    cs = p[0:1]
    for j in range(1, layer + 1):
        cs = cs + p[j:j + 1]
    lb = jnp.maximum(cs - p[0:1], 0.0)
    log_lb = jnp.log(lb)
    log1m_lb = jnp.log1p(-lb)

    sub = min(HGRN_SUB, c_len)
    tri = (_iota((c_len, c_len), 1) <= _iota((c_len, c_len), 0)).astype(F32)
    lane = _iota((sub, c_len), 1)
    row1 = _iota((c_len, 1), 0)
    ones_cv = jnp.ones((c_len, HA_DV), F32)

    def chunk_body(c, carry):
        for hh in range(HGRN_HEADS_PER_STEP):
            head_chunk(c, hh)
        return carry

    def head_chunk(c, hh):
        r = _chunk_start(c, c_len)
        hl = slice(hh * HA_DK, (hh + 1) * HA_DK)
        fa = fa_ref[pl.ds(r, c_len), hl]
        qa = qa_ref[pl.ds(r, c_len), hl]
        v = ia_ref[pl.ds(r, c_len), hl]
        ga = ga_ref[pl.ds(r, c_len), hl]

        b_ = log1m_lb[:, hl] + _log_sigmoid(fa)
        log_f = jnp.maximum(log_lb[:, hl], b_) + jnp.log1p(jnp.exp(-jnp.abs(log_lb[:, hl] - b_)))
        k = (1.0 - lb[:, hl]) * _sigmoid(-fa)
        if l_valid < l_padded:
            valid = (i * (n_chunks * c_len) + r + row1) < l_valid
            log_f = jnp.where(valid, log_f, 0.0)
            k = jnp.where(valid, k, 0.0)
        q = _silu(qa)
        a_cum = _dot_hi(tri, log_f)

        blocks = []
        for bi in range(c_len // sub):
            lo, hi = bi * sub, (bi + 1) * sub
            if bi == 0:
                att_b = jnp.zeros((sub, c_len), F32)
            else:
                a_ref = a_cum[lo - 1:lo]
                qs = q[lo:hi] * jnp.exp(a_cum[lo:hi] - a_ref)
                ks = jnp.where(row1 < lo, k * jnp.exp(jnp.minimum(a_ref - a_cum, 0.0)), 0.0)
                att_b = _dot_nt(qs.astype(BF16), ks.astype(BF16))
            for s in range(lo, hi):
                r0 = (s // SUBLANES) * SUBLANES
                rel = a_cum[r0:hi] - a_cum[s:s + 1]
                dec = jnp.exp(jnp.where(row1[r0:hi] >= s, rel, NEG_INF))
                col = jnp.sum(q[r0:hi] * k[s:s + 1] * dec, axis=-1, keepdims=True)
                if r0 > lo:
                    col = jnp.concatenate([jnp.zeros((r0 - lo, 1), F32), col], axis=0)
                att_b = jnp.where(lane == s, col, att_b)
            blocks.append(att_b)
        att = blocks[0] if len(blocks) == 1 else jnp.concatenate(blocks, axis=0)

        s_prev = s_scr[hh]
        qd = q * jnp.exp(a_cum)
        o = _dot(qd.astype(BF16), s_prev.astype(BF16)) + _dot(att.astype(BF16), v.astype(BF16))
        a_last = a_cum[c_len - 1:c_len]
        kd = k * jnp.exp(a_last - a_cum)
        dec_s = jnp.exp(_dot_tn(log_f, ones_cv, precision=HI))
        s_scr[hh] = dec_s * s_prev + _dot_tn(kd.astype(BF16), v.astype(BF16))

        rr = lax.rsqrt(jnp.mean(o * o, axis=-1, keepdims=True) + EPS)
        o_ref[pl.ds(r, c_len), hl] = ((o * rr) * nw_ref[...] * _silu(ga)).astype(o_ref.dtype)

    _for_chunks(n_chunks, chunk_body, unroll=True)

    @pl.when(i == pl.num_programs(2) - 1)
    def _():
        sout_ref[...] = s_scr[...]


def _hgrn(z3, lb_logits, norm_w, s0, *, layer, chunk, l_valid):
    bs, lp, _ = z3.shape
    tb = _tile(lp, 4 * chunk)
    kern = functools.partial(_hgrn_kernel, layer=layer, chunk=chunk, n_chunks=tb // chunk,
                             l_valid=l_valid, l_padded=lp)

    hg = HGRN_HEADS_PER_STEP
    wide = hg * HA_DK

    def col(off):
        return pl.BlockSpec((None, tb, wide), lambda b, h, i: (b, i, off // wide + h))

    return pl.pallas_call(
        kern,
        out_shape=(jax.ShapeDtypeStruct((bs, lp, A_W), BF16),
                   jax.ShapeDtypeStruct((bs, HA_H, HA_DK, HA_DV), F32)),
        grid=(bs, HA_H // hg, lp // tb),
        in_specs=[col(OFF_FA), col(OFF_QA), col(OFF_IA), col(OFF_GA),
                  pl.BlockSpec((DEPTH, wide), lambda b, h, i: (0, h)),
                  pl.BlockSpec((1, HA_DV), lambda b, h, i: (0, 0)),
                  pl.BlockSpec((None, hg, HA_DK, HA_DV), lambda b, h, i: (b, h, 0, 0))],
        out_specs=(pl.BlockSpec((None, tb, wide), lambda b, h, i: (b, i, h)),
                   pl.BlockSpec((None, hg, HA_DK, HA_DV), lambda b, h, i: (b, h, 0, 0))),
        scratch_shapes=[pltpu.VMEM((hg, HA_DK, HA_DV), F32)],
        compiler_params=_params(("parallel", "parallel", "arbitrary")),
        name="hgrn2",
    )(z3, z3, z3, z3, lb_logits, norm_w.reshape(1, HA_DV), s0)


def _gdn_kernel(q_ref, k_ref, v_ref, zc_ref, ab_ref, cw_ref, alog_ref, dt_ref, nw_ref, conv0_ref, s0_ref,
                o_ref, sout_ref, s_scr, prev_scr, act_scr, gb_scr, x_scr,
                *, chunk, n_chunks, l_valid, l_padded):
    i = pl.program_id(1)
    c_len = chunk
    tb = n_chunks * c_len

    @pl.when(i == 0)
    def _():
        s_scr[...] = s0_ref[...]
        prev_scr[...] = conv0_ref[...]

    row8 = _iota((SUBLANES, C_K), 0)

    def conv(x, prev, w):
        y = x * w[CONV_W - 1:CONV_W]
        for j in range(1, CONV_W):
            xr = pltpu.roll(x, j, 0)
            head = jnp.where(row8 < j, pltpu.roll(prev, j, 0), xr[:SUBLANES])
            xs = head if tb == SUBLANES else jnp.concatenate([head, xr[SUBLANES:]], axis=0)
            y = y + xs * w[CONV_W - 1 - j:CONV_W - j]
        return _silu(y)

    for n, ref in enumerate((q_ref, k_ref, v_ref)):
        x = ref[...]
        lo, hi = n * C_K, (n + 1) * C_K
        y = conv(x, prev_scr[:, lo:hi], cw_ref[:, lo:hi])
        for h in range(HC_H):
            act_scr[n * HC_H + h] = y[:, h * HC_DK:(h + 1) * HC_DK]
        prev_scr[:, lo:hi] = x[tb - SUBLANES:]

    ab = ab_ref[...]
    g_all = -jnp.exp(alog_ref[...]) * _softplus(ab + dt_ref[...])
    b_all = _sigmoid(ab)
    if l_valid < l_padded:
        valid = (i * tb + _iota((tb, 1), 0)) < l_valid
        g_all = jnp.where(valid, g_all, 0.0)
        b_all = jnp.where(valid, b_all, 0.0)
    gb_scr[:, 0:LANES] = g_all
    gb_scr[:, LANES:2 * LANES] = b_all

    ii = _iota((c_len, c_len), 0)
    jj = _iota((c_len, c_len), 1)
    tri = (jj <= ii).astype(F32)
    tri_u = (ii <= jj).astype(F32)

    def chunk_body(c, carry):
        r = _chunk_start(c, c_len)
        heads = []
        for h in range(HC_H):
            lo, hi = h * HC_DK, (h + 1) * HC_DK
            qh = act_scr[h, pl.ds(r, c_len), :]
            kh = act_scr[HC_H + h, pl.ds(r, c_len), :]
            vh = act_scr[2 * HC_H + h, pl.ds(r, c_len), :]
            qh = qh * lax.rsqrt(jnp.sum(qh * qh, axis=-1, keepdims=True) + EPS) * (HC_DK ** -0.5)
            kh = kh * lax.rsqrt(jnp.sum(kh * kh, axis=-1, keepdims=True) + EPS)
            g_col = gb_scr[pl.ds(r, c_len), h:h + 1]
            b_col = gb_scr[pl.ds(r, c_len), LANES + HC_H + h:LANES + HC_H + h + 1]
            g_b = jnp.broadcast_to(g_col, (c_len, LANES))
            g_cum = _dot_hi(tri, g_b)
            g_row = _dot_tn(g_b, tri_u, precision=HI)[:c_len]
            rel = g_cum[:, :c_len] - g_row
            d_causal = jnp.exp(jnp.where(jj <= ii, rel, NEG_INF))
            d_strict_t = jnp.exp(jnp.where(jj > ii, -rel, NEG_INF))
            kb = (b_col * kh).astype(BF16)
            khb = kh.astype(BF16)
            l_t = _dot_nt(khb, kb) * d_strict_t
            qk = _dot_nt(qh.astype(BF16), khb) * d_causal
            e_g = jnp.exp(g_cum)
            x_scr[h, 0] = b_col * e_g * kh
            x_scr[h, 1] = b_col * vh
            heads.append((qh, kh, l_t, qk, e_g, g_cum))

        for t in range(1, c_len):
            r1 = ((t + SUBLANES - 1) // SUBLANES) * SUBLANES
            for h in range(HC_H):
                col = heads[h][2][0:r1, t:t + 1]
                for part in range(2):
                    contrib = jnp.sum(x_scr[h, part, 0:r1, :] * col, axis=0, keepdims=True)
                    x_scr[h, part, t:t + 1, :] = x_scr[h, part, t:t + 1, :] - contrib

        for h in range(HC_H):
            qh, kh, _, qk, e_g, g_cum = heads[h]
            lo, hi = h * HC_DV, (h + 1) * HC_DV
            s_prev = s_scr[h]
            s_b = s_prev.astype(BF16)
            u = x_scr[h, 1] - _dot(x_scr[h, 0].astype(BF16), s_b)
            u_b = u.astype(BF16)
            o = e_g * _dot(qh.astype(BF16), s_b) + _dot(qk.astype(BF16), u_b)
            g_last = g_cum[c_len - 1:c_len]
            kd = kh * jnp.exp(g_last - g_cum)
            s_scr[h] = jnp.exp(g_last) * s_prev + _dot_tn(kd.astype(BF16), u_b)
            rr = lax.rsqrt(jnp.mean(o * o, axis=-1, keepdims=True) + EPS)
            zc = zc_ref[pl.ds(r, c_len), lo:hi]
            o_ref[pl.ds(r, c_len), lo:hi] = ((o * rr) * nw_ref[...] * _silu(zc)).astype(o_ref.dtype)
        return carry

    _for_chunks(n_chunks, chunk_body)

    @pl.when(i == pl.num_programs(1) - 1)
    def _():
        sout_ref[...] = s_scr[...]


def _gdn(z3, conv_w, a_log, dt_bias, norm_w, conv0, s0, *, chunk, l_valid):
    bs, lp, _ = z3.shape
    tb = _tile(lp, 4 * chunk)
    kern = functools.partial(_gdn_kernel, chunk=chunk, n_chunks=tb // chunk, l_valid=l_valid, l_padded=lp)
    pad = jnp.zeros((LANES - HC_H,), F32)
    alog_row = jnp.concatenate([a_log, pad]).reshape(1, LANES)
    dt_row = jnp.concatenate([dt_bias, pad]).reshape(1, LANES)

    def wide(off):
        return pl.BlockSpec((None, tb, C_K), lambda b, i: (b, i, off // C_K))

    return pl.pallas_call(
        kern,
        out_shape=(jax.ShapeDtypeStruct((bs, lp, C_K), BF16),
                   jax.ShapeDtypeStruct((bs, HC_H, HC_DK, HC_DV), F32)),
        grid=(bs, lp // tb),
        in_specs=[wide(OFF_QC), wide(OFF_KC), wide(OFF_VC), wide(OFF_ZC),
                  pl.BlockSpec((None, tb, LANES), lambda b, i: (b, i, OFF_AB // LANES)),
                  pl.BlockSpec((CONV_W, CONV_CH), lambda b, i: (0, 0)),
                  pl.BlockSpec((1, LANES), lambda b, i: (0, 0)),
                  pl.BlockSpec((1, LANES), lambda b, i: (0, 0)),
                  pl.BlockSpec((1, HC_DV), lambda b, i: (0, 0)),
                  pl.BlockSpec((None, SUBLANES, CONV_CH), lambda b, i: (b, 0, 0)),
                  pl.BlockSpec((None, HC_H, HC_DK, HC_DV), lambda b, i: (b, 0, 0, 0))],
        out_specs=(pl.BlockSpec((None, tb, C_K), lambda b, i: (b, i, 0)),
                   pl.BlockSpec((None, HC_H, HC_DK, HC_DV), lambda b, i: (b, 0, 0, 0))),
        scratch_shapes=[pltpu.VMEM((HC_H, HC_DK, HC_DV), F32),
                        pltpu.VMEM((SUBLANES, CONV_CH), F32),
                        pltpu.VMEM((3 * HC_H, tb, HC_DK), F32),
                        pltpu.VMEM((tb, 2 * LANES), F32),
                        pltpu.VMEM((HC_H, 2, chunk, HC_DK), F32)],
        compiler_params=_params(("parallel", "arbitrary")),
        name="gdn",
    )(z3, z3, z3, z3, z3, conv_w, alog_row, dt_row, norm_w.reshape(1, HC_DV), conv0, s0)


def _foxprep_kernel(q_ref, k_ref, v_ref, fb_ref, gq_ref, gk_ref, bf_ref, bd_ref,
                    qo_ref, ko_ref, kbo_ref, vbo_ref, lfo_ref, *rest, cumsum):
    bd = bd_ref[...]

    def head_rms(x, g):
        x2 = x * x
        hi = x2.astype(BF16)
        lo = (x2 - hi.astype(F32)).astype(BF16)
        ss = _dot(hi, bd) + _dot(lo, bd)
        return x * lax.rsqrt(ss * (1.0 / HB_D) + EPS) * g

    qn = head_rms(q_ref[...], gq_ref[...])
    kn = head_rms(k_ref[...], gk_ref[...])
    qo_ref[...] = (qn * (HB_D ** -0.5)).astype(BF16)
    ko_ref[...] = kn
    kbo_ref[...] = kn.astype(BF16)
    vbo_ref[...] = v_ref[...].astype(BF16)
    lf = _log_sigmoid(fb_ref[...] + bf_ref[...])
    lf = jnp.where(_iota(lf.shape, 1) < HB_H, lf, 0.0)
    lfo_ref[...] = lf
    if cumsum:
        ft_ref, carry = rest
        tm = lf.shape[0]

        @pl.when(pl.program_id(1) == 0)
        def _():
            carry[...] = jnp.zeros_like(carry)

        tri = (_iota((tm, tm), 1) <= _iota((tm, tm), 0)).astype(F32)
        f_cum = _dot_hi(tri, lf) + carry[...]
        carry[...] = f_cum[tm - 1:tm]
        ft_ref[...] = f_cum.T[:HB_H]


def _fox_prep(z3, gq, gk, b_f, *, cumsum):
    bg, lg, _ = z3.shape
    tm = _tile(lg, 256)
    gq_row = jnp.tile(gq, HB_H).reshape(1, B_W)
    gk_row = jnp.tile(gk, HB_H).reshape(1, B_W)
    bf_row = jnp.concatenate([b_f, jnp.zeros((LANES - HB_H,), F32)]).reshape(1, LANES)
    seg = jnp.arange(B_W) // HB_D
    bd = (seg[:, None] == seg[None, :]).astype(BF16)

    def wide(off):
        return pl.BlockSpec((None, tm, B_W), lambda b, i: (b, i, off // B_W))

    tok = pl.BlockSpec((None, tm, B_W), lambda b, i: (b, i, 0))
    out_shape = [jax.ShapeDtypeStruct((bg, lg, B_W), BF16), jax.ShapeDtypeStruct((bg, lg, B_W), F32),
                 jax.ShapeDtypeStruct((bg, lg, B_W), BF16), jax.ShapeDtypeStruct((bg, lg, B_W), BF16),
                 jax.ShapeDtypeStruct((bg, lg, LANES), F32)]
    out_specs = [tok, tok, tok, tok, pl.BlockSpec((None, tm, LANES), lambda b, i: (b, i, 0))]
    scratch = []
    if cumsum:
        out_shape.append(jax.ShapeDtypeStruct((bg, HB_H, lg), F32))
        out_specs.append(pl.BlockSpec((None, HB_H, tm), lambda b, i: (b, 0, i)))
        scratch.append(pltpu.VMEM((1, LANES), F32))
    return pl.pallas_call(
        functools.partial(_foxprep_kernel, cumsum=cumsum),
        out_shape=tuple(out_shape),
        grid=(bg, lg // tm),
        in_specs=[wide(OFF_QB), wide(OFF_KB), wide(OFF_VB),
                  pl.BlockSpec((None, tm, LANES), lambda b, i: (b, i, OFF_FB // LANES)),
                  pl.BlockSpec((1, B_W), lambda b, i: (0, 0)),
                  pl.BlockSpec((1, B_W), lambda b, i: (0, 0)),
                  pl.BlockSpec((1, LANES), lambda b, i: (0, 0)),
                  pl.BlockSpec((B_W, B_W), lambda b, i: (0, 0))],
        out_specs=tuple(out_specs),
        scratch_shapes=scratch,
        compiler_params=_params(("parallel", "arbitrary")),
        name="fox_prep",
    )(z3, z3, z3, z3, gq_row, gk_row, bf_row, bd)


def _foxattn_kernel(q_ref, k_ref, v_ref, f_ref, o_ref, *, tq):
    qi = pl.program_id(2)
    q = q_ref[...]
    lane_q = _iota(q.shape, 1)
    q_heads = (jnp.where(lane_q < HB_D, q, jnp.zeros_like(q)), jnp.where(lane_q >= HB_D, q, jnp.zeros_like(q)))
    row = _iota((tq, tq), 0)
    colm = _iota((tq, tq), 1)

    def step(j, carry, masked):
        r = pl.multiple_of(j * tq, tq)
        kj = k_ref[pl.ds(r, tq), :]
        vj = v_ref[pl.ds(r, tq), :]
        fj = f_ref[j]
        out = []
        for hh in range(2):
            m, l, acc = carry[hh]
            s = _dot_nt(q_heads[hh], kj) - fj[hh:hh + 1, :]
            if masked:
                s = jnp.where(colm <= row, s, NEG_INF)
            m_new = jnp.maximum(m, jnp.max(s, axis=-1, keepdims=True))
            alpha = jnp.exp(m - m_new)
            p = jnp.exp(s - m_new)
            l = alpha * l + jnp.sum(p, axis=-1, keepdims=True)
            acc = alpha * acc + _dot(p.astype(BF16), vj)
            out.append((m_new, l, acc))
        return tuple(out)

    init = tuple((jnp.full((tq, 1), -1e30, F32), jnp.zeros((tq, 1), F32), jnp.zeros((tq, LANES), F32))
                 for _ in range(2))
    carry = lax.fori_loop(0, qi, lambda j, c: step(j, c, False), init)
    (_, l0, a0), (_, l1, a1) = step(qi, carry, True)
    o = jnp.where(_iota((tq, LANES), 1) < HB_D, a0 / l0, a1 / l1)
    o_ref[...] = o.astype(o_ref.dtype)


def _fox_attn_prompt(qb, kb, vb, ft):
    bg, s_len, _ = qb.shape
    tq = _tile(s_len, 512)
    nk = s_len // tq
    pairs = HB_H // 2
    f5 = ft.reshape(bg, pairs, 2, nk, tq).transpose(0, 1, 3, 2, 4)
    return pl.pallas_call(
        functools.partial(_foxattn_kernel, tq=tq),
        out_shape=jax.ShapeDtypeStruct((bg, s_len, B_W), BF16),
        grid=(bg, pairs, nk),
        in_specs=[pl.BlockSpec((None, tq, LANES), lambda b, p, i: (b, i, p)),
                  pl.BlockSpec((None, s_len, LANES), lambda b, p, i: (b, 0, p)),
                  pl.BlockSpec((None, s_len, LANES), lambda b, p, i: (b, 0, p)),
                  pl.BlockSpec((None, None, nk, 2, tq), lambda b, p, i: (b, p, 0, 0, 0))],
        out_specs=pl.BlockSpec((None, tq, LANES), lambda b, p, i: (b, i, p)),
        compiler_params=_params(("parallel", "parallel", "arbitrary")),
        name="fox_attn_prompt",
    )(qb, kb, vb, f5)


def _foxsample_kernel(pt_ref, q_ref, kn_ref, vn_ref, lfn_ref, *rest, n_new, ps):
    k_pages = rest[0:ps]
    v_pages = rest[ps:2 * ps]
    f_pages = rest[2 * ps:3 * ps]
    o_ref, q_scr, m_scr, l_scr, acc_scr, fc_scr = rest[3 * ps:]
    g = pl.program_id(1)
    rows = n_new * HB_H
    hmask = (_iota((HB_H, B_W), 1) // HB_D) == _iota((HB_H, B_W), 0)

    @pl.when(g == 0)
    def _():
        for t in range(n_new):
            qt = jnp.broadcast_to(q_ref[t:t + 1, :], (HB_H, B_W))
            q_scr[t * HB_H:(t + 1) * HB_H, :] = jnp.where(hmask, qt, 0.0)
        m_scr[...] = jnp.full_like(m_scr, -1e30)
        l_scr[...] = jnp.zeros_like(l_scr)
        acc_scr[...] = jnp.zeros_like(acc_scr)
        fc_scr[...] = jnp.zeros_like(fc_scr)

    tri_u = (_iota((PAGE, PAGE), 0) <= _iota((PAGE, PAGE), 1)).astype(F32)

    def update(s, v_b):
        m = m_scr[...]
        m_new = jnp.maximum(m, jnp.max(s, axis=-1, keepdims=True))
        alpha = jnp.exp(m - m_new)
        p = jnp.exp(s - m_new)
        l_scr[...] = alpha * l_scr[...] + jnp.sum(p, axis=-1, keepdims=True)
        acc_scr[...] = alpha * acc_scr[...] + _dot_nt(p.astype(BF16), v_b)
        m_scr[...] = m_new

    def cum_forget(lf_parts):
        f_loc = [_dot_hi(lf_t, tri_u) for lf_t in lf_parts]
        off = fc_scr[...]
        out = []
        for fl in f_loc:
            out.append(fl + off)
            off = off + fl[:, PAGE - 1:PAGE]
        fc_scr[...] = off
        return out

    f_cat = jnp.concatenate(cum_forget([f_pages[r][...] for r in range(ps)]), axis=1)
    bias = jnp.concatenate([f_cat] * n_new, axis=0)
    k_cat = jnp.concatenate([k_pages[r][...].astype(BF16) for r in range(ps)], axis=1)
    v_cat = jnp.concatenate([v_pages[r][...].astype(BF16) for r in range(ps)], axis=1)
    q_rows = q_scr[...].astype(BF16)
    update(_dot(q_rows, k_cat) - bias, v_cat)

    @pl.when(g == pl.num_programs(1) - 1)
    def _():
        f_new = cum_forget([lfn_ref[...]])[0]
        s = _dot(q_rows, kn_ref[...].astype(BF16)) - jnp.concatenate([f_new] * n_new, axis=0)
        visible = _iota((rows, PAGE), 1) <= (_iota((rows, PAGE), 0) // HB_H)
        update(jnp.where(visible, s, NEG_INF), vn_ref[...].astype(BF16))
        o = acc_scr[...] / l_scr[...]
        out = jnp.zeros((SUBLANES, B_W), F32)
        out_row = _iota((SUBLANES, B_W), 0)
        for t in range(n_new):
            ot = jnp.where(hmask, o[t * HB_H:(t + 1) * HB_H, :], 0.0)
            out = jnp.where(out_row == t, jnp.sum(ot, axis=0, keepdims=True), out)
        o_ref[...] = out.astype(o_ref.dtype)


def _fox_attn_sample(layer, q_new, k_new, v_new, lf_new, cache_kt, cache_vt, cache_lf_t, page_table):
    db, n_new, _ = q_new.shape
    n_pages = page_table.shape[1]
    n_pool = cache_kt.shape[0] // DEPTH
    ps = min(PAGES_PER_STEP, n_pages)
    assert n_pages % ps == 0 and n_new <= SUBLANES
    qp = jnp.pad(q_new, ((0, 0), (0, SUBLANES - n_new), (0, 0)))
    pad_pos = ((0, 0), (0, 0), (0, PAGE - n_new))
    knp = jnp.pad(k_new.transpose(0, 2, 1), pad_pos)
    vnp = jnp.pad(v_new.transpose(0, 2, 1), pad_pos)
    lfp = jnp.pad(lf_new.transpose(0, 2, 1), pad_pos)
    base = layer * n_pool

    def page_spec(r, rows, cols):
        return pl.BlockSpec((None, rows, cols), lambda b, g, pt: (base + pt[b, g * ps + r], 0, 0))

    in_specs = [pl.BlockSpec((None, SUBLANES, B_W), lambda b, g, pt: (b, 0, 0)),
                pl.BlockSpec((None, B_W, PAGE), lambda b, g, pt: (b, 0, 0)),
                pl.BlockSpec((None, B_W, PAGE), lambda b, g, pt: (b, 0, 0)),
                pl.BlockSpec((None, HB_H, PAGE), lambda b, g, pt: (b, 0, 0))]
    in_specs += [page_spec(r, B_W, PAGE) for r in range(ps)]
    in_specs += [page_spec(r, B_W, PAGE) for r in range(ps)]
    in_specs += [page_spec(r, HB_H, PAGE) for r in range(ps)]
    rows = n_new * HB_H
    return pl.pallas_call(
        functools.partial(_foxsample_kernel, n_new=n_new, ps=ps),
        out_shape=jax.ShapeDtypeStruct((db, SUBLANES, B_W), BF16),
        grid_spec=pltpu.PrefetchScalarGridSpec(
            num_scalar_prefetch=1,
            grid=(db, n_pages // ps),
            in_specs=in_specs,
            out_specs=pl.BlockSpec((None, SUBLANES, B_W), lambda b, g, pt: (b, 0, 0)),
            scratch_shapes=[pltpu.VMEM((rows, B_W), F32), pltpu.VMEM((rows, 1), F32),
                            pltpu.VMEM((rows, 1), F32), pltpu.VMEM((rows, B_W), F32),
                            pltpu.VMEM((HB_H, 1), F32)]),
        compiler_params=_params(("parallel", "arbitrary")),
        name="fox_attn_sample",
    )(page_table, qp, knp, vnp, lfp, *([cache_kt] * ps), *([cache_vt] * ps), *([cache_lf_t] * ps))


def _merge_kernel(oa_ref, ob_ref, oc_ref, ga_ref, gb_ref, gc_ref, x_ref, gt1_ref, sh2_ref, sc2_ref,
                  wa_ref, wb_ref, wc_ref, wo_ref, nf_ref, wq_ref, keys_ref,
                  x1_ref, h2t_ref, st_ref):
    merged = (_sigmoid(ga_ref[...]) * _dot(oa_ref[...], wa_ref[...])
              + _sigmoid(gb_ref[...]) * _dot(ob_ref[...], wb_ref[...])
              + _sigmoid(gc_ref[...]) * _dot(oc_ref[...], wc_ref[...]))
    x1 = x_ref[...] + gt1_ref[...] * _dot(merged.astype(BF16), wo_ref[...])
    x1_ref[...] = x1
    r = lax.rsqrt(jnp.mean(x1 * x1, axis=-1, keepdims=True) + EPS)
    h2 = (x1 * r) * nf_ref[...] * (1.0 + sc2_ref[...]) + sh2_ref[...]
    h2t_ref[...] = h2.T.astype(BF16)
    qb = _dot(h2.astype(BF16), wq_ref[...]).astype(BF16)
    half = P_QD // 2
    for h in range(P_H):
        for p in range(2):
            lo = (h * 2 + p) * half
            st_ref[lo:lo + half, :] = _dot_nt(keys_ref[p], qb[:, lo:lo + half])


def _merge(oa, ob, oc, z3, x, mod, w_a, w_b, w_c, w_o, norm_ffn, w_q, keys):
    bg, lg, _ = x.shape
    tm = _tile(lg, 256)
    nl = lg // tm
    t_all = bg * lg

    def tok(width):
        return pl.BlockSpec((None, tm, width), lambda b, i: (b, i, 0))

    def gate(k):
        return pl.BlockSpec((None, tm, D), lambda b, i: (b, i, OFF_GATE // D + k))

    def full(shape):
        return pl.BlockSpec(shape, lambda b, i: (0,) * len(shape))

    return pl.pallas_call(
        _merge_kernel,
        out_shape=(jax.ShapeDtypeStruct((bg, lg, D), F32),
                   jax.ShapeDtypeStruct((D, t_all), BF16),
                   jax.ShapeDtypeStruct((P_H * P_QD, t_all), F32)),
        grid=(bg, nl),
        in_specs=[tok(A_W), tok(B_W), tok(C_K), gate(0), gate(1), gate(2), tok(D),
                  _mod_spec(mod, tm, 2), _mod_spec(mod, tm, 3), _mod_spec(mod, tm, 4),
                  full((A_W, D)), full((B_W, D)), full((C_K, D)), full((D, D)), full((1, D)),
                  full((D, P_H * P_QD)), full((2, P_NK, P_QD // 2))],
        out_specs=(tok(D),
                   pl.BlockSpec((D, tm), lambda b, i: (0, b * nl + i)),
                   pl.BlockSpec((P_H * P_QD, tm), lambda b, i: (0, b * nl + i))),
        compiler_params=_params(("parallel", "parallel")),
        name="merge",
    )(oa, ob, oc, z3, z3, z3, x, mod, mod, mod, w_a, w_b, w_c, w_o, norm_ffn.reshape(1, D), w_q, keys)


def _topk_kernel(st_ref, o_ref):
    tt = st_ref.shape[1]

    def top_rows(s_ref, lo):
        n_v = P_NK // SUBLANES
        v = [s_ref[lo + k * SUBLANES:lo + (k + 1) * SUBLANES, :] for k in range(n_v)]
        k = 2
        while k <= n_v:
            j = k // 2
            while j >= 1:
                for a in range(n_v):
                    b = a ^ j
                    if b > a:
                        hi, lo_v = jnp.maximum(v[a], v[b]), jnp.minimum(v[a], v[b])
                        v[a], v[b] = (hi, lo_v) if (a & k) == 0 else (lo_v, hi)
                j //= 2
            k *= 2
        rows = []
        for it in range(P_TOPK + 1):
            m = jnp.max(v[0], axis=0, keepdims=True)
            rows.append(m)
            popped = v[0] == m
            for a in range(P_TOPK - it):
                v[a] = jnp.where(popped, v[a + 1] if a + 1 < n_v else NEG_INF, v[a])
        return rows

    rank = _iota((P_TOPK, tt), 0)
    rank8 = _iota((SUBLANES, tt), 0)
    thr_rows, nrm_rows = [], []
    for h in range(P_H):
        lo = h * P_QD
        v1 = top_rows(st_ref, lo)
        v2 = top_rows(st_ref, lo + P_NK)
        v1s = jnp.concatenate(v1[:P_TOPK], axis=0)
        v2s = jnp.concatenate(v2[:P_TOPK], axis=0)
        groups = [jnp.where(rank8 == 0, v1[P_TOPK] + v2[0], jnp.where(rank8 == 1, v1[0] + v2[P_TOPK], NEG_INF))]
        for b in range(3):
            groups.append(jnp.where(rank < P_TOPK // (b + 1), v1s + v2[b], NEG_INF))
        for a in range(4):
            nb = P_TOPK // (a + 1)
            n_rows = P_TOPK if nb > SUBLANES else SUBLANES
            rk = _iota((n_rows, tt), 0)
            ok = jnp.where(rk >= 3, rk, nb) < nb
            groups.append(jnp.where(ok, v2s[:n_rows] + v1[a], NEG_INF))
        m_top = None
        z = None
        for it in range(P_TOPK):
            m = functools.reduce(jnp.maximum, [jnp.max(gp, axis=0, keepdims=True) for gp in groups])
            if it == 0:
                m_top = m
                z = jnp.ones_like(m)
            else:
                z = z + jnp.exp(m - m_top)
            groups = [jnp.where(gp == m, NEG_INF, gp) for gp in groups]
        m_next = functools.reduce(jnp.maximum, [jnp.max(gp, axis=0, keepdims=True) for gp in groups])
        thr_rows.append(0.5 * m + 0.5 * m_next)
        nrm_rows.append(-(m_top + jnp.log(z)))
    o_ref[...] = jnp.concatenate(thr_rows + nrm_rows, axis=0)


def _topk(st):
    t_all = st.shape[1]
    tt = _tile(t_all, 256)
    return pl.pallas_call(
        _topk_kernel,
        out_shape=jax.ShapeDtypeStruct((2 * P_H, t_all), F32),
        grid=(t_all // tt,),
        in_specs=[pl.BlockSpec((P_H * P_QD, tt), lambda t: (0, t))],
        out_specs=pl.BlockSpec((2 * P_H, tt), lambda t: (0, t)),
        compiler_params=_params(("parallel",)),
        name="peer_topk",
    )(st)


def _peer_kernel(st_ref, stat_ref, h2t_ref, u0_ref, u1_ref, vt0_ref, vt1_ref, x1_ref, gt2_ref, o_ref,
                 acc_scr, e1_scr, tau_scr, e2_scr, s2_scr, ht0_scr, ht1_scr, wa0_scr, wa1_scr, *, ti, n_tiles):
    s = pl.program_id(2)
    tt = st_ref.shape[1]

    @pl.when(s == 0)
    def _():
        acc_scr[...] = jnp.zeros_like(acc_scr)
        ht0_scr[...] = jnp.zeros_like(ht0_scr)
        ht1_scr[...] = jnp.zeros_like(ht1_scr)
        wa0_scr[...] = jnp.zeros_like(wa0_scr)
        wa1_scr[...] = jnp.zeros_like(wa1_scr)
        for h in range(P_H):
            lo = h * P_QD
            hr = slice(h * P_NK, (h + 1) * P_NK)
            for lt in range(tt // LANES):
                ls = slice(lt * LANES, (lt + 1) * LANES)
                s1 = st_ref[lo:lo + P_NK, ls]
                s2 = st_ref[lo + P_NK:lo + 2 * P_NK, ls]
                mx2 = jnp.max(s2, axis=0, keepdims=True)
                s2_scr[lt, hr, :] = s2
                e2_scr[lt, hr, :] = jnp.exp(s2 - mx2)
                e1_scr[hr, ls] = 0.5 * jnp.exp(s1 + (stat_ref[P_H + h:P_H + h + 1, ls] + mx2))
                tau_scr[hr, ls] = stat_ref[h:h + 1, ls] - s1

    c0 = math.sqrt(2.0 / math.pi)
    c1 = c0 * 0.044715
    tile_b = jnp.clip(s - 1, 0, n_tiles - 1)

    def stages(ht_w, ht_r, wa_w, wa_r):
        te = ti * P_NK
        n_lt = tt // LANES
        per_group = 2 if n_lt % 2 == 0 else 1
        n_p = ti * (n_lt // per_group)
        ka, kc = n_p // 2, 2
        ma, mc = te // 2, D // (n_p // 2)
        wka, wkc = D // ka, te // kc

        def matmul_pieces(pi):
            mh, kq = pi // ka, pi % ka
            u_half = (u0_ref, u1_ref)[mh]
            part_a = _dot(u_half[:, kq * wka:(kq + 1) * wka], h2t_ref[kq * wka:(kq + 1) * wka, :])
            mq, kh = pi // kc, pi % kc
            vt_half = (vt0_ref, vt1_ref)[mq * mc // (D // 2)]
            v0 = (mq * mc) % (D // 2)
            wa_k = jnp.concatenate([wa_r[lt, kh * wkc:(kh + 1) * wkc, :] for lt in range(n_lt)], axis=1)
            part_c = _dot(vt_half[v0:v0 + mc, kh * wkc:(kh + 1) * wkc], wa_k)
            for lt in range(n_lt):
                ls = slice(lt * LANES, (lt + 1) * LANES)
                if kq == 0:
                    ht_w[lt, mh * ma:(mh + 1) * ma, :] = part_a[:, ls]
                else:
                    ht_w[lt, mh * ma:(mh + 1) * ma, :] += part_a[:, ls]
                acc_scr[lt, mq * mc:(mq + 1) * mc, :] += part_c[:, ls]

        for ii in range(ti):
            i_row = tile_b * ti + ii
            tau_rows = [tau_scr[pl.ds(h * P_NK + i_row, 1), :] for h in range(P_H)]
            e1_rows = [e1_scr[pl.ds(h * P_NK + i_row, 1), :] for h in range(P_H)]
            for lt in range(n_lt):
                if lt % per_group == 0:
                    matmul_pieces(ii * (n_lt // per_group) + lt // per_group)
                ls = slice(lt * LANES, (lt + 1) * LANES)
                tau_b = [jnp.broadcast_to(tau_rows[h][:, ls], (PEER_JB, LANES)) for h in range(P_H)]
                e1_b = [jnp.broadcast_to(e1_rows[h][:, ls], (PEER_JB, LANES)) for h in range(P_H)]
                for jb in range(P_NK // PEER_JB):
                    j0 = jb * PEER_JB
                    hs = ht_r[lt, ii * P_NK + j0:ii * P_NK + j0 + PEER_JB, :]
                    act = hs * (1.0 + jnp.tanh(hs * (c0 + c1 * (hs * hs))))
                    w = None
                    for h in range(P_H):
                        jr = slice(h * P_NK + j0, h * P_NK + j0 + PEER_JB)
                        wh = jnp.where(s2_scr[lt, jr, :] >= tau_b[h], e2_scr[lt, jr, :] * e1_b[h], 0.0)
                        w = wh if w is None else w + wh
                    wa_w[lt, ii * P_NK + j0:ii * P_NK + j0 + PEER_JB, :] = (w * act).astype(BF16)

    @pl.when(s % 2 == 0)
    def _():
        stages(ht0_scr, ht1_scr, wa1_scr, wa0_scr)

    @pl.when(s % 2 == 1)
    def _():
        stages(ht1_scr, ht0_scr, wa0_scr, wa1_scr)

    @pl.when(s == pl.num_programs(2) - 1)
    def _():
        for lt in range(tt // LANES):
            rows = slice(lt * LANES, (lt + 1) * LANES)
            gt2 = gt2_ref[rows, :] if gt2_ref.shape[0] == tt else gt2_ref[0:1, :]
            o_ref[rows, :] = x1_ref[rows, :] + gt2 * acc_scr[lt].T


def _peer(st, stats, h2t, u_b, vt_b, x1, mod):
    bg, lg, _ = x1.shape
    tt = _tile(lg, 512)
    nl = lg // tt
    ti = PEER_TI
    te = ti * P_NK
    assert vt_b.shape == (N_EXP // te, D, te)
    n_tiles = N_EXP // te
    if mod.shape[1] == 1:
        mod = jnp.broadcast_to(mod, (bg, SUBLANES, 6 * D))
        gate_spec = pl.BlockSpec((None, SUBLANES, D), lambda b, i, s: (b, 0, 5))
    else:
        gate_spec = _mod_spec(mod, tt, 5)
    return pl.pallas_call(
        functools.partial(_peer_kernel, ti=ti, n_tiles=n_tiles),
        out_shape=jax.ShapeDtypeStruct((bg, lg, D), F32),
        grid=(bg, nl, n_tiles + 2),
        in_specs=[pl.BlockSpec((P_H * P_QD, tt), lambda b, i, s: (0, b * nl + i)),
                  pl.BlockSpec((2 * P_H, tt), lambda b, i, s: (0, b * nl + i)),
                  pl.BlockSpec((D, tt), lambda b, i, s: (0, b * nl + i)),
                  pl.BlockSpec((te // 2, D), lambda b, i, s: (2 * jnp.minimum(s, n_tiles - 1), 0)),
                  pl.BlockSpec((te // 2, D), lambda b, i, s: (2 * jnp.minimum(s, n_tiles - 1) + 1, 0)),
                  pl.BlockSpec((None, D // 2, te), lambda b, i, s: (jnp.clip(s - 2, 0, n_tiles - 1), 0, 0)),
                  pl.BlockSpec((None, D // 2, te), lambda b, i, s: (jnp.clip(s - 2, 0, n_tiles - 1), 1, 0)),
                  pl.BlockSpec((None, tt, D), lambda b, i, s: (b, i, 0)),
                  gate_spec],
        out_specs=pl.BlockSpec((None, tt, D), lambda b, i, s: (b, i, 0)),
        scratch_shapes=[pltpu.VMEM((tt // LANES, D, LANES), F32),
                        pltpu.VMEM((P_H * P_NK, tt), F32), pltpu.VMEM((P_H * P_NK, tt), F32),
                        pltpu.VMEM((tt // LANES, P_H * P_NK, LANES), F32),
                        pltpu.VMEM((tt // LANES, P_H * P_NK, LANES), F32),
                        pltpu.VMEM((tt // LANES, te, LANES), F32), pltpu.VMEM((tt // LANES, te, LANES), F32),
                        pltpu.VMEM((tt // LANES, te, LANES), BF16), pltpu.VMEM((tt // LANES, te, LANES), BF16)],
        compiler_params=_params(("parallel", "parallel", "arbitrary")),
        name="peer_experts",
    )(st, stats, h2t, u_b, u_b, vt_b, vt_b, x1, mod)


def _pack_w_in(w):
    pad = jnp.zeros((D, LANES - 8), w.dtype)
    o_fb = 2 * A_W + 2 * A_W + 3 * B_W
    o_c = o_fb + HB_H
    o_ab = o_c + 3 * C_K
    o_zc = o_ab + 2 * HC_H
    o_gate = o_zc + C_K
    packed = jnp.concatenate([
        w[:, 0:4 * A_W], w[:, o_gate:o_gate + 3 * D], w[:, 4 * A_W:o_fb], w[:, o_c:o_ab],
        w[:, o_zc:o_gate], w[:, o_fb:o_c], pad, w[:, o_ab:o_zc], pad], axis=1)
    assert packed.shape[1] == NP
    return packed.astype(BF16)


def _layer(l, x, mod, w, sample):
    bg, lg, _ = x.shape
    z = _in_proj(x, mod, w["norm_mix"][l], w["w_in"][l])
    if sample is None:
        z3 = z
        bs, l_seq = bg, lg
        chunk = LIN_C
        hgrn0 = jnp.zeros((bs, HA_H, HA_DK, HA_DV), F32)
        gdn0 = jnp.zeros((bs, HC_H, HC_DK, HC_DV), F32)
        conv0 = jnp.zeros((bs, SUBLANES, CONV_CH), F32)
    else:
        bs, l_seq = sample["db"], sample["t"]
        chunk = SUBLANES
        z3 = jnp.pad(z.reshape(bs, l_seq, NP), ((0, 0), (0, SUBLANES - l_seq), (0, 0)))
        hgrn0 = sample["state_hgrn"][l]
        gdn0 = sample["state_gdn"][l]
        conv0 = jnp.pad(sample["state_conv"][l], ((0, 0), (SUBLANES - (CONV_W - 1), 0), (0, 0)))

    o_a, s_hgrn = _hgrn(z3, w["hgrn_lb_logits"], w["hgrn_norm"][l], hgrn0, layer=l, chunk=chunk, l_valid=l_seq)
    o_c, s_gdn = _gdn(z3, w["gdn_conv_w"][l], w["gdn_a_log"][l], w["gdn_dt_bias"][l], w["gdn_norm"][l],
                      conv0, gdn0, chunk=chunk, l_valid=l_seq)
    zseq = z.reshape(bs, l_seq, NP)
    conv_new = zseq[:, l_seq - (CONV_W - 1):, OFF_QC:OFF_QC + CONV_CH]

    prep = _fox_prep(z, w["fox_q_norm"][l], w["fox_k_norm"][l], w["fox_b_f"][l], cumsum=sample is None)
    qn, kn, knb, vnb, lf = prep[:5]
    v_b = z[:, :, OFF_VB:OFF_VB + B_W]
    if sample is None:
        o_b = _fox_attn_prompt(qn, knb, vnb, prep[5])
    else:
        o_b = _fox_attn_sample(
            l, qn.astype(F32).reshape(bs, l_seq, B_W), kn.reshape(bs, l_seq, B_W), v_b.reshape(bs, l_seq, B_W),
            lf[0, :, :HB_H].reshape(bs, l_seq, HB_H), sample["cache_kt"], sample["cache_vt"], sample["cache_lf_t"],
            sample["page_table"])
        o_b = o_b[:, :l_seq].reshape(bg, lg, B_W)
        o_a = o_a[:, :l_seq].reshape(bg, lg, A_W)
        o_c = o_c[:, :l_seq].reshape(bg, lg, C_K)

    x1, h2t, st = _merge(o_a, o_b, o_c, z, x, mod, w["w_br_a"][l], w["w_br_b"][l], w["w_br_c"][l],
                         w["w_out"][l], w["norm_ffn"][l], w["peer_w_q"][l], w["peer_keys"][l])
    stats = _topk(st)
    x2 = _peer(st, stats, h2t, w["peer_u"][l], w["peer_vt"][l], x1, mod)

    k_leaf = kn.reshape(bs, l_seq, HB_H, HB_D)
    v_leaf = v_b.reshape(bs, l_seq, HB_H, HB_D)
    lf_leaf = lf[:, :, :HB_H].reshape(bs, l_seq, HB_H)
    return x2, (k_leaf, v_leaf, lf_leaf, s_hgrn, s_gdn, conv_new)


def _trunk(x, mods, w, sample):
    leaves = [[] for _ in range(6)]
    for l in range(DEPTH):
        x, st = _layer(l, x, mods[l], w, sample)
        for lst, s in zip(leaves, st):
            lst.append(s)
    return x, [jnp.stack(v) for v in leaves]


def kernel(x_prompt, x_sample, c_prompt, c_sample, cache_fox_k, cache_fox_v, cache_fox_logf, page_table,
           state_hgrn, state_gdn, state_gdn_conv, w_ada, b_ada, norm_mix, norm_ffn, w_in, hgrn_lb_logits,
           hgrn_norm, fox_b_f, fox_q_norm, fox_k_norm, gdn_conv_w, gdn_a_log, gdn_dt_bias, gdn_norm,
           w_br_a, w_br_b, w_br_c, w_out, peer_w_q, peer_keys, peer_u, peer_v):
    bp = x_prompt.shape[0]
    db, t_new, _ = x_sample.shape
    n_pool = cache_fox_k.shape[1]

    w = {
        "norm_mix": norm_mix, "norm_ffn": norm_ffn, "hgrn_lb_logits": hgrn_lb_logits, "hgrn_norm": hgrn_norm,
        "fox_b_f": fox_b_f, "fox_q_norm": fox_q_norm, "fox_k_norm": fox_k_norm, "gdn_conv_w": gdn_conv_w,
        "gdn_a_log": gdn_a_log, "gdn_dt_bias": gdn_dt_bias, "gdn_norm": gdn_norm,
        "w_in": [_pack_w_in(w_in[l]) for l in range(DEPTH)],
        "w_br_a": w_br_a.astype(BF16), "w_br_b": w_br_b.astype(BF16), "w_br_c": w_br_c.astype(BF16),
        "w_out": w_out.astype(BF16), "peer_w_q": peer_w_q.astype(BF16), "peer_keys": peer_keys.astype(BF16),
        "peer_u": peer_u.astype(BF16),
        "peer_vt": peer_v.astype(BF16).reshape(DEPTH, N_EXP // (PEER_TI * P_NK), PEER_TI * P_NK, D).transpose(0, 1, 3, 2),
    }

    n_c = bp + db
    c_all = jnp.pad(jnp.concatenate([c_prompt, c_sample], axis=0), ((0, (-n_c) % SUBLANES), (0, 0)))
    mod = _ada(c_all, w_ada, b_ada)
    mods_p = [mod[l, :bp].reshape(bp, 1, 6 * D) for l in range(DEPTH)]
    mods_s = [jnp.repeat(mod[l, bp:n_c], t_new, axis=0).reshape(1, db * t_new, 6 * D) for l in range(DEPTH)]

    sample = {
        "db": db, "t": t_new, "page_table": page_table,
        "state_hgrn": state_hgrn, "state_gdn": state_gdn, "state_conv": state_gdn_conv,
        "cache_kt": cache_fox_k.transpose(0, 1, 3, 4, 2).reshape(DEPTH * n_pool, B_W, PAGE),
        "cache_vt": cache_fox_v.transpose(0, 1, 3, 4, 2).reshape(DEPTH * n_pool, B_W, PAGE),
        "cache_lf_t": cache_fox_logf.transpose(0, 1, 3, 2).reshape(DEPTH * n_pool, HB_H, PAGE),
    }

    y_p, leaves_p = _trunk(x_prompt, mods_p, w, None)
    y_s, leaves_s = _trunk(x_sample.reshape(1, db * t_new, D), mods_s, w, sample)
    return (y_p, y_s.reshape(db, t_new, D), *leaves_p, *leaves_s)
```

```python
import functools
import math

import jax
import jax.numpy as jnp
from jax import lax
from jax.experimental import pallas as pl
from jax.experimental.pallas import tpu as pltpu

F32 = jnp.float32
BF16 = jnp.bfloat16
HI = lax.Precision.HIGHEST
NEG_INF = float("-inf")

D = 1024
DEPTH = 2
HA_H, HA_DK, HA_DV = 4, 128, 128
HB_H, HB_D = 8, 64
HC_H, HC_DK, HC_DV = 4, 128, 128
CONV_W = 4
P_H, P_TOPK, P_NK, P_QD = 8, 16, 128, 256
N_EXP = P_NK * P_NK
EPS = 1e-6
A_W = HA_H * HA_DK
B_W = HB_H * HB_D
C_K = HC_H * HC_DK
CONV_CH = 3 * C_K
PAGE = 128

LANES = 128
SUBLANES = 8
VMEM_LIMIT = 52 * 1024 * 1024

OFF_FA, OFF_QA, OFF_IA, OFF_GA = 0, 512, 1024, 1536
OFF_GATE = 2048
OFF_QB, OFF_KB, OFF_VB = 5120, 5632, 6144
OFF_QC, OFF_KC, OFF_VC = 6656, 7168, 7680
OFF_ZC = 8192
OFF_FB = 8704
OFF_AB = 8832
NP = 8960

LIN_C = 64
GDN_C = 64
HGRN_SUB = 16
HGRN_HEADS_PER_STEP = 2
PEER_JB = 32
PEER_TI = 4
PEER_PIECES = 8
PEER_KW = 256
PAGES_PER_STEP = 16


def _dot(a, b):
    return jnp.dot(a, b, preferred_element_type=F32)


def _dot_hi(a, b):
    return jnp.dot(a, b, preferred_element_type=F32, precision=HI)


def _dot_nt(a, b):
    return lax.dot_general(a, b, (((1,), (1,)), ((), ())), preferred_element_type=F32)


def _dot_tn(a, b, precision=None):
    return lax.dot_general(a, b, (((0,), (0,)), ((), ())), preferred_element_type=F32, precision=precision)


def _iota(shape, dim):
    return lax.broadcasted_iota(jnp.int32, shape, dim)


def _sigmoid(x):
    return jax.nn.sigmoid(x)


def _silu(x):
    return x * jax.nn.sigmoid(x)


def _softplus(x):
    return jnp.maximum(x, 0.0) + jnp.log1p(jnp.exp(-jnp.abs(x)))


def _log_sigmoid(x):
    return jnp.minimum(x, 0.0) - jnp.log1p(jnp.exp(-jnp.abs(x)))


def _params(sem):
    return pltpu.CompilerParams(dimension_semantics=sem, vmem_limit_bytes=VMEM_LIMIT)


def _chunk_start(c, c_len):
    return c * c_len if isinstance(c, int) else pl.multiple_of(c * c_len, c_len)


def _for_chunks(n_chunks, body, unroll=False):
    if n_chunks == 1 or unroll:
        for c in range(n_chunks):
            body(c, 0)
    else:
        lax.fori_loop(0, n_chunks, body, 0)


def _tile(n, pref):
    t = min(n, pref)
    assert n % t == 0, (n, pref)
    return t


def _ada_kernel(c_ref, w_ref, b_ref, o_ref):
    sc = _silu(c_ref[...]).astype(BF16)
    o_ref[...] = _dot(sc, w_ref[...].astype(BF16)) + b_ref[...]


def _ada(c_all, w_ada, b_ada):
    rows = c_all.shape[0]
    tn = 768
    return pl.pallas_call(
        _ada_kernel,
        out_shape=jax.ShapeDtypeStruct((DEPTH, rows, 6 * D), F32),
        grid=(DEPTH, 6 * D // tn),
        in_specs=[pl.BlockSpec((rows, D), lambda l, n: (0, 0)),
                  pl.BlockSpec((None, D, tn), lambda l, n: (l, 0, n)),
                  pl.BlockSpec((None, 1, tn), lambda l, n: (l, 0, n))],
        out_specs=pl.BlockSpec((None, rows, tn), lambda l, n: (l, 0, n)),
        compiler_params=_params(("parallel", "parallel")),
        name="ada",
    )(c_all, w_ada, b_ada.reshape(DEPTH, 1, 6 * D))


def _mod_spec(mod, tm, k):
    if mod.shape[1] == 1:
        return pl.BlockSpec((None, 1, D), lambda b, i, *_: (b, 0, k))
    return pl.BlockSpec((None, tm, D), lambda b, i, *_: (b, i, k))


def _in_kernel(x_ref, sh_ref, sc_ref, g_ref, w_ref, z_ref, h_scr):
    @pl.when(pl.program_id(2) == 0)
    def _():
        x = x_ref[...]
        r = lax.rsqrt(jnp.mean(x * x, axis=-1, keepdims=True) + EPS)
        h = (x * r) * g_ref[...] * (1.0 + sc_ref[...]) + sh_ref[...]
        h_scr[...] = h.astype(BF16)

    z_ref[...] = _dot(h_scr[...], w_ref[...])


def _in_proj(x, mod, gain, w_packed):
    bg, lg, _ = x.shape
    tm = _tile(lg, 512)
    tn = 1280
    return pl.pallas_call(
        _in_kernel,
        out_shape=jax.ShapeDtypeStruct((bg, lg, NP), F32),
        grid=(bg, lg // tm, NP // tn),
        in_specs=[pl.BlockSpec((None, tm, D), lambda b, i, n: (b, i, 0)),
                  _mod_spec(mod, tm, 0), _mod_spec(mod, tm, 1),
                  pl.BlockSpec((1, D), lambda b, i, n: (0, 0)),
                  pl.BlockSpec((D, tn), lambda b, i, n: (0, n))],
        out_specs=pl.BlockSpec((None, tm, tn), lambda b, i, n: (b, i, n)),
        scratch_shapes=[pltpu.VMEM((tm, D), BF16)],
        compiler_params=_params(("parallel", "parallel", "arbitrary")),
        name="in_proj",
    )(x, mod, mod, gain.reshape(1, D), w_packed)


def _hgrn_kernel(fa_ref, qa_ref, ia_ref, ga_ref, lbl_ref, nw_ref, s0_ref, o_ref, sout_ref, s_scr,
                 *, layer, chunk, n_chunks, l_valid, l_padded):
    i = pl.program_id(2)
    c_len = chunk

    @pl.when(i == 0)
    def _():
        s_scr[...] = s0_ref[...]

    lg = lbl_ref[...]
    e = jnp.exp(lg - jnp.max(lg, axis=0, keepdims=True))
    p = e / jnp.sum(e, axis=0, keepdims=True)
    cs = p[0:1]
    for j in range(1, layer + 1):
        cs = cs + p[j:j + 1]
    lb = jnp.maximum(cs - p[0:1], 0.0)
    log_lb = jnp.log(lb)
    log1m_lb = jnp.log1p(-lb)

    sub = min(HGRN_SUB, c_len)
    tri = (_iota((c_len, c_len), 1) <= _iota((c_len, c_len), 0)).astype(F32)
    lane = _iota((sub, c_len), 1)
    row1 = _iota((c_len, 1), 0)
    ones_cv = jnp.ones((c_len, HA_DV), F32)

    def chunk_body(c, carry):
        for hh in range(HGRN_HEADS_PER_STEP):
            head_chunk(c, hh)
        return carry

    def head_chunk(c, hh):
        r = _chunk_start(c, c_len)
        hl = slice(hh * HA_DK, (hh + 1) * HA_DK)
        fa = fa_ref[pl.ds(r, c_len), hl]
        qa = qa_ref[pl.ds(r, c_len), hl]
        v = ia_ref[pl.ds(r, c_len), hl]
        ga = ga_ref[pl.ds(r, c_len), hl]

        b_ = log1m_lb[:, hl] + _log_sigmoid(fa)
        log_f = jnp.maximum(log_lb[:, hl], b_) + jnp.log1p(jnp.exp(-jnp.abs(log_lb[:, hl] - b_)))
        k = (1.0 - lb[:, hl]) * _sigmoid(-fa)
        if l_valid < l_padded:
            valid = (i * (n_chunks * c_len) + r + row1) < l_valid
            log_f = jnp.where(valid, log_f, 0.0)
            k = jnp.where(valid, k, 0.0)
        q = _silu(qa)
        a_cum = _dot_hi(tri, log_f)

        blocks = []
        for bi in range(c_len // sub):
            lo, hi = bi * sub, (bi + 1) * sub
            if bi == 0:
                att_b = jnp.zeros((sub, c_len), F32)
            else:
                a_ref = a_cum[lo - 1:lo]
                qs = q[lo:hi] * jnp.exp(a_cum[lo:hi] - a_ref)
                ks = jnp.where(row1 < lo, k * jnp.exp(jnp.minimum(a_ref - a_cum, 0.0)), 0.0)
                att_b = _dot_nt(qs.astype(BF16), ks.astype(BF16))
            for s in range(lo, hi):
                r0 = (s // SUBLANES) * SUBLANES
                rel = a_cum[r0:hi] - a_cum[s:s + 1]
                dec = jnp.exp(jnp.where(row1[r0:hi] >= s, rel, NEG_INF))
                col = jnp.sum(q[r0:hi] * k[s:s + 1] * dec, axis=-1, keepdims=True)
                if r0 > lo:
                    col = jnp.concatenate([jnp.zeros((r0 - lo, 1), F32), col], axis=0)
                att_b = jnp.where(lane == s, col, att_b)
            blocks.append(att_b)
        att = blocks[0] if len(blocks) == 1 else jnp.concatenate(blocks, axis=0)

        s_prev = s_scr[hh]
        qd = q * jnp.exp(a_cum)
        o = _dot(qd.astype(BF16), s_prev.astype(BF16)) + _dot(att.astype(BF16), v.astype(BF16))
        a_last = a_cum[c_len - 1:c_len]
        kd = k * jnp.exp(a_last - a_cum)
        dec_s = jnp.exp(_dot_tn(log_f, ones_cv, precision=HI))
        s_scr[hh] = dec_s * s_prev + _dot_tn(kd.astype(BF16), v.astype(BF16))

        rr = lax.rsqrt(jnp.mean(o * o, axis=-1, keepdims=True) + EPS)
        o_ref[pl.ds(r, c_len), hl] = ((o * rr) * nw_ref[...] * _silu(ga)).astype(o_ref.dtype)

    _for_chunks(n_chunks, chunk_body, unroll=True)

    @pl.when(i == pl.num_programs(2) - 1)
    def _():
        sout_ref[...] = s_scr[...]


def _hgrn(z3, lb_logits, norm_w, s0, *, layer, chunk, l_valid):
    bs, lp, _ = z3.shape
    tb = _tile(lp, 4 * chunk)
    kern = functools.partial(_hgrn_kernel, layer=layer, chunk=chunk, n_chunks=tb // chunk,
                             l_valid=l_valid, l_padded=lp)

    hg = HGRN_HEADS_PER_STEP
    wide = hg * HA_DK

    def col(off):
        return pl.BlockSpec((None, tb, wide), lambda b, h, i: (b, i, off // wide + h))

    return pl.pallas_call(
        kern,
        out_shape=(jax.ShapeDtypeStruct((bs, lp, A_W), BF16),
                   jax.ShapeDtypeStruct((bs, HA_H, HA_DK, HA_DV), F32)),
        grid=(bs, HA_H // hg, lp // tb),
        in_specs=[col(OFF_FA), col(OFF_QA), col(OFF_IA), col(OFF_GA),
                  pl.BlockSpec((DEPTH, wide), lambda b, h, i: (0, h)),
                  pl.BlockSpec((1, HA_DV), lambda b, h, i: (0, 0)),
                  pl.BlockSpec((None, hg, HA_DK, HA_DV), lambda b, h, i: (b, h, 0, 0))],
        out_specs=(pl.BlockSpec((None, tb, wide), lambda b, h, i: (b, i, h)),
                   pl.BlockSpec((None, hg, HA_DK, HA_DV), lambda b, h, i: (b, h, 0, 0))),
        scratch_shapes=[pltpu.VMEM((hg, HA_DK, HA_DV), F32)],
        compiler_params=_params(("parallel", "parallel", "arbitrary")),
        name="hgrn2",
    )(z3, z3, z3, z3, lb_logits, norm_w.reshape(1, HA_DV), s0)


def _gdn_kernel(q_ref, k_ref, v_ref, zc_ref, ab_ref, cw_ref, alog_ref, dt_ref, nw_ref, conv0_ref, s0_ref,
                o_ref, sout_ref, s_scr, prev_scr, act_scr, gb_scr, x_scr,
                *, chunk, n_chunks, l_valid, l_padded):
    i = pl.program_id(1)
    c_len = chunk
    tb = n_chunks * c_len

    @pl.when(i == 0)
    def _():
        s_scr[...] = s0_ref[...]
        prev_scr[...] = conv0_ref[...]

    row8 = _iota((SUBLANES, C_K), 0)

    def conv(x, prev, w):
        y = x * w[CONV_W - 1:CONV_W]
        for j in range(1, CONV_W):
            xr = pltpu.roll(x, j, 0)
            head = jnp.where(row8 < j, pltpu.roll(prev, j, 0), xr[:SUBLANES])
            xs = head if tb == SUBLANES else jnp.concatenate([head, xr[SUBLANES:]], axis=0)
            y = y + xs * w[CONV_W - 1 - j:CONV_W - j]
        return _silu(y)

    for n, ref in enumerate((q_ref, k_ref, v_ref)):
        x = ref[...]
        lo, hi = n * C_K, (n + 1) * C_K
        y = conv(x, prev_scr[:, lo:hi], cw_ref[:, lo:hi])
        for h in range(HC_H):
            act_scr[n * HC_H + h] = y[:, h * HC_DK:(h + 1) * HC_DK]
        prev_scr[:, lo:hi] = x[tb - SUBLANES:]

    ab = ab_ref[...]
    g_all = -jnp.exp(alog_ref[...]) * _softplus(ab + dt_ref[...])
    b_all = _sigmoid(ab)
    if l_valid < l_padded:
        valid = (i * tb + _iota((tb, 1), 0)) < l_valid
        g_all = jnp.where(valid, g_all, 0.0)
        b_all = jnp.where(valid, b_all, 0.0)
    gb_scr[:, 0:LANES] = g_all
    gb_scr[:, LANES:2 * LANES] = b_all

    ii = _iota((c_len, c_len), 0)
    jj = _iota((c_len, c_len), 1)
    tri = (jj <= ii).astype(F32)
    tri_u = (ii <= jj).astype(F32)

    def chunk_body(c, carry):
        r = _chunk_start(c, c_len)
        heads = []
        for h in range(HC_H):
            lo, hi = h * HC_DK, (h + 1) * HC_DK
            qh = act_scr[h, pl.ds(r, c_len), :]
            kh = act_scr[HC_H + h, pl.ds(r, c_len), :]
            vh = act_scr[2 * HC_H + h, pl.ds(r, c_len), :]
            qh = qh * lax.rsqrt(jnp.sum(qh * qh, axis=-1, keepdims=True) + EPS) * (HC_DK ** -0.5)
            kh = kh * lax.rsqrt(jnp.sum(kh * kh, axis=-1, keepdims=True) + EPS)
            g_col = gb_scr[pl.ds(r, c_len), h:h + 1]
            b_col = gb_scr[pl.ds(r, c_len), LANES + HC_H + h:LANES + HC_H + h + 1]
            g_b = jnp.broadcast_to(g_col, (c_len, LANES))
            g_cum = _dot_hi(tri, g_b)
            g_row = _dot_tn(g_b, tri_u, precision=HI)[:c_len]
            rel = g_cum[:, :c_len] - g_row
            d_causal = jnp.exp(jnp.where(jj <= ii, rel, NEG_INF))
            d_strict_t = jnp.exp(jnp.where(jj > ii, -rel, NEG_INF))
            kb = (b_col * kh).astype(BF16)
            khb = kh.astype(BF16)
            l_t = _dot_nt(khb, kb) * d_strict_t
            qk = _dot_nt(qh.astype(BF16), khb) * d_causal
            e_g = jnp.exp(g_cum)
            x_scr[h, 0] = b_col * e_g * kh
            x_scr[h, 1] = b_col * vh
            heads.append((qh, kh, l_t, qk, e_g, g_cum))

        for t in range(1, c_len):
            r1 = ((t + SUBLANES - 1) // SUBLANES) * SUBLANES
            for h in range(HC_H):
                col = heads[h][2][0:r1, t:t + 1]
                for part in range(2):
                    contrib = jnp.sum(x_scr[h, part, 0:r1, :] * col, axis=0, keepdims=True)
                    x_scr[h, part, t:t + 1, :] = x_scr[h, part, t:t + 1, :] - contrib

        for h in range(HC_H):
            qh, kh, _, qk, e_g, g_cum = heads[h]
            lo, hi = h * HC_DV, (h + 1) * HC_DV
            s_prev = s_scr[h]
            s_b = s_prev.astype(BF16)
            u = x_scr[h, 1] - _dot(x_scr[h, 0].astype(BF16), s_b)
            u_b = u.astype(BF16)
            o = e_g * _dot(qh.astype(BF16), s_b) + _dot(qk.astype(BF16), u_b)
            g_last = g_cum[c_len - 1:c_len]
            kd = kh * jnp.exp(g_last - g_cum)
            s_scr[h] = jnp.exp(g_last) * s_prev + _dot_tn(kd.astype(BF16), u_b)
            rr = lax.rsqrt(jnp.mean(o * o, axis=-1, keepdims=True) + EPS)
            zc = zc_ref[pl.ds(r, c_len), lo:hi]
            o_ref[pl.ds(r, c_len), lo:hi] = ((o * rr) * nw_ref[...] * _silu(zc)).astype(o_ref.dtype)
        return carry

    _for_chunks(n_chunks, chunk_body)

    @pl.when(i == pl.num_programs(1) - 1)
    def _():
        sout_ref[...] = s_scr[...]


def _gdn(z3, conv_w, a_log, dt_bias, norm_w, conv0, s0, *, chunk, l_valid):
    bs, lp, _ = z3.shape
    tb = _tile(lp, 4 * chunk)
    kern = functools.partial(_gdn_kernel, chunk=chunk, n_chunks=tb // chunk, l_valid=l_valid, l_padded=lp)
    pad = jnp.zeros((LANES - HC_H,), F32)
    alog_row = jnp.concatenate([a_log, pad]).reshape(1, LANES)
    dt_row = jnp.concatenate([dt_bias, pad]).reshape(1, LANES)

    def wide(off):
        return pl.BlockSpec((None, tb, C_K), lambda b, i: (b, i, off // C_K))

    return pl.pallas_call(
        kern,
        out_shape=(jax.ShapeDtypeStruct((bs, lp, C_K), BF16),
                   jax.ShapeDtypeStruct((bs, HC_H, HC_DK, HC_DV), F32)),
        grid=(bs, lp // tb),
        in_specs=[wide(OFF_QC), wide(OFF_KC), wide(OFF_VC), wide(OFF_ZC),
                  pl.BlockSpec((None, tb, LANES), lambda b, i: (b, i, OFF_AB // LANES)),
                  pl.BlockSpec((CONV_W, CONV_CH), lambda b, i: (0, 0)),
                  pl.BlockSpec((1, LANES), lambda b, i: (0, 0)),
                  pl.BlockSpec((1, LANES), lambda b, i: (0, 0)),
                  pl.BlockSpec((1, HC_DV), lambda b, i: (0, 0)),
                  pl.BlockSpec((None, SUBLANES, CONV_CH), lambda b, i: (b, 0, 0)),
                  pl.BlockSpec((None, HC_H, HC_DK, HC_DV), lambda b, i: (b, 0, 0, 0))],
        out_specs=(pl.BlockSpec((None, tb, C_K), lambda b, i: (b, i, 0)),
                   pl.BlockSpec((None, HC_H, HC_DK, HC_DV), lambda b, i: (b, 0, 0, 0))),
        scratch_shapes=[pltpu.VMEM((HC_H, HC_DK, HC_DV), F32),
                        pltpu.VMEM((SUBLANES, CONV_CH), F32),
                        pltpu.VMEM((3 * HC_H, tb, HC_DK), F32),
                        pltpu.VMEM((tb, 2 * LANES), F32),
                        pltpu.VMEM((HC_H, 2, chunk, HC_DK), F32)],
        compiler_params=_params(("parallel", "arbitrary")),
        name="gdn",
    )(z3, z3, z3, z3, z3, conv_w, alog_row, dt_row, norm_w.reshape(1, HC_DV), conv0, s0)


def _foxprep_kernel(q_ref, k_ref, v_ref, fb_ref, gq_ref, gk_ref, bf_ref, bd_ref,
                    qo_ref, ko_ref, kbo_ref, vbo_ref, lfo_ref, *rest, cumsum):
    bd = bd_ref[...]

    def head_rms(x, g):
        x2 = x * x
        hi = x2.astype(BF16)
        lo = (x2 - hi.astype(F32)).astype(BF16)
        ss = _dot(hi, bd) + _dot(lo, bd)
        return x * lax.rsqrt(ss * (1.0 / HB_D) + EPS) * g

    qn = head_rms(q_ref[...], gq_ref[...])
    kn = head_rms(k_ref[...], gk_ref[...])
    qo_ref[...] = (qn * (HB_D ** -0.5)).astype(BF16)
    ko_ref[...] = kn
    kbo_ref[...] = kn.astype(BF16)
    vbo_ref[...] = v_ref[...].astype(BF16)
    lf = _log_sigmoid(fb_ref[...] + bf_ref[...])
    lf = jnp.where(_iota(lf.shape, 1) < HB_H, lf, 0.0)
    lfo_ref[...] = lf
    if cumsum:
        ft_ref, carry = rest
        tm = lf.shape[0]

        @pl.when(pl.program_id(1) == 0)
        def _():
            carry[...] = jnp.zeros_like(carry)

        tri = (_iota((tm, tm), 1) <= _iota((tm, tm), 0)).astype(F32)
        f_cum = _dot_hi(tri, lf) + carry[...]
        carry[...] = f_cum[tm - 1:tm]
        ft_ref[...] = f_cum.T[:HB_H]


def _fox_prep(z3, gq, gk, b_f, *, cumsum):
    bg, lg, _ = z3.shape
    tm = _tile(lg, 256)
    gq_row = jnp.tile(gq, HB_H).reshape(1, B_W)
    gk_row = jnp.tile(gk, HB_H).reshape(1, B_W)
    bf_row = jnp.concatenate([b_f, jnp.zeros((LANES - HB_H,), F32)]).reshape(1, LANES)
    seg = jnp.arange(B_W) // HB_D
    bd = (seg[:, None] == seg[None, :]).astype(BF16)

    def wide(off):
        return pl.BlockSpec((None, tm, B_W), lambda b, i: (b, i, off // B_W))

    tok = pl.BlockSpec((None, tm, B_W), lambda b, i: (b, i, 0))
    out_shape = [jax.ShapeDtypeStruct((bg, lg, B_W), BF16), jax.ShapeDtypeStruct((bg, lg, B_W), F32),
                 jax.ShapeDtypeStruct((bg, lg, B_W), BF16), jax.ShapeDtypeStruct((bg, lg, B_W), BF16),
                 jax.ShapeDtypeStruct((bg, lg, LANES), F32)]
    out_specs = [tok, tok, tok, tok, pl.BlockSpec((None, tm, LANES), lambda b, i: (b, i, 0))]
    scratch = []
    if cumsum:
        out_shape.append(jax.ShapeDtypeStruct((bg, HB_H, lg), F32))
        out_specs.append(pl.BlockSpec((None, HB_H, tm), lambda b, i: (b, 0, i)))
        scratch.append(pltpu.VMEM((1, LANES), F32))
    return pl.pallas_call(
        functools.partial(_foxprep_kernel, cumsum=cumsum),
        out_shape=tuple(out_shape),
        grid=(bg, lg // tm),
        in_specs=[wide(OFF_QB), wide(OFF_KB), wide(OFF_VB),
                  pl.BlockSpec((None, tm, LANES), lambda b, i: (b, i, OFF_FB // LANES)),
                  pl.BlockSpec((1, B_W), lambda b, i: (0, 0)),
                  pl.BlockSpec((1, B_W), lambda b, i: (0, 0)),
                  pl.BlockSpec((1, LANES), lambda b, i: (0, 0)),
                  pl.BlockSpec((B_W, B_W), lambda b, i: (0, 0))],
        out_specs=tuple(out_specs),
        scratch_shapes=scratch,
        compiler_params=_params(("parallel", "arbitrary")),
        name="fox_prep",
    )(z3, z3, z3, z3, gq_row, gk_row, bf_row, bd)


def _foxattn_kernel(q_ref, k_ref, v_ref, f_ref, o_ref, *, tq):
    qi = pl.program_id(2)
    q = q_ref[...]
    lane_q = _iota(q.shape, 1)
    q_heads = (jnp.where(lane_q < HB_D, q, jnp.zeros_like(q)), jnp.where(lane_q >= HB_D, q, jnp.zeros_like(q)))
    row = _iota((tq, tq), 0)
    colm = _iota((tq, tq), 1)

    def step(j, carry, masked):
        r = pl.multiple_of(j * tq, tq)
        kj = k_ref[pl.ds(r, tq), :]
        vj = v_ref[pl.ds(r, tq), :]
        fj = f_ref[j]
        out = []
        for hh in range(2):
            m, l, acc = carry[hh]
            s = _dot_nt(q_heads[hh], kj) - fj[hh:hh + 1, :]
            if masked:
                s = jnp.where(colm <= row, s, NEG_INF)
            m_new = jnp.maximum(m, jnp.max(s, axis=-1, keepdims=True))
            alpha = jnp.exp(m - m_new)
            p = jnp.exp(s - m_new)
            l = alpha * l + jnp.sum(p, axis=-1, keepdims=True)
            acc = alpha * acc + _dot(p.astype(BF16), vj)
            out.append((m_new, l, acc))
        return tuple(out)

    init = tuple((jnp.full((tq, 1), -1e30, F32), jnp.zeros((tq, 1), F32), jnp.zeros((tq, LANES), F32))
                 for _ in range(2))
    carry = lax.fori_loop(0, qi, lambda j, c: step(j, c, False), init)
    (_, l0, a0), (_, l1, a1) = step(qi, carry, True)
    o = jnp.where(_iota((tq, LANES), 1) < HB_D, a0 / l0, a1 / l1)
    o_ref[...] = o.astype(o_ref.dtype)


def _fox_attn_prompt(qb, kb, vb, ft):
    bg, s_len, _ = qb.shape
    tq = _tile(s_len, 512)
    nk = s_len // tq
    pairs = HB_H // 2
    f5 = ft.reshape(bg, pairs, 2, nk, tq).transpose(0, 1, 3, 2, 4)
    return pl.pallas_call(
        functools.partial(_foxattn_kernel, tq=tq),
        out_shape=jax.ShapeDtypeStruct((bg, s_len, B_W), BF16),
        grid=(bg, pairs, nk),
        in_specs=[pl.BlockSpec((None, tq, LANES), lambda b, p, i: (b, i, p)),
                  pl.BlockSpec((None, s_len, LANES), lambda b, p, i: (b, 0, p)),
                  pl.BlockSpec((None, s_len, LANES), lambda b, p, i: (b, 0, p)),
                  pl.BlockSpec((None, None, nk, 2, tq), lambda b, p, i: (b, p, 0, 0, 0))],
        out_specs=pl.BlockSpec((None, tq, LANES), lambda b, p, i: (b, i, p)),
        compiler_params=_params(("parallel", "parallel", "arbitrary")),
        name="fox_attn_prompt",
    )(qb, kb, vb, f5)


def _foxsample_kernel(pt_ref, q_ref, kn_ref, vn_ref, lfn_ref, *rest, n_new, ps):
    k_pages = rest[0:ps]
    v_pages = rest[ps:2 * ps]
    f_pages = rest[2 * ps:3 * ps]
    o_ref, q_scr, m_scr, l_scr, acc_scr, fc_scr = rest[3 * ps:]
    g = pl.program_id(1)
    rows = n_new * HB_H
    hmask = (_iota((HB_H, B_W), 1) // HB_D) == _iota((HB_H, B_W), 0)

    @pl.when(g == 0)
    def _():
        for t in range(n_new):
            qt = jnp.broadcast_to(q_ref[t:t + 1, :], (HB_H, B_W))
            q_scr[t * HB_H:(t + 1) * HB_H, :] = jnp.where(hmask, qt, 0.0)
        m_scr[...] = jnp.full_like(m_scr, -1e30)
        l_scr[...] = jnp.zeros_like(l_scr)
        acc_scr[...] = jnp.zeros_like(acc_scr)
        fc_scr[...] = jnp.zeros_like(fc_scr)

    tri_u = (_iota((PAGE, PAGE), 0) <= _iota((PAGE, PAGE), 1)).astype(F32)

    def update(s, v_b):
        m = m_scr[...]
        m_new = jnp.maximum(m, jnp.max(s, axis=-1, keepdims=True))
        alpha = jnp.exp(m - m_new)
        p = jnp.exp(s - m_new)
        l_scr[...] = alpha * l_scr[...] + jnp.sum(p, axis=-1, keepdims=True)
        acc_scr[...] = alpha * acc_scr[...] + _dot_nt(p.astype(BF16), v_b)
        m_scr[...] = m_new

    def cum_forget(lf_parts):
        f_loc = [_dot_hi(lf_t, tri_u) for lf_t in lf_parts]
        off = fc_scr[...]
        out = []
        for fl in f_loc:
            out.append(fl + off)
            off = off + fl[:, PAGE - 1:PAGE]
        fc_scr[...] = off
        return out

    f_cat = jnp.concatenate(cum_forget([f_pages[r][...] for r in range(ps)]), axis=1)
    bias = jnp.concatenate([f_cat] * n_new, axis=0)
    k_cat = jnp.concatenate([k_pages[r][...].astype(BF16) for r in range(ps)], axis=1)
    v_cat = jnp.concatenate([v_pages[r][...].astype(BF16) for r in range(ps)], axis=1)
    q_rows = q_scr[...].astype(BF16)
    update(_dot(q_rows, k_cat) - bias, v_cat)

    @pl.when(g == pl.num_programs(1) - 1)
    def _():
        f_new = cum_forget([lfn_ref[...]])[0]
        s = _dot(q_rows, kn_ref[...].astype(BF16)) - jnp.concatenate([f_new] * n_new, axis=0)
        visible = _iota((rows, PAGE), 1) <= (_iota((rows, PAGE), 0) // HB_H)
        update(jnp.where(visible, s, NEG_INF), vn_ref[...].astype(BF16))
        o = acc_scr[...] / l_scr[...]
        out = jnp.zeros((SUBLANES, B_W), F32)
        out_row = _iota((SUBLANES, B_W), 0)
        for t in range(n_new):
            ot = jnp.where(hmask, o[t * HB_H:(t + 1) * HB_H, :], 0.0)
            out = jnp.where(out_row == t, jnp.sum(ot, axis=0, keepdims=True), out)
        o_ref[...] = out.astype(o_ref.dtype)


def _fox_attn_sample(layer, q_new, k_new, v_new, lf_new, cache_kt, cache_vt, cache_lf_t, page_table):
    db, n_new, _ = q_new.shape
    n_pages = page_table.shape[1]
    n_pool = cache_kt.shape[0] // DEPTH
    ps = min(PAGES_PER_STEP, n_pages)
    assert n_pages % ps == 0 and n_new <= SUBLANES
    qp = jnp.pad(q_new, ((0, 0), (0, SUBLANES - n_new), (0, 0)))
    pad_pos = ((0, 0), (0, 0), (0, PAGE - n_new))
    knp = jnp.pad(k_new.transpose(0, 2, 1), pad_pos)
    vnp = jnp.pad(v_new.transpose(0, 2, 1), pad_pos)
    lfp = jnp.pad(lf_new.transpose(0, 2, 1), pad_pos)
    base = layer * n_pool

    def page_spec(r, rows, cols):
        return pl.BlockSpec((None, rows, cols), lambda b, g, pt: (base + pt[b, g * ps + r], 0, 0))

    in_specs = [pl.BlockSpec((None, SUBLANES, B_W), lambda b, g, pt: (b, 0, 0)),
                pl.BlockSpec((None, B_W, PAGE), lambda b, g, pt: (b, 0, 0)),
                pl.BlockSpec((None, B_W, PAGE), lambda b, g, pt: (b, 0, 0)),
                pl.BlockSpec((None, HB_H, PAGE), lambda b, g, pt: (b, 0, 0))]
    in_specs += [page_spec(r, B_W, PAGE) for r in range(ps)]
    in_specs += [page_spec(r, B_W, PAGE) for r in range(ps)]
    in_specs += [page_spec(r, HB_H, PAGE) for r in range(ps)]
    rows = n_new * HB_H
    return pl.pallas_call(
        functools.partial(_foxsample_kernel, n_new=n_new, ps=ps),
        out_shape=jax.ShapeDtypeStruct((db, SUBLANES, B_W), BF16),
        grid_spec=pltpu.PrefetchScalarGridSpec(
            num_scalar_prefetch=1,
            grid=(db, n_pages // ps),
            in_specs=in_specs,
            out_specs=pl.BlockSpec((None, SUBLANES, B_W), lambda b, g, pt: (b, 0, 0)),
            scratch_shapes=[pltpu.VMEM((rows, B_W), F32), pltpu.VMEM((rows, 1), F32),
                            pltpu.VMEM((rows, 1), F32), pltpu.VMEM((rows, B_W), F32),
                            pltpu.VMEM((HB_H, 1), F32)]),
        compiler_params=_params(("parallel", "arbitrary")),
        name="fox_attn_sample",
    )(page_table, qp, knp, vnp, lfp, *([cache_kt] * ps), *([cache_vt] * ps), *([cache_lf_t] * ps))


def _merge_kernel(oa_ref, ob_ref, oc_ref, ga_ref, gb_ref, gc_ref, x_ref, gt1_ref, sh2_ref, sc2_ref,
                  wa_ref, wb_ref, wc_ref, wo_ref, nf_ref, wq_ref, keys_ref,
                  x1_ref, h2t_ref, st_ref):
    merged = (_sigmoid(ga_ref[...]) * _dot(oa_ref[...], wa_ref[...])
              + _sigmoid(gb_ref[...]) * _dot(ob_ref[...], wb_ref[...])
              + _sigmoid(gc_ref[...]) * _dot(oc_ref[...], wc_ref[...]))
    x1 = x_ref[...] + gt1_ref[...] * _dot(merged.astype(BF16), wo_ref[...])
    x1_ref[...] = x1
    r = lax.rsqrt(jnp.mean(x1 * x1, axis=-1, keepdims=True) + EPS)
    h2 = (x1 * r) * nf_ref[...] * (1.0 + sc2_ref[...]) + sh2_ref[...]
    h2t_ref[...] = h2.T.astype(BF16)
    qb = _dot(h2.astype(BF16), wq_ref[...]).astype(BF16)
    half = P_QD // 2
    for h in range(P_H):
        for p in range(2):
            lo = (h * 2 + p) * half
            st_ref[lo:lo + half, :] = _dot_nt(keys_ref[p], qb[:, lo:lo + half])


def _merge(oa, ob, oc, z3, x, mod, w_a, w_b, w_c, w_o, norm_ffn, w_q, keys):
    bg, lg, _ = x.shape
    tm = _tile(lg, 256)
    nl = lg // tm
    t_all = bg * lg

    def tok(width):
        return pl.BlockSpec((None, tm, width), lambda b, i: (b, i, 0))

    def gate(k):
        return pl.BlockSpec((None, tm, D), lambda b, i: (b, i, OFF_GATE // D + k))

    def full(shape):
        return pl.BlockSpec(shape, lambda b, i: (0,) * len(shape))

    return pl.pallas_call(
        _merge_kernel,
        out_shape=(jax.ShapeDtypeStruct((bg, lg, D), F32),
                   jax.ShapeDtypeStruct((D, t_all), BF16),
                   jax.ShapeDtypeStruct((P_H * P_QD, t_all), F32)),
        grid=(bg, nl),
        in_specs=[tok(A_W), tok(B_W), tok(C_K), gate(0), gate(1), gate(2), tok(D),
                  _mod_spec(mod, tm, 2), _mod_spec(mod, tm, 3), _mod_spec(mod, tm, 4),
                  full((A_W, D)), full((B_W, D)), full((C_K, D)), full((D, D)), full((1, D)),
                  full((D, P_H * P_QD)), full((2, P_NK, P_QD // 2))],
        out_specs=(tok(D),
                   pl.BlockSpec((D, tm), lambda b, i: (0, b * nl + i)),
                   pl.BlockSpec((P_H * P_QD, tm), lambda b, i: (0, b * nl + i))),
        compiler_params=_params(("parallel", "parallel")),
        name="merge",
    )(oa, ob, oc, z3, z3, z3, x, mod, mod, mod, w_a, w_b, w_c, w_o, norm_ffn.reshape(1, D), w_q, keys)


def _topk_kernel(st_ref, o_ref):
    tt = st_ref.shape[1]

    def top_rows(s_ref, lo):
        n_v = P_NK // SUBLANES
        v = [s_ref[lo + k * SUBLANES:lo + (k + 1) * SUBLANES, :] for k in range(n_v)]
        k = 2
        while k <= n_v:
            j = k // 2
            while j >= 1:
                for a in range(n_v):
                    b = a ^ j
                    if b > a:
                        hi, lo_v = jnp.maximum(v[a], v[b]), jnp.minimum(v[a], v[b])
                        v[a], v[b] = (hi, lo_v) if (a & k) == 0 else (lo_v, hi)
                j //= 2
            k *= 2
        rows = []
        for it in range(P_TOPK + 1):
            m = jnp.max(v[0], axis=0, keepdims=True)
            rows.append(m)
            popped = v[0] == m
            for a in range(P_TOPK - it):
                v[a] = jnp.where(popped, v[a + 1] if a + 1 < n_v else NEG_INF, v[a])
        return rows

    rank = _iota((P_TOPK, tt), 0)
    rank8 = _iota((SUBLANES, tt), 0)
    thr_rows, nrm_rows = [], []
    for h in range(P_H):
        lo = h * P_QD
        v1 = top_rows(st_ref, lo)
        v2 = top_rows(st_ref, lo + P_NK)
        v1s = jnp.concatenate(v1[:P_TOPK], axis=0)
        v2s = jnp.concatenate(v2[:P_TOPK], axis=0)
        groups = [jnp.where(rank8 == 0, v1[P_TOPK] + v2[0], jnp.where(rank8 == 1, v1[0] + v2[P_TOPK], NEG_INF))]
        for b in range(3):
            groups.append(jnp.where(rank < P_TOPK // (b + 1), v1s + v2[b], NEG_INF))
        for a in range(4):
            nb = P_TOPK // (a + 1)
            n_rows = P_TOPK if nb > SUBLANES else SUBLANES
            rk = _iota((n_rows, tt), 0)
            ok = jnp.where(rk >= 3, rk, nb) < nb
            groups.append(jnp.where(ok, v2s[:n_rows] + v1[a], NEG_INF))
        m_top = None
        z = None
        for it in range(P_TOPK):
            m = functools.reduce(jnp.maximum, [jnp.max(gp, axis=0, keepdims=True) for gp in groups])
            if it == 0:
                m_top = m
                z = jnp.ones_like(m)
            else:
                z = z + jnp.exp(m - m_top)
            groups = [jnp.where(gp == m, NEG_INF, gp) for gp in groups]
        m_next = functools.reduce(jnp.maximum, [jnp.max(gp, axis=0, keepdims=True) for gp in groups])
        thr_rows.append(0.5 * m + 0.5 * m_next)
        nrm_rows.append(-(m_top + jnp.log(z)))
    o_ref[...] = jnp.concatenate(thr_rows + nrm_rows, axis=0)


def _topk(st):
    t_all = st.shape[1]
    tt = _tile(t_all, 256)
    return pl.pallas_call(
        _topk_kernel,
        out_shape=jax.ShapeDtypeStruct((2 * P_H, t_all), F32),
        grid=(t_all // tt,),
        in_specs=[pl.BlockSpec((P_H * P_QD, tt), lambda t: (0, t))],
        out_specs=pl.BlockSpec((2 * P_H, tt), lambda t: (0, t)),
        compiler_params=_params(("parallel",)),
        name="peer_topk",
    )(st)


def _peer_kernel(st_ref, stat_ref, h2t_ref, u0_ref, u1_ref, vt0_ref, vt1_ref, x1_ref, gt2_ref, o_ref,
                 acc_scr, e1_scr, tau_scr, e2_scr, s2_scr, ht0_scr, ht1_scr, wa0_scr, wa1_scr, *, ti, n_tiles):
    s = pl.program_id(2)
    tt = st_ref.shape[1]

    @pl.when(s == 0)
    def _():
        acc_scr[...] = jnp.zeros_like(acc_scr)
        ht0_scr[...] = jnp.zeros_like(ht0_scr)
        ht1_scr[...] = jnp.zeros_like(ht1_scr)
        wa0_scr[...] = jnp.zeros_like(wa0_scr)
        wa1_scr[...] = jnp.zeros_like(wa1_scr)
        for h in range(P_H):
            lo = h * P_QD
            hr = slice(h * P_NK, (h + 1) * P_NK)
            for lt in range(tt // LANES):
                ls = slice(lt * LANES, (lt + 1) * LANES)
                s1 = st_ref[lo:lo + P_NK, ls]
                s2 = st_ref[lo + P_NK:lo + 2 * P_NK, ls]
                mx2 = jnp.max(s2, axis=0, keepdims=True)
                s2_scr[lt, hr, :] = s2
                e2_scr[lt, hr, :] = jnp.exp(s2 - mx2)
                e1_scr[hr, ls] = 0.5 * jnp.exp(s1 + (stat_ref[P_H + h:P_H + h + 1, ls] + mx2))
                tau_scr[hr, ls] = stat_ref[h:h + 1, ls] - s1

    c0 = math.sqrt(2.0 / math.pi)
    c1 = c0 * 0.044715
    tile_b = jnp.clip(s - 1, 0, n_tiles - 1)

    def stages(ht_w, ht_r, wa_w, wa_r):
        te = ti * P_NK
        n_lt = tt // LANES
        n_p = min(PEER_PIECES, ti * n_lt)
        per_group = ti * n_lt // n_p
        n_split = 2 if n_lt % 2 == 0 else 1
        m_split = n_p // n_split
        ma, mc, nw = te // m_split, D // m_split, n_lt // n_split
        assert per_group >= 1 and (te // 2) % ma == 0 and (D // 2) % mc == 0

        def matmul_pieces(pi):
            mi, ni = pi // n_split, pi % n_split
            lts = range(ni * nw, (ni + 1) * nw)
            u_half = (u0_ref, u1_ref)[mi * ma // (te // 2)]
            a0 = (mi * ma) % (te // 2)
            part_a = _dot(u_half[a0:a0 + ma, :], h2t_ref[:, ni * nw * LANES:(ni + 1) * nw * LANES])
            vt_half = (vt0_ref, vt1_ref)[mi * mc // (D // 2)]
            v0 = (mi * mc) % (D // 2)
            part_c = _dot(vt_half[v0:v0 + mc, :], jnp.concatenate([wa_r[lt] for lt in lts], axis=1))
            for k, lt in enumerate(lts):
                ht_w[lt, mi * ma:(mi + 1) * ma, :] = part_a[:, k * LANES:(k + 1) * LANES]
                acc_scr[lt, mi * mc:(mi + 1) * mc, :] += part_c[:, k * LANES:(k + 1) * LANES]

        for ii in range(ti):
            i_row = tile_b * ti + ii
            tau_rows = [tau_scr[pl.ds(h * P_NK + i_row, 1), :] for h in range(P_H)]
            e1_rows = [e1_scr[pl.ds(h * P_NK + i_row, 1), :] for h in range(P_H)]
            for lt in range(n_lt):
                if (ii * n_lt + lt) % per_group == 0:
                    matmul_pieces((ii * n_lt + lt) // per_group)
                ls = slice(lt * LANES, (lt + 1) * LANES)
                tau_b = [jnp.broadcast_to(tau_rows[h][:, ls], (PEER_JB, LANES)) for h in range(P_H)]
                e1_b = [jnp.broadcast_to(e1_rows[h][:, ls], (PEER_JB, LANES)) for h in range(P_H)]
                for jb in range(P_NK // PEER_JB):
                    j0 = jb * PEER_JB
                    hs = ht_r[lt, ii * P_NK + j0:ii * P_NK + j0 + PEER_JB, :]
                    act = hs * (1.0 + jnp.tanh(hs * (c0 + c1 * (hs * hs))))
                    w = None
                    for h in range(P_H):
                        jr = slice(h * P_NK + j0, h * P_NK + j0 + PEER_JB)
                        wh = jnp.where(s2_scr[lt, jr, :] >= tau_b[h], e2_scr[lt, jr, :] * e1_b[h], 0.0)
                        w = wh if w is None else w + wh
                    wa_w[lt, ii * P_NK + j0:ii * P_NK + j0 + PEER_JB, :] = (w * act).astype(BF16)

    @pl.when(s % 2 == 0)
    def _():
        stages(ht0_scr, ht1_scr, wa1_scr, wa0_scr)

    @pl.when(s % 2 == 1)
    def _():
        stages(ht1_scr, ht0_scr, wa0_scr, wa1_scr)

    @pl.when(s == pl.num_programs(2) - 1)
    def _():
        for lt in range(tt // LANES):
            rows = slice(lt * LANES, (lt + 1) * LANES)
            gt2 = gt2_ref[rows, :] if gt2_ref.shape[0] == tt else gt2_ref[0:1, :]
            o_ref[rows, :] = x1_ref[rows, :] + gt2 * acc_scr[lt].T


def _peer(st, stats, h2t, u_b, vt_b, x1, mod):
    bg, lg, _ = x1.shape
    tt = _tile(lg, 512)
    nl = lg // tt
    ti = PEER_TI
    te = ti * P_NK
    assert vt_b.shape == (N_EXP // te, D, te)
    n_tiles = N_EXP // te
    if mod.shape[1] == 1:
        mod = jnp.broadcast_to(mod, (bg, SUBLANES, 6 * D))
        gate_spec = pl.BlockSpec((None, SUBLANES, D), lambda b, i, s: (b, 0, 5))
    else:
        gate_spec = _mod_spec(mod, tt, 5)
    return pl.pallas_call(
        functools.partial(_peer_kernel, ti=ti, n_tiles=n_tiles),
        out_shape=jax.ShapeDtypeStruct((bg, lg, D), F32),
        grid=(bg, nl, n_tiles + 2),
        in_specs=[pl.BlockSpec((P_H * P_QD, tt), lambda b, i, s: (0, b * nl + i)),
                  pl.BlockSpec((2 * P_H, tt), lambda b, i, s: (0, b * nl + i)),
                  pl.BlockSpec((D, tt), lambda b, i, s: (0, b * nl + i)),
                  pl.BlockSpec((te // 2, D), lambda b, i, s: (2 * jnp.minimum(s, n_tiles - 1), 0)),
                  pl.BlockSpec((te // 2, D), lambda b, i, s: (2 * jnp.minimum(s, n_tiles - 1) + 1, 0)),
                  pl.BlockSpec((None, D // 2, te), lambda b, i, s: (jnp.clip(s - 2, 0, n_tiles - 1), 0, 0)),
                  pl.BlockSpec((None, D // 2, te), lambda b, i, s: (jnp.clip(s - 2, 0, n_tiles - 1), 1, 0)),
                  pl.BlockSpec((None, tt, D), lambda b, i, s: (b, i, 0)),
                  gate_spec],
        out_specs=pl.BlockSpec((None, tt, D), lambda b, i, s: (b, i, 0)),
        scratch_shapes=[pltpu.VMEM((tt // LANES, D, LANES), F32),
                        pltpu.VMEM((P_H * P_NK, tt), F32), pltpu.VMEM((P_H * P_NK, tt), F32),
                        pltpu.VMEM((tt // LANES, P_H * P_NK, LANES), F32),
                        pltpu.VMEM((tt // LANES, P_H * P_NK, LANES), F32),
                        pltpu.VMEM((tt // LANES, te, LANES), F32), pltpu.VMEM((tt // LANES, te, LANES), F32),
                        pltpu.VMEM((tt // LANES, te, LANES), BF16), pltpu.VMEM((tt // LANES, te, LANES), BF16)],
        compiler_params=_params(("parallel", "parallel", "arbitrary")),
        name="peer_experts",
    )(st, stats, h2t, u_b, u_b, vt_b, vt_b, x1, mod)


def _pack_w_in(w):
    pad = jnp.zeros((D, LANES - 8), w.dtype)
    o_fb = 2 * A_W + 2 * A_W + 3 * B_W
    o_c = o_fb + HB_H
    o_ab = o_c + 3 * C_K
    o_zc = o_ab + 2 * HC_H
    o_gate = o_zc + C_K
    packed = jnp.concatenate([
        w[:, 0:4 * A_W], w[:, o_gate:o_gate + 3 * D], w[:, 4 * A_W:o_fb], w[:, o_c:o_ab],
        w[:, o_zc:o_gate], w[:, o_fb:o_c], pad, w[:, o_ab:o_zc], pad], axis=1)
    assert packed.shape[1] == NP
    return packed.astype(BF16)


def _layer(l, x, mod, w, sample):
    bg, lg, _ = x.shape
    z = _in_proj(x, mod, w["norm_mix"][l], w["w_in"][l])
    if sample is None:
        z3 = z
        bs, l_seq = bg, lg
        chunk = LIN_C
        hgrn0 = jnp.zeros((bs, HA_H, HA_DK, HA_DV), F32)
        gdn0 = jnp.zeros((bs, HC_H, HC_DK, HC_DV), F32)
        conv0 = jnp.zeros((bs, SUBLANES, CONV_CH), F32)
    else:
        bs, l_seq = sample["db"], sample["t"]
        chunk = SUBLANES
        z3 = jnp.pad(z.reshape(bs, l_seq, NP), ((0, 0), (0, SUBLANES - l_seq), (0, 0)))
        hgrn0 = sample["state_hgrn"][l]
        gdn0 = sample["state_gdn"][l]
        conv0 = jnp.pad(sample["state_conv"][l], ((0, 0), (SUBLANES - (CONV_W - 1), 0), (0, 0)))

    o_a, s_hgrn = _hgrn(z3, w["hgrn_lb_logits"], w["hgrn_norm"][l], hgrn0, layer=l, chunk=chunk, l_valid=l_seq)
    o_c, s_gdn = _gdn(z3, w["gdn_conv_w"][l], w["gdn_a_log"][l], w["gdn_dt_bias"][l], w["gdn_norm"][l],
                      conv0, gdn0, chunk=min(chunk, GDN_C), l_valid=l_seq)
    zseq = z.reshape(bs, l_seq, NP)
    conv_new = zseq[:, l_seq - (CONV_W - 1):, OFF_QC:OFF_QC + CONV_CH]

    prep = _fox_prep(z, w["fox_q_norm"][l], w["fox_k_norm"][l], w["fox_b_f"][l], cumsum=sample is None)
    qn, kn, knb, vnb, lf = prep[:5]
    v_b = z[:, :, OFF_VB:OFF_VB + B_W]
    if sample is None:
        o_b = _fox_attn_prompt(qn, knb, vnb, prep[5])
    else:
        o_b = _fox_attn_sample(
            l, qn.astype(F32).reshape(bs, l_seq, B_W), kn.reshape(bs, l_seq, B_W), v_b.reshape(bs, l_seq, B_W),
            lf[0, :, :HB_H].reshape(bs, l_seq, HB_H), sample["cache_kt"], sample["cache_vt"], sample["cache_lf_t"],
            sample["page_table"])
        o_b = o_b[:, :l_seq].reshape(bg, lg, B_W)
        o_a = o_a[:, :l_seq].reshape(bg, lg, A_W)
        o_c = o_c[:, :l_seq].reshape(bg, lg, C_K)

    x1, h2t, st = _merge(o_a, o_b, o_c, z, x, mod, w["w_br_a"][l], w["w_br_b"][l], w["w_br_c"][l],
                         w["w_out"][l], w["norm_ffn"][l], w["peer_w_q"][l], w["peer_keys"][l])
    stats = _topk(st)
    x2 = _peer(st, stats, h2t, w["peer_u"][l], w["peer_vt"][l], x1, mod)

    k_leaf = kn.reshape(bs, l_seq, HB_H, HB_D)
    v_leaf = v_b.reshape(bs, l_seq, HB_H, HB_D)
    lf_leaf = lf[:, :, :HB_H].reshape(bs, l_seq, HB_H)
    return x2, (k_leaf, v_leaf, lf_leaf, s_hgrn, s_gdn, conv_new)


def _trunk(x, mods, w, sample):
    leaves = [[] for _ in range(6)]
    for l in range(DEPTH):
        x, st = _layer(l, x, mods[l], w, sample)
        for lst, s in zip(leaves, st):
            lst.append(s)
    return x, [jnp.stack(v) for v in leaves]


def kernel(x_prompt, x_sample, c_prompt, c_sample, cache_fox_k, cache_fox_v, cache_fox_logf, page_table,
           state_hgrn, state_gdn, state_gdn_conv, w_ada, b_ada, norm_mix, norm_ffn, w_in, hgrn_lb_logits,
           hgrn_norm, fox_b_f, fox_q_norm, fox_k_norm, gdn_conv_w, gdn_a_log, gdn_dt_bias, gdn_norm,
           w_br_a, w_br_b, w_br_c, w_out, peer_w_q, peer_keys, peer_u, peer_v):
    bp = x_prompt.shape[0]
    db, t_new, _ = x_sample.shape
    n_pool = cache_fox_k.shape[1]

    w = {
        "norm_mix": norm_mix, "norm_ffn": norm_ffn, "hgrn_lb_logits": hgrn_lb_logits, "hgrn_norm": hgrn_norm,
        "fox_b_f": fox_b_f, "fox_q_norm": fox_q_norm, "fox_k_norm": fox_k_norm, "gdn_conv_w": gdn_conv_w,
        "gdn_a_log": gdn_a_log, "gdn_dt_bias": gdn_dt_bias, "gdn_norm": gdn_norm,
        "w_in": [_pack_w_in(w_in[l]) for l in range(DEPTH)],
        "w_br_a": w_br_a.astype(BF16), "w_br_b": w_br_b.astype(BF16), "w_br_c": w_br_c.astype(BF16),
        "w_out": w_out.astype(BF16), "peer_w_q": peer_w_q.astype(BF16), "peer_keys": peer_keys.astype(BF16),
        "peer_u": peer_u.astype(BF16),
        "peer_vt": peer_v.astype(BF16).reshape(DEPTH, N_EXP // (PEER_TI * P_NK), PEER_TI * P_NK, D).transpose(0, 1, 3, 2),
    }

    n_c = bp + db
    c_all = jnp.pad(jnp.concatenate([c_prompt, c_sample], axis=0), ((0, (-n_c) % SUBLANES), (0, 0)))
    mod = _ada(c_all, w_ada, b_ada)
    mods_p = [mod[l, :bp].reshape(bp, 1, 6 * D) for l in range(DEPTH)]
    mods_s = [jnp.repeat(mod[l, bp:n_c], t_new, axis=0).reshape(1, db * t_new, 6 * D) for l in range(DEPTH)]

    sample = {
        "db": db, "t": t_new, "page_table": page_table,
        "state_hgrn": state_hgrn, "state_gdn": state_gdn, "state_conv": state_gdn_conv,
        "cache_kt": cache_fox_k.transpose(0, 1, 3, 4, 2).reshape(DEPTH * n_pool, B_W, PAGE),
        "cache_vt": cache_fox_v.transpose(0, 1, 3, 4, 2).reshape(DEPTH * n_pool, B_W, PAGE),
        "cache_lf_t": cache_fox_logf.transpose(0, 1, 3, 2).reshape(DEPTH * n_pool, HB_H, PAGE),
    }

    y_p, leaves_p = _trunk(x_prompt, mods_p, w, None)
    y_s, leaves_s = _trunk(x_sample.reshape(1, db * t_new, D), mods_s, w, sample)
    return (y_p, y_s.reshape(db, t_new, D), *leaves_p, *leaves_s)
```

```python
import functools
import math

import jax
import jax.numpy as jnp
from jax import lax
from jax.experimental import pallas as pl
from jax.experimental.pallas import tpu as pltpu

F32 = jnp.float32
BF16 = jnp.bfloat16
HI = lax.Precision.HIGHEST
NEG_INF = float("-inf")

D = 1024
DEPTH = 2
HA_H, HA_DK, HA_DV = 4, 128, 128
HB_H, HB_D = 8, 64
HC_H, HC_DK, HC_DV = 4, 128, 128
CONV_W = 4
P_H, P_TOPK, P_NK, P_QD = 8, 16, 128, 256
N_EXP = P_NK * P_NK
EPS = 1e-6
A_W = HA_H * HA_DK
B_W = HB_H * HB_D
C_K = HC_H * HC_DK
CONV_CH = 3 * C_K
PAGE = 128

LANES = 128
SUBLANES = 8
VMEM_LIMIT = 52 * 1024 * 1024

OFF_FA, OFF_QA, OFF_IA, OFF_GA = 0, 512, 1024, 1536
OFF_GATE = 2048
OFF_QB, OFF_KB, OFF_VB = 5120, 5632, 6144
OFF_QC, OFF_KC, OFF_VC = 6656, 7168, 7680
OFF_ZC = 8192
OFF_FB = 8704
OFF_AB = 8832
NP = 8960

LIN_C = 64
GDN_C = 64
HGRN_SUB = 16
HGRN_HEADS_PER_STEP = 4
PEER_JB = 32
PEER_TI = 4
PEER_PIECES = 8
PEER_KW = 256
PAGES_PER_STEP = 16


def _dot(a, b):
    return jnp.dot(a, b, preferred_element_type=F32)


def _dot_hi(a, b):
    return jnp.dot(a, b, preferred_element_type=F32, precision=HI)


def _dot_nt(a, b):
    return lax.dot_general(a, b, (((1,), (1,)), ((), ())), preferred_element_type=F32)


def _dot_tn(a, b, precision=None):
    return lax.dot_general(a, b, (((0,), (0,)), ((), ())), preferred_element_type=F32, precision=precision)


def _iota(shape, dim):
    return lax.broadcasted_iota(jnp.int32, shape, dim)


def _sigmoid(x):
    return jax.nn.sigmoid(x)


def _silu(x):
    return x * jax.nn.sigmoid(x)


def _softplus(x):
    return jnp.maximum(x, 0.0) + jnp.log1p(jnp.exp(-jnp.abs(x)))


def _log_sigmoid(x):
    return jnp.minimum(x, 0.0) - jnp.log1p(jnp.exp(-jnp.abs(x)))


def _params(sem):
    return pltpu.CompilerParams(dimension_semantics=sem, vmem_limit_bytes=VMEM_LIMIT)


def _chunk_start(c, c_len):
    return c * c_len if isinstance(c, int) else pl.multiple_of(c * c_len, c_len)


def _for_chunks(n_chunks, body, unroll=False):
    if n_chunks == 1 or unroll:
        for c in range(n_chunks):
            body(c, 0)
    else:
        lax.fori_loop(0, n_chunks, body, 0)


def _tile(n, pref):
    t = min(n, pref)
    assert n % t == 0, (n, pref)
    return t


def _ada_kernel(c_ref, w_ref, b_ref, o_ref):
    sc = _silu(c_ref[...]).astype(BF16)
    o_ref[...] = _dot(sc, w_ref[...].astype(BF16)) + b_ref[...]


def _ada(c_all, w_ada, b_ada):
    rows = c_all.shape[0]
    tn = 768
    return pl.pallas_call(
        _ada_kernel,
        out_shape=jax.ShapeDtypeStruct((DEPTH, rows, 6 * D), F32),
        grid=(DEPTH, 6 * D // tn),
        in_specs=[pl.BlockSpec((rows, D), lambda l, n: (0, 0)),
                  pl.BlockSpec((None, D, tn), lambda l, n: (l, 0, n)),
                  pl.BlockSpec((None, 1, tn), lambda l, n: (l, 0, n))],
        out_specs=pl.BlockSpec((None, rows, tn), lambda l, n: (l, 0, n)),
        compiler_params=_params(("parallel", "parallel")),
        name="ada",
    )(c_all, w_ada, b_ada.reshape(DEPTH, 1, 6 * D))


def _mod_spec(mod, tm, k):
    if mod.shape[1] == 1:
        return pl.BlockSpec((None, 1, D), lambda b, i, *_: (b, 0, k))
    return pl.BlockSpec((None, tm, D), lambda b, i, *_: (b, i, k))


def _in_kernel(x_ref, sh_ref, sc_ref, g_ref, w_ref, z_ref, h_scr):
    @pl.when(pl.program_id(2) == 0)
    def _():
        x = x_ref[...]
        r = lax.rsqrt(jnp.mean(x * x, axis=-1, keepdims=True) + EPS)
        h = (x * r) * g_ref[...] * (1.0 + sc_ref[...]) + sh_ref[...]
        h_scr[...] = h.astype(BF16)

    z_ref[...] = _dot(h_scr[...], w_ref[...])


def _in_proj(x, mod, gain, w_packed):
    bg, lg, _ = x.shape
    tm = _tile(lg, 512)
    tn = 1280
    return pl.pallas_call(
        _in_kernel,
        out_shape=jax.ShapeDtypeStruct((bg, lg, NP), F32),
        grid=(bg, lg // tm, NP // tn),
        in_specs=[pl.BlockSpec((None, tm, D), lambda b, i, n: (b, i, 0)),
                  _mod_spec(mod, tm, 0), _mod_spec(mod, tm, 1),
                  pl.BlockSpec((1, D), lambda b, i, n: (0, 0)),
                  pl.BlockSpec((D, tn), lambda b, i, n: (0, n))],
        out_specs=pl.BlockSpec((None, tm, tn), lambda b, i, n: (b, i, n)),
        scratch_shapes=[pltpu.VMEM((tm, D), BF16)],
        compiler_params=_params(("parallel", "parallel", "arbitrary")),
        name="in_proj",
    )(x, mod, mod, gain.reshape(1, D), w_packed)


def _hgrn_kernel(fa_ref, qa_ref, ia_ref, ga_ref, lbl_ref, nw_ref, s0_ref, o_ref, sout_ref, s_scr,
                 *, layer, chunk, n_chunks, l_valid, l_padded):
    i = pl.program_id(2)
    c_len = chunk

    @pl.when(i == 0)
    def _():
        s_scr[...] = s0_ref[...]

    lg = lbl_ref[...]
    e = jnp.exp(lg - jnp.max(lg, axis=0, keepdims=True))
    p = e / jnp.sum(e, axis=0, keepdims=True)
    cs = p[0:1]
    for j in range(1, layer + 1):
        cs = cs + p[j:j + 1]
    lb = jnp.maximum(cs - p[0:1], 0.0)
    log_lb = jnp.log(lb)
    log1m_lb = jnp.log1p(-lb)

    sub = min(HGRN_SUB, c_len)
    tri = (_iota((c_len, c_len), 1) <= _iota((c_len, c_len), 0)).astype(F32)
    lane = _iota((sub, c_len), 1)
    row1 = _iota((c_len, 1), 0)
    ones_cv = jnp.ones((c_len, HA_DV), F32)

    def chunk_body(c, carry):
        for hh in range(HGRN_HEADS_PER_STEP):
            head_chunk(c, hh)
        return carry

    def head_chunk(c, hh):
        r = _chunk_start(c, c_len)
        hl = slice(hh * HA_DK, (hh + 1) * HA_DK)
        fa = fa_ref[pl.ds(r, c_len), hl]
        qa = qa_ref[pl.ds(r, c_len), hl]
        v = ia_ref[pl.ds(r, c_len), hl]
        ga = ga_ref[pl.ds(r, c_len), hl]

        b_ = log1m_lb[:, hl] + _log_sigmoid(fa)
        log_f = jnp.maximum(log_lb[:, hl], b_) + jnp.log1p(jnp.exp(-jnp.abs(log_lb[:, hl] - b_)))
        k = (1.0 - lb[:, hl]) * _sigmoid(-fa)
        if l_valid < l_padded:
            valid = (i * (n_chunks * c_len) + r + row1) < l_valid
            log_f = jnp.where(valid, log_f, 0.0)
            k = jnp.where(valid, k, 0.0)
        q = _silu(qa)
        a_cum = _dot_hi(tri, log_f)

        blocks = []
        for bi in range(c_len // sub):
            lo, hi = bi * sub, (bi + 1) * sub
            if bi == 0:
                att_b = jnp.zeros((sub, c_len), F32)
            else:
                a_ref = a_cum[lo - 1:lo]
                qs = q[lo:hi] * jnp.exp(a_cum[lo:hi] - a_ref)
                ks = jnp.where(row1 < lo, k * jnp.exp(jnp.minimum(a_ref - a_cum, 0.0)), 0.0)
                att_b = _dot_nt(qs.astype(BF16), ks.astype(BF16))
            for s in range(lo, hi):
                r0 = (s // SUBLANES) * SUBLANES
                rel = a_cum[r0:hi] - a_cum[s:s + 1]
                dec = jnp.exp(jnp.where(row1[r0:hi] >= s, rel, NEG_INF))
                col = jnp.sum(q[r0:hi] * k[s:s + 1] * dec, axis=-1, keepdims=True)
                if r0 > lo:
                    col = jnp.concatenate([jnp.zeros((r0 - lo, 1), F32), col], axis=0)
                att_b = jnp.where(lane == s, col, att_b)
            blocks.append(att_b)
        att = blocks[0] if len(blocks) == 1 else jnp.concatenate(blocks, axis=0)

        s_prev = s_scr[hh]
        qd = q * jnp.exp(a_cum)
        o = _dot(qd.astype(BF16), s_prev.astype(BF16)) + _dot(att.astype(BF16), v.astype(BF16))
        a_last = a_cum[c_len - 1:c_len]
        kd = k * jnp.exp(a_last - a_cum)
        dec_s = jnp.exp(_dot_tn(log_f, ones_cv, precision=HI))
        s_scr[hh] = dec_s * s_prev + _dot_tn(kd.astype(BF16), v.astype(BF16))

        rr = lax.rsqrt(jnp.mean(o * o, axis=-1, keepdims=True) + EPS)
        o_ref[pl.ds(r, c_len), hl] = ((o * rr) * nw_ref[...] * _silu(ga)).astype(o_ref.dtype)

    _for_chunks(n_chunks, chunk_body, unroll=True)

    @pl.when(i == pl.num_programs(2) - 1)
    def _():
        sout_ref[...] = s_scr[...]


def _hgrn(z3, lb_logits, norm_w, s0, *, layer, chunk, l_valid):
    bs, lp, _ = z3.shape
    tb = _tile(lp, 4 * chunk)
    kern = functools.partial(_hgrn_kernel, layer=layer, chunk=chunk, n_chunks=tb // chunk,
                             l_valid=l_valid, l_padded=lp)

    hg = HGRN_HEADS_PER_STEP
    wide = hg * HA_DK

    def col(off):
        return pl.BlockSpec((None, tb, wide), lambda b, h, i: (b, i, off // wide + h))

    return pl.pallas_call(
        kern,
        out_shape=(jax.ShapeDtypeStruct((bs, lp, A_W), BF16),
                   jax.ShapeDtypeStruct((bs, HA_H, HA_DK, HA_DV), F32)),
        grid=(bs, HA_H // hg, lp // tb),
        in_specs=[col(OFF_FA), col(OFF_QA), col(OFF_IA), col(OFF_GA),
                  pl.BlockSpec((DEPTH, wide), lambda b, h, i: (0, h)),
                  pl.BlockSpec((1, HA_DV), lambda b, h, i: (0, 0)),
                  pl.BlockSpec((None, hg, HA_DK, HA_DV), lambda b, h, i: (b, h, 0, 0))],
        out_specs=(pl.BlockSpec((None, tb, wide), lambda b, h, i: (b, i, h)),
                   pl.BlockSpec((None, hg, HA_DK, HA_DV), lambda b, h, i: (b, h, 0, 0))),
        scratch_shapes=[pltpu.VMEM((hg, HA_DK, HA_DV), F32)],
        compiler_params=_params(("parallel", "parallel", "arbitrary")),
        name="hgrn2",
    )(z3, z3, z3, z3, lb_logits, norm_w.reshape(1, HA_DV), s0)


def _gdn_kernel(q_ref, k_ref, v_ref, zc_ref, ab_ref, cw_ref, alog_ref, dt_ref, nw_ref, conv0_ref, s0_ref,
                o_ref, sout_ref, s_scr, prev_scr, act_scr, gb_scr, x_scr,
                *, chunk, n_chunks, l_valid, l_padded):
    i = pl.program_id(1)
    c_len = chunk
    tb = n_chunks * c_len

    @pl.when(i == 0)
    def _():
        s_scr[...] = s0_ref[...]
        prev_scr[...] = conv0_ref[...]

    row8 = _iota((SUBLANES, C_K), 0)

    def conv(x, prev, w):
        y = x * w[CONV_W - 1:CONV_W]
        for j in range(1, CONV_W):
            xr = pltpu.roll(x, j, 0)
            head = jnp.where(row8 < j, pltpu.roll(prev, j, 0), xr[:SUBLANES])
            xs = head if tb == SUBLANES else jnp.concatenate([head, xr[SUBLANES:]], axis=0)
            y = y + xs * w[CONV_W - 1 - j:CONV_W - j]
        return _silu(y)

    for n, ref in enumerate((q_ref, k_ref, v_ref)):
        x = ref[...]
        lo, hi = n * C_K, (n + 1) * C_K
        y = conv(x, prev_scr[:, lo:hi], cw_ref[:, lo:hi])
        for h in range(HC_H):
            act_scr[n * HC_H + h] = y[:, h * HC_DK:(h + 1) * HC_DK]
        prev_scr[:, lo:hi] = x[tb - SUBLANES:]

    ab = ab_ref[...]
    g_all = -jnp.exp(alog_ref[...]) * _softplus(ab + dt_ref[...])
    b_all = _sigmoid(ab)
    if l_valid < l_padded:
        valid = (i * tb + _iota((tb, 1), 0)) < l_valid
        g_all = jnp.where(valid, g_all, 0.0)
        b_all = jnp.where(valid, b_all, 0.0)
    gb_scr[:, 0:LANES] = g_all
    gb_scr[:, LANES:2 * LANES] = b_all

    ii = _iota((c_len, c_len), 0)
    jj = _iota((c_len, c_len), 1)
    tri = (jj <= ii).astype(F32)
    tri_u = (ii <= jj).astype(F32)

    def chunk_body(c, carry):
        r = _chunk_start(c, c_len)
        heads = []
        for h in range(HC_H):
            lo, hi = h * HC_DK, (h + 1) * HC_DK
            qh = act_scr[h, pl.ds(r, c_len), :]
            kh = act_scr[HC_H + h, pl.ds(r, c_len), :]
            vh = act_scr[2 * HC_H + h, pl.ds(r, c_len), :]
            qh = qh * lax.rsqrt(jnp.sum(qh * qh, axis=-1, keepdims=True) + EPS) * (HC_DK ** -0.5)
            kh = kh * lax.rsqrt(jnp.sum(kh * kh, axis=-1, keepdims=True) + EPS)
            g_col = gb_scr[pl.ds(r, c_len), h:h + 1]
            b_col = gb_scr[pl.ds(r, c_len), LANES + HC_H + h:LANES + HC_H + h + 1]
            g_b = jnp.broadcast_to(g_col, (c_len, LANES))
            g_cum = _dot_hi(tri, g_b)
            g_row = _dot_tn(g_b, tri_u, precision=HI)[:c_len]
            rel = g_cum[:, :c_len] - g_row
            d_causal = jnp.exp(jnp.where(jj <= ii, rel, NEG_INF))
            d_strict_t = jnp.exp(jnp.where(jj > ii, -rel, NEG_INF))
            kb = (b_col * kh).astype(BF16)
            khb = kh.astype(BF16)
            l_t = _dot_nt(khb, kb) * d_strict_t
            qk = _dot_nt(qh.astype(BF16), khb) * d_causal
            e_g = jnp.exp(g_cum)
            x_scr[h, 0] = b_col * e_g * kh
            x_scr[h, 1] = b_col * vh
            heads.append((qh, kh, l_t, qk, e_g, g_cum))

        for t in range(1, c_len):
            r1 = ((t + SUBLANES - 1) // SUBLANES) * SUBLANES
            for h in range(HC_H):
                col = heads[h][2][0:r1, t:t + 1]
                for part in range(2):
                    contrib = jnp.sum(x_scr[h, part, 0:r1, :] * col, axis=0, keepdims=True)
                    x_scr[h, part, t:t + 1, :] = x_scr[h, part, t:t + 1, :] - contrib

        for h in range(HC_H):
            qh, kh, _, qk, e_g, g_cum = heads[h]
            lo, hi = h * HC_DV, (h + 1) * HC_DV
            s_prev = s_scr[h]
            s_b = s_prev.astype(BF16)
            u = x_scr[h, 1] - _dot(x_scr[h, 0].astype(BF16), s_b)
            u_b = u.astype(BF16)
            o = e_g * _dot(qh.astype(BF16), s_b) + _dot(qk.astype(BF16), u_b)
            g_last = g_cum[c_len - 1:c_len]
            kd = kh * jnp.exp(g_last - g_cum)
            s_scr[h] = jnp.exp(g_last) * s_prev + _dot_tn(kd.astype(BF16), u_b)
            rr = lax.rsqrt(jnp.mean(o * o, axis=-1, keepdims=True) + EPS)
            zc = zc_ref[pl.ds(r, c_len), lo:hi]
            o_ref[pl.ds(r, c_len), lo:hi] = ((o * rr) * nw_ref[...] * _silu(zc)).astype(o_ref.dtype)
        return carry

    _for_chunks(n_chunks, chunk_body, unroll=True)

    @pl.when(i == pl.num_programs(1) - 1)
    def _():
        sout_ref[...] = s_scr[...]


def _gdn(z3, conv_w, a_log, dt_bias, norm_w, conv0, s0, *, chunk, l_valid):
    bs, lp, _ = z3.shape
    tb = _tile(lp, 4 * chunk)
    kern = functools.partial(_gdn_kernel, chunk=chunk, n_chunks=tb // chunk, l_valid=l_valid, l_padded=lp)
    pad = jnp.zeros((LANES - HC_H,), F32)
    alog_row = jnp.concatenate([a_log, pad]).reshape(1, LANES)
    dt_row = jnp.concatenate([dt_bias, pad]).reshape(1, LANES)

    def wide(off):
        return pl.BlockSpec((None, tb, C_K), lambda b, i: (b, i, off // C_K))

    return pl.pallas_call(
        kern,
        out_shape=(jax.ShapeDtypeStruct((bs, lp, C_K), BF16),
                   jax.ShapeDtypeStruct((bs, HC_H, HC_DK, HC_DV), F32)),
        grid=(bs, lp // tb),
        in_specs=[wide(OFF_QC), wide(OFF_KC), wide(OFF_VC), wide(OFF_ZC),
                  pl.BlockSpec((None, tb, LANES), lambda b, i: (b, i, OFF_AB // LANES)),
                  pl.BlockSpec((CONV_W, CONV_CH), lambda b, i: (0, 0)),
                  pl.BlockSpec((1, LANES), lambda b, i: (0, 0)),
                  pl.BlockSpec((1, LANES), lambda b, i: (0, 0)),
                  pl.BlockSpec((1, HC_DV), lambda b, i: (0, 0)),
                  pl.BlockSpec((None, SUBLANES, CONV_CH), lambda b, i: (b, 0, 0)),
                  pl.BlockSpec((None, HC_H, HC_DK, HC_DV), lambda b, i: (b, 0, 0, 0))],
        out_specs=(pl.BlockSpec((None, tb, C_K), lambda b, i: (b, i, 0)),
                   pl.BlockSpec((None, HC_H, HC_DK, HC_DV), lambda b, i: (b, 0, 0, 0))),
        scratch_shapes=[pltpu.VMEM((HC_H, HC_DK, HC_DV), F32),
                        pltpu.VMEM((SUBLANES, CONV_CH), F32),
                        pltpu.VMEM((3 * HC_H, tb, HC_DK), F32),
                        pltpu.VMEM((tb, 2 * LANES), F32),
                        pltpu.VMEM((HC_H, 2, chunk, HC_DK), F32)],
        compiler_params=_params(("parallel", "arbitrary")),
        name="gdn",
    )(z3, z3, z3, z3, z3, conv_w, alog_row, dt_row, norm_w.reshape(1, HC_DV), conv0, s0)


def _foxprep_kernel(q_ref, k_ref, v_ref, fb_ref, gq_ref, gk_ref, bf_ref, bd_ref,
                    qo_ref, ko_ref, kbo_ref, vbo_ref, lfo_ref, *rest, cumsum):
    bd = bd_ref[...]

    def head_rms(x, g):
        x2 = x * x
        hi = x2.astype(BF16)
        lo = (x2 - hi.astype(F32)).astype(BF16)
        ss = _dot(hi, bd) + _dot(lo, bd)
        return x * lax.rsqrt(ss * (1.0 / HB_D) + EPS) * g

    qn = head_rms(q_ref[...], gq_ref[...])
    kn = head_rms(k_ref[...], gk_ref[...])
    qo_ref[...] = (qn * (HB_D ** -0.5)).astype(BF16)
    ko_ref[...] = kn
    kbo_ref[...] = kn.astype(BF16)
    vbo_ref[...] = v_ref[...].astype(BF16)
    lf = _log_sigmoid(fb_ref[...] + bf_ref[...])
    lf = jnp.where(_iota(lf.shape, 1) < HB_H, lf, 0.0)
    lfo_ref[...] = lf
    if cumsum:
        ft_ref, carry = rest
        tm = lf.shape[0]

        @pl.when(pl.program_id(1) == 0)
        def _():
            carry[...] = jnp.zeros_like(carry)

        tri = (_iota((tm, tm), 1) <= _iota((tm, tm), 0)).astype(F32)
        f_cum = _dot_hi(tri, lf) + carry[...]
        carry[...] = f_cum[tm - 1:tm]
        ft_ref[...] = f_cum.T[:HB_H]


def _fox_prep(z3, gq, gk, b_f, *, cumsum):
    bg, lg, _ = z3.shape
    tm = _tile(lg, 256)
    gq_row = jnp.tile(gq, HB_H).reshape(1, B_W)
    gk_row = jnp.tile(gk, HB_H).reshape(1, B_W)
    bf_row = jnp.concatenate([b_f, jnp.zeros((LANES - HB_H,), F32)]).reshape(1, LANES)
    seg = jnp.arange(B_W) // HB_D
    bd = (seg[:, None] == seg[None, :]).astype(BF16)

    def wide(off):
        return pl.BlockSpec((None, tm, B_W), lambda b, i: (b, i, off // B_W))

    tok = pl.BlockSpec((None, tm, B_W), lambda b, i: (b, i, 0))
    out_shape = [jax.ShapeDtypeStruct((bg, lg, B_W), BF16), jax.ShapeDtypeStruct((bg, lg, B_W), F32),
                 jax.ShapeDtypeStruct((bg, lg, B_W), BF16), jax.ShapeDtypeStruct((bg, lg, B_W), BF16),
                 jax.ShapeDtypeStruct((bg, lg, LANES), F32)]
    out_specs = [tok, tok, tok, tok, pl.BlockSpec((None, tm, LANES), lambda b, i: (b, i, 0))]
    scratch = []
    if cumsum:
        out_shape.append(jax.ShapeDtypeStruct((bg, HB_H, lg), F32))
        out_specs.append(pl.BlockSpec((None, HB_H, tm), lambda b, i: (b, 0, i)))
        scratch.append(pltpu.VMEM((1, LANES), F32))
    return pl.pallas_call(
        functools.partial(_foxprep_kernel, cumsum=cumsum),
        out_shape=tuple(out_shape),
        grid=(bg, lg // tm),
        in_specs=[wide(OFF_QB), wide(OFF_KB), wide(OFF_VB),
                  pl.BlockSpec((None, tm, LANES), lambda b, i: (b, i, OFF_FB // LANES)),
                  pl.BlockSpec((1, B_W), lambda b, i: (0, 0)),
                  pl.BlockSpec((1, B_W), lambda b, i: (0, 0)),
                  pl.BlockSpec((1, LANES), lambda b, i: (0, 0)),
                  pl.BlockSpec((B_W, B_W), lambda b, i: (0, 0))],
        out_specs=tuple(out_specs),
        scratch_shapes=scratch,
        compiler_params=_params(("parallel", "arbitrary")),
        name="fox_prep",
    )(z3, z3, z3, z3, gq_row, gk_row, bf_row, bd)


def _foxattn_kernel(q_ref, k_ref, v_ref, f_ref, o_ref, *, tq):
    qi = pl.program_id(2)
    q = q_ref[...]
    lane_q = _iota(q.shape, 1)
    q_heads = (jnp.where(lane_q < HB_D, q, jnp.zeros_like(q)), jnp.where(lane_q >= HB_D, q, jnp.zeros_like(q)))
    row = _iota((tq, tq), 0)
    colm = _iota((tq, tq), 1)

    def step(j, carry, masked):
        r = pl.multiple_of(j * tq, tq)
        kj = k_ref[pl.ds(r, tq), :]
        vj = v_ref[pl.ds(r, tq), :]
        fj = f_ref[j]
        out = []
        for hh in range(2):
            m, l, acc = carry[hh]
            s = _dot_nt(q_heads[hh], kj) - fj[hh:hh + 1, :]
            if masked:
                s = jnp.where(colm <= row, s, NEG_INF)
            m_new = jnp.maximum(m, jnp.max(s, axis=-1, keepdims=True))
            alpha = jnp.exp(m - m_new)
            p = jnp.exp(s - m_new)
            l = alpha * l + jnp.sum(p, axis=-1, keepdims=True)
            acc = alpha * acc + _dot(p.astype(BF16), vj)
            out.append((m_new, l, acc))
        return tuple(out)

    init = tuple((jnp.full((tq, 1), -1e30, F32), jnp.zeros((tq, 1), F32), jnp.zeros((tq, LANES), F32))
                 for _ in range(2))
    carry = lax.fori_loop(0, qi, lambda j, c: step(j, c, False), init)
    (_, l0, a0), (_, l1, a1) = step(qi, carry, True)
    o = jnp.where(_iota((tq, LANES), 1) < HB_D, a0 / l0, a1 / l1)
    o_ref[...] = o.astype(o_ref.dtype)


def _fox_attn_prompt(qb, kb, vb, ft):
    bg, s_len, _ = qb.shape
    tq = _tile(s_len, 512)
    nk = s_len // tq
    pairs = HB_H // 2
    f5 = ft.reshape(bg, pairs, 2, nk, tq).transpose(0, 1, 3, 2, 4)
    return pl.pallas_call(
        functools.partial(_foxattn_kernel, tq=tq),
        out_shape=jax.ShapeDtypeStruct((bg, s_len, B_W), BF16),
        grid=(bg, pairs, nk),
        in_specs=[pl.BlockSpec((None, tq, LANES), lambda b, p, i: (b, i, p)),
                  pl.BlockSpec((None, s_len, LANES), lambda b, p, i: (b, 0, p)),
                  pl.BlockSpec((None, s_len, LANES), lambda b, p, i: (b, 0, p)),
                  pl.BlockSpec((None, None, nk, 2, tq), lambda b, p, i: (b, p, 0, 0, 0))],
        out_specs=pl.BlockSpec((None, tq, LANES), lambda b, p, i: (b, i, p)),
        compiler_params=_params(("parallel", "parallel", "arbitrary")),
        name="fox_attn_prompt",
    )(qb, kb, vb, f5)


def _foxsample_kernel(pt_ref, q_ref, kn_ref, vn_ref, lfn_ref, *rest, n_new, ps):
    k_pages = rest[0:ps]
    v_pages = rest[ps:2 * ps]
    f_pages = rest[2 * ps:3 * ps]
    o_ref, q_scr, m_scr, l_scr, acc_scr, fc_scr = rest[3 * ps:]
    g = pl.program_id(1)
    rows = n_new * HB_H
    hmask = (_iota((HB_H, B_W), 1) // HB_D) == _iota((HB_H, B_W), 0)

    @pl.when(g == 0)
    def _():
        for t in range(n_new):
            qt = jnp.broadcast_to(q_ref[t:t + 1, :], (HB_H, B_W))
            q_scr[t * HB_H:(t + 1) * HB_H, :] = jnp.where(hmask, qt, 0.0)
        m_scr[...] = jnp.full_like(m_scr, -1e30)
        l_scr[...] = jnp.zeros_like(l_scr)
        acc_scr[...] = jnp.zeros_like(acc_scr)
        fc_scr[...] = jnp.zeros_like(fc_scr)

    tri_u = (_iota((PAGE, PAGE), 0) <= _iota((PAGE, PAGE), 1)).astype(F32)

    def update(s, v_b):
        m = m_scr[...]
        m_new = jnp.maximum(m, jnp.max(s, axis=-1, keepdims=True))
        alpha = jnp.exp(m - m_new)
        p = jnp.exp(s - m_new)
        l_scr[...] = alpha * l_scr[...] + jnp.sum(p, axis=-1, keepdims=True)
        acc_scr[...] = alpha * acc_scr[...] + _dot_nt(p.astype(BF16), v_b)
        m_scr[...] = m_new

    def cum_forget(lf_parts):
        f_loc = [_dot_hi(lf_t, tri_u) for lf_t in lf_parts]
        off = fc_scr[...]
        out = []
        for fl in f_loc:
            out.append(fl + off)
            off = off + fl[:, PAGE - 1:PAGE]
        fc_scr[...] = off
        return out

    f_cat = jnp.concatenate(cum_forget([f_pages[r][...] for r in range(ps)]), axis=1)
    bias = jnp.concatenate([f_cat] * n_new, axis=0)
    k_cat = jnp.concatenate([k_pages[r][...].astype(BF16) for r in range(ps)], axis=1)
    v_cat = jnp.concatenate([v_pages[r][...].astype(BF16) for r in range(ps)], axis=1)
    q_rows = q_scr[...].astype(BF16)
    update(_dot(q_rows, k_cat) - bias, v_cat)

    @pl.when(g == pl.num_programs(1) - 1)
    def _():
        f_new = cum_forget([lfn_ref[...]])[0]
        s = _dot(q_rows, kn_ref[...].astype(BF16)) - jnp.concatenate([f_new] * n_new, axis=0)
        visible = _iota((rows, PAGE), 1) <= (_iota((rows, PAGE), 0) // HB_H)
        update(jnp.where(visible, s, NEG_INF), vn_ref[...].astype(BF16))
        o = acc_scr[...] / l_scr[...]
        out = jnp.zeros((SUBLANES, B_W), F32)
        out_row = _iota((SUBLANES, B_W), 0)
        for t in range(n_new):
            ot = jnp.where(hmask, o[t * HB_H:(t + 1) * HB_H, :], 0.0)
            out = jnp.where(out_row == t, jnp.sum(ot, axis=0, keepdims=True), out)
        o_ref[...] = out.astype(o_ref.dtype)


def _fox_attn_sample(layer, q_new, k_new, v_new, lf_new, cache_kt, cache_vt, cache_lf_t, page_table):
    db, n_new, _ = q_new.shape
    n_pages = page_table.shape[1]
    n_pool = cache_kt.shape[0] // DEPTH
    ps = min(PAGES_PER_STEP, n_pages)
    assert n_pages % ps == 0 and n_new <= SUBLANES
    qp = jnp.pad(q_new, ((0, 0), (0, SUBLANES - n_new), (0, 0)))
    pad_pos = ((0, 0), (0, 0), (0, PAGE - n_new))
    knp = jnp.pad(k_new.transpose(0, 2, 1), pad_pos)
    vnp = jnp.pad(v_new.transpose(0, 2, 1), pad_pos)
    lfp = jnp.pad(lf_new.transpose(0, 2, 1), pad_pos)
    base = layer * n_pool

    def page_spec(r, rows, cols):
        return pl.BlockSpec((None, rows, cols), lambda b, g, pt: (base + pt[b, g * ps + r], 0, 0))

    in_specs = [pl.BlockSpec((None, SUBLANES, B_W), lambda b, g, pt: (b, 0, 0)),
                pl.BlockSpec((None, B_W, PAGE), lambda b, g, pt: (b, 0, 0)),
                pl.BlockSpec((None, B_W, PAGE), lambda b, g, pt: (b, 0, 0)),
                pl.BlockSpec((None, HB_H, PAGE), lambda b, g, pt: (b, 0, 0))]
    in_specs += [page_spec(r, B_W, PAGE) for r in range(ps)]
    in_specs += [page_spec(r, B_W, PAGE) for r in range(ps)]
    in_specs += [page_spec(r, HB_H, PAGE) for r in range(ps)]
    rows = n_new * HB_H
    return pl.pallas_call(
        functools.partial(_foxsample_kernel, n_new=n_new, ps=ps),
        out_shape=jax.ShapeDtypeStruct((db, SUBLANES, B_W), BF16),
        grid_spec=pltpu.PrefetchScalarGridSpec(
            num_scalar_prefetch=1,
            grid=(db, n_pages // ps),
            in_specs=in_specs,
            out_specs=pl.BlockSpec((None, SUBLANES, B_W), lambda b, g, pt: (b, 0, 0)),
            scratch_shapes=[pltpu.VMEM((rows, B_W), F32), pltpu.VMEM((rows, 1), F32),
                            pltpu.VMEM((rows, 1), F32), pltpu.VMEM((rows, B_W), F32),
                            pltpu.VMEM((HB_H, 1), F32)]),
        compiler_params=_params(("parallel", "arbitrary")),
        name="fox_attn_sample",
    )(page_table, qp, knp, vnp, lfp, *([cache_kt] * ps), *([cache_vt] * ps), *([cache_lf_t] * ps))


def _merge_kernel(oa_ref, ob_ref, oc_ref, ga_ref, gb_ref, gc_ref, x_ref, gt1_ref, sh2_ref, sc2_ref,
                  wa_ref, wb_ref, wc_ref, wo_ref, nf_ref, wq_ref, keys_ref,
                  x1_ref, h2t_ref, st_ref):
    merged = (_sigmoid(ga_ref[...]) * _dot(oa_ref[...], wa_ref[...])
              + _sigmoid(gb_ref[...]) * _dot(ob_ref[...], wb_ref[...])
              + _sigmoid(gc_ref[...]) * _dot(oc_ref[...], wc_ref[...]))
    x1 = x_ref[...] + gt1_ref[...] * _dot(merged.astype(BF16), wo_ref[...])
    x1_ref[...] = x1
    r = lax.rsqrt(jnp.mean(x1 * x1, axis=-1, keepdims=True) + EPS)
    h2 = (x1 * r) * nf_ref[...] * (1.0 + sc2_ref[...]) + sh2_ref[...]
    h2t_ref[...] = h2.T.astype(BF16)
    qb = _dot(h2.astype(BF16), wq_ref[...]).astype(BF16)
    half = P_QD // 2
    for h in range(P_H):
        for p in range(2):
            lo = (h * 2 + p) * half
            st_ref[lo:lo + half, :] = _dot_nt(keys_ref[p], qb[:, lo:lo + half])


def _merge(oa, ob, oc, z3, x, mod, w_a, w_b, w_c, w_o, norm_ffn, w_q, keys):
    bg, lg, _ = x.shape
    tm = _tile(lg, 256)
    nl = lg // tm
    t_all = bg * lg

    def tok(width):
        return pl.BlockSpec((None, tm, width), lambda b, i: (b, i, 0))

    def gate(k):
        return pl.BlockSpec((None, tm, D), lambda b, i: (b, i, OFF_GATE // D + k))

    def full(shape):
        return pl.BlockSpec(shape, lambda b, i: (0,) * len(shape))

    return pl.pallas_call(
        _merge_kernel,
        out_shape=(jax.ShapeDtypeStruct((bg, lg, D), F32),
                   jax.ShapeDtypeStruct((D, t_all), BF16),
                   jax.ShapeDtypeStruct((P_H * P_QD, t_all), F32)),
        grid=(bg, nl),
        in_specs=[tok(A_W), tok(B_W), tok(C_K), gate(0), gate(1), gate(2), tok(D),
                  _mod_spec(mod, tm, 2), _mod_spec(mod, tm, 3), _mod_spec(mod, tm, 4),
                  full((A_W, D)), full((B_W, D)), full((C_K, D)), full((D, D)), full((1, D)),
                  full((D, P_H * P_QD)), full((2, P_NK, P_QD // 2))],
        out_specs=(tok(D),
                   pl.BlockSpec((D, tm), lambda b, i: (0, b * nl + i)),
                   pl.BlockSpec((P_H * P_QD, tm), lambda b, i: (0, b * nl + i))),
        compiler_params=_params(("parallel", "parallel")),
        name="merge",
    )(oa, ob, oc, z3, z3, z3, x, mod, mod, mod, w_a, w_b, w_c, w_o, norm_ffn.reshape(1, D), w_q, keys)


def _topk_kernel(st_ref, o_ref):
    tt = st_ref.shape[1]

    def top_rows(s_ref, lo):
        n_v = P_NK // SUBLANES
        v = [s_ref[lo + k * SUBLANES:lo + (k + 1) * SUBLANES, :] for k in range(n_v)]
        k = 2
        while k <= n_v:
            j = k // 2
            while j >= 1:
                for a in range(n_v):
                    b = a ^ j
                    if b > a:
                        hi, lo_v = jnp.maximum(v[a], v[b]), jnp.minimum(v[a], v[b])
                        v[a], v[b] = (hi, lo_v) if (a & k) == 0 else (lo_v, hi)
                j //= 2
            k *= 2
        rows = []
        for it in range(P_TOPK + 1):
            m = jnp.max(v[0], axis=0, keepdims=True)
            rows.append(m)
            popped = v[0] == m
            for a in range(P_TOPK - it):
                v[a] = jnp.where(popped, v[a + 1] if a + 1 < n_v else NEG_INF, v[a])
        return rows

    rank = _iota((P_TOPK, tt), 0)
    rank8 = _iota((SUBLANES, tt), 0)
    thr_rows, nrm_rows = [], []
    for h in range(P_H):
        lo = h * P_QD
        v1 = top_rows(st_ref, lo)
        v2 = top_rows(st_ref, lo + P_NK)
        v1s = jnp.concatenate(v1[:P_TOPK], axis=0)
        v2s = jnp.concatenate(v2[:P_TOPK], axis=0)
        groups = [jnp.where(rank8 == 0, v1[P_TOPK] + v2[0], jnp.where(rank8 == 1, v1[0] + v2[P_TOPK], NEG_INF))]
        for b in range(3):
            groups.append(jnp.where(rank < P_TOPK // (b + 1), v1s + v2[b], NEG_INF))
        for a in range(4):
            nb = P_TOPK // (a + 1)
            n_rows = P_TOPK if nb > SUBLANES else SUBLANES
            rk = _iota((n_rows, tt), 0)
            ok = jnp.where(rk >= 3, rk, nb) < nb
            groups.append(jnp.where(ok, v2s[:n_rows] + v1[a], NEG_INF))
        m_top = None
        z = None
        for it in range(P_TOPK):
            m = functools.reduce(jnp.maximum, [jnp.max(gp, axis=0, keepdims=True) for gp in groups])
            if it == 0:
                m_top = m
                z = jnp.ones_like(m)
            else:
                z = z + jnp.exp(m - m_top)
            groups = [jnp.where(gp == m, NEG_INF, gp) for gp in groups]
        m_next = functools.reduce(jnp.maximum, [jnp.max(gp, axis=0, keepdims=True) for gp in groups])
        thr_rows.append(0.5 * m + 0.5 * m_next)
        nrm_rows.append(-(m_top + jnp.log(z)))
    o_ref[...] = jnp.concatenate(thr_rows + nrm_rows, axis=0)


def _topk(st):
    t_all = st.shape[1]
    tt = _tile(t_all, 256)
    return pl.pallas_call(
        _topk_kernel,
        out_shape=jax.ShapeDtypeStruct((2 * P_H, t_all), F32),
        grid=(t_all // tt,),
        in_specs=[pl.BlockSpec((P_H * P_QD, tt), lambda t: (0, t))],
        out_specs=pl.BlockSpec((2 * P_H, tt), lambda t: (0, t)),
        compiler_params=_params(("parallel",)),
        name="peer_topk",
    )(st)


def _peer_kernel(st_ref, stat_ref, h2t_ref, u0_ref, u1_ref, vt0_ref, vt1_ref, x1_ref, gt2_ref, o_ref,
                 acc_scr, e1_scr, tau_scr, e2_scr, s2_scr, ht0_scr, ht1_scr, wa0_scr, wa1_scr, *, ti, n_tiles):
    s = pl.program_id(2)
    tt = st_ref.shape[1]

    @pl.when(s == 0)
    def _():
        acc_scr[...] = jnp.zeros_like(acc_scr)
        ht0_scr[...] = jnp.zeros_like(ht0_scr)
        ht1_scr[...] = jnp.zeros_like(ht1_scr)
        wa0_scr[...] = jnp.zeros_like(wa0_scr)
        wa1_scr[...] = jnp.zeros_like(wa1_scr)
        for h in range(P_H):
            lo = h * P_QD
            hr = slice(h * P_NK, (h + 1) * P_NK)
            for lt in range(tt // LANES):
                ls = slice(lt * LANES, (lt + 1) * LANES)
                s1 = st_ref[lo:lo + P_NK, ls]
                s2 = st_ref[lo + P_NK:lo + 2 * P_NK, ls]
                mx2 = jnp.max(s2, axis=0, keepdims=True)
                s2_scr[lt, hr, :] = s2
                e2_scr[lt, hr, :] = jnp.exp(s2 - mx2)
                e1_scr[hr, ls] = 0.5 * jnp.exp(s1 + (stat_ref[P_H + h:P_H + h + 1, ls] + mx2))
                tau_scr[hr, ls] = stat_ref[h:h + 1, ls] - s1

    c0 = math.sqrt(2.0 / math.pi)
    c1 = c0 * 0.044715
    tile_b = jnp.clip(s - 1, 0, n_tiles - 1)

    def stages(ht_w, ht_r, wa_w, wa_r):
        te = ti * P_NK
        n_lt = tt // LANES
        n_p = min(PEER_PIECES, ti * n_lt)
        per_group = ti * n_lt // n_p
        n_split = 2 if n_lt % 2 == 0 else 1
        m_split = n_p // n_split
        ma, mc, nw = te // m_split, D // m_split, n_lt // n_split
        assert per_group >= 1 and (te // 2) % ma == 0 and (D // 2) % mc == 0

        def matmul_pieces(pi):
            mi, ni = pi // n_split, pi % n_split
            lts = range(ni * nw, (ni + 1) * nw)
            u_half = (u0_ref, u1_ref)[mi * ma // (te // 2)]
            a0 = (mi * ma) % (te // 2)
            part_a = _dot(u_half[a0:a0 + ma, :], h2t_ref[:, ni * nw * LANES:(ni + 1) * nw * LANES])
            vt_half = (vt0_ref, vt1_ref)[mi * mc // (D // 2)]
            v0 = (mi * mc) % (D // 2)
            part_c = _dot(vt_half[v0:v0 + mc, :], jnp.concatenate([wa_r[lt] for lt in lts], axis=1))
            for k, lt in enumerate(lts):
                ht_w[lt, mi * ma:(mi + 1) * ma, :] = part_a[:, k * LANES:(k + 1) * LANES]
                acc_scr[lt, mi * mc:(mi + 1) * mc, :] += part_c[:, k * LANES:(k + 1) * LANES]

        for ii in range(ti):
            i_row = tile_b * ti + ii
            tau_rows = [tau_scr[pl.ds(h * P_NK + i_row, 1), :] for h in range(P_H)]
            e1_rows = [e1_scr[pl.ds(h * P_NK + i_row, 1), :] for h in range(P_H)]
            for lt in range(n_lt):
                if (ii * n_lt + lt) % per_group == 0:
                    matmul_pieces((ii * n_lt + lt) // per_group)
                ls = slice(lt * LANES, (lt + 1) * LANES)
                tau_b = [jnp.broadcast_to(tau_rows[h][:, ls], (PEER_JB, LANES)) for h in range(P_H)]
                e1_b = [jnp.broadcast_to(e1_rows[h][:, ls], (PEER_JB, LANES)) for h in range(P_H)]
                for jb in range(P_NK // PEER_JB):
                    j0 = jb * PEER_JB
                    hs = ht_r[lt, ii * P_NK + j0:ii * P_NK + j0 + PEER_JB, :]
                    act = hs * (1.0 + jnp.tanh(hs * (c0 + c1 * (hs * hs))))
                    w = None
                    for h in range(P_H):
                        jr = slice(h * P_NK + j0, h * P_NK + j0 + PEER_JB)
                        wh = jnp.where(s2_scr[lt, jr, :] >= tau_b[h], e2_scr[lt, jr, :] * e1_b[h], 0.0)
                        w = wh if w is None else w + wh
                    wa_w[lt, ii * P_NK + j0:ii * P_NK + j0 + PEER_JB, :] = (w * act).astype(BF16)

    @pl.when(s % 2 == 0)
    def _():
        stages(ht0_scr, ht1_scr, wa1_scr, wa0_scr)

    @pl.when(s % 2 == 1)
    def _():
        stages(ht1_scr, ht0_scr, wa0_scr, wa1_scr)

    @pl.when(s == pl.num_programs(2) - 1)
    def _():
        for lt in range(tt // LANES):
            rows = slice(lt * LANES, (lt + 1) * LANES)
            gt2 = gt2_ref[rows, :] if gt2_ref.shape[0] == tt else gt2_ref[0:1, :]
            o_ref[rows, :] = x1_ref[rows, :] + gt2 * acc_scr[lt].T


def _peer(st, stats, h2t, u_b, vt_b, x1, mod):
    bg, lg, _ = x1.shape
    tt = _tile(lg, 512)
    nl = lg // tt
    ti = PEER_TI
    te = ti * P_NK
    assert vt_b.shape == (N_EXP // te, D, te)
    n_tiles = N_EXP // te
    if mod.shape[1] == 1:
        mod = jnp.broadcast_to(mod, (bg, SUBLANES, 6 * D))
        gate_spec = pl.BlockSpec((None, SUBLANES, D), lambda b, i, s: (b, 0, 5))
    else:
        gate_spec = _mod_spec(mod, tt, 5)
    return pl.pallas_call(
        functools.partial(_peer_kernel, ti=ti, n_tiles=n_tiles),
        out_shape=jax.ShapeDtypeStruct((bg, lg, D), F32),
        grid=(bg, nl, n_tiles + 2),
        in_specs=[pl.BlockSpec((P_H * P_QD, tt), lambda b, i, s: (0, b * nl + i)),
                  pl.BlockSpec((2 * P_H, tt), lambda b, i, s: (0, b * nl + i)),
                  pl.BlockSpec((D, tt), lambda b, i, s: (0, b * nl + i)),
                  pl.BlockSpec((te // 2, D), lambda b, i, s: (2 * jnp.minimum(s, n_tiles - 1), 0)),
                  pl.BlockSpec((te // 2, D), lambda b, i, s: (2 * jnp.minimum(s, n_tiles - 1) + 1, 0)),
                  pl.BlockSpec((None, D // 2, te), lambda b, i, s: (jnp.clip(s - 2, 0, n_tiles - 1), 0, 0)),
                  pl.BlockSpec((None, D // 2, te), lambda b, i, s: (jnp.clip(s - 2, 0, n_tiles - 1), 1, 0)),
                  pl.BlockSpec((None, tt, D), lambda b, i, s: (b, i, 0)),
                  gate_spec],
        out_specs=pl.BlockSpec((None, tt, D), lambda b, i, s: (b, i, 0)),
        scratch_shapes=[pltpu.VMEM((tt // LANES, D, LANES), F32),
                        pltpu.VMEM((P_H * P_NK, tt), F32), pltpu.VMEM((P_H * P_NK, tt), F32),
                        pltpu.VMEM((tt // LANES, P_H * P_NK, LANES), F32),
                        pltpu.VMEM((tt // LANES, P_H * P_NK, LANES), F32),
                        pltpu.VMEM((tt // LANES, te, LANES), F32), pltpu.VMEM((tt // LANES, te, LANES), F32),
                        pltpu.VMEM((tt // LANES, te, LANES), BF16), pltpu.VMEM((tt // LANES, te, LANES), BF16)],
        compiler_params=_params(("parallel", "parallel", "arbitrary")),
        name="peer_experts",
    )(st, stats, h2t, u_b, u_b, vt_b, vt_b, x1, mod)


def _pack_w_in(w):
    pad = jnp.zeros((D, LANES - 8), w.dtype)
    o_fb = 2 * A_W + 2 * A_W + 3 * B_W
    o_c = o_fb + HB_H
    o_ab = o_c + 3 * C_K
    o_zc = o_ab + 2 * HC_H
    o_gate = o_zc + C_K
    packed = jnp.concatenate([
        w[:, 0:4 * A_W], w[:, o_gate:o_gate + 3 * D], w[:, 4 * A_W:o_fb], w[:, o_c:o_ab],
        w[:, o_zc:o_gate], w[:, o_fb:o_c], pad, w[:, o_ab:o_zc], pad], axis=1)
    assert packed.shape[1] == NP
    return packed.astype(BF16)


def _layer(l, x, mod, w, sample):
    bg, lg, _ = x.shape
    z = _in_proj(x, mod, w["norm_mix"][l], w["w_in"][l])
    if sample is None:
        z3 = z
        bs, l_seq = bg, lg
        chunk = LIN_C
        hgrn0 = jnp.zeros((bs, HA_H, HA_DK, HA_DV), F32)
        gdn0 = jnp.zeros((bs, HC_H, HC_DK, HC_DV), F32)
        conv0 = jnp.zeros((bs, SUBLANES, CONV_CH), F32)
    else:
        bs, l_seq = sample["db"], sample["t"]
        chunk = SUBLANES
        z3 = jnp.pad(z.reshape(bs, l_seq, NP), ((0, 0), (0, SUBLANES - l_seq), (0, 0)))
        hgrn0 = sample["state_hgrn"][l]
        gdn0 = sample["state_gdn"][l]
        conv0 = jnp.pad(sample["state_conv"][l], ((0, 0), (SUBLANES - (CONV_W - 1), 0), (0, 0)))

    o_a, s_hgrn = _hgrn(z3, w["hgrn_lb_logits"], w["hgrn_norm"][l], hgrn0, layer=l, chunk=chunk, l_valid=l_seq)
    o_c, s_gdn = _gdn(z3, w["gdn_conv_w"][l], w["gdn_a_log"][l], w["gdn_dt_bias"][l], w["gdn_norm"][l],
                      conv0, gdn0, chunk=min(chunk, GDN_C), l_valid=l_seq)
    zseq = z.reshape(bs, l_seq, NP)
    conv_new = zseq[:, l_seq - (CONV_W - 1):, OFF_QC:OFF_QC + CONV_CH]

    prep = _fox_prep(z, w["fox_q_norm"][l], w["fox_k_norm"][l], w["fox_b_f"][l], cumsum=sample is None)
    qn, kn, knb, vnb, lf = prep[:5]
    v_b = z[:, :, OFF_VB:OFF_VB + B_W]
    if sample is None:
        o_b = _fox_attn_prompt(qn, knb, vnb, prep[5])
    else:
        o_b = _fox_attn_sample(
            l, qn.astype(F32).reshape(bs, l_seq, B_W), kn.reshape(bs, l_seq, B_W), v_b.reshape(bs, l_seq, B_W),
            lf[0, :, :HB_H].reshape(bs, l_seq, HB_H), sample["cache_kt"], sample["cache_vt"], sample["cache_lf_t"],
            sample["page_table"])
        o_b = o_b[:, :l_seq].reshape(bg, lg, B_W)
        o_a = o_a[:, :l_seq].reshape(bg, lg, A_W)
        o_c = o_c[:, :l_seq].reshape(bg, lg, C_K)

    x1, h2t, st = _merge(o_a, o_b, o_c, z, x, mod, w["w_br_a"][l], w["w_br_b"][l], w["w_br_c"][l],
                         w["w_out"][l], w["norm_ffn"][l], w["peer_w_q"][l], w["peer_keys"][l])
    stats = _topk(st)
    x2 = _peer(st, stats, h2t, w["peer_u"][l], w["peer_vt"][l], x1, mod)

    k_leaf = kn.reshape(bs, l_seq, HB_H, HB_D)
    v_leaf = v_b.reshape(bs, l_seq, HB_H, HB_D)
    lf_leaf = lf[:, :, :HB_H].reshape(bs, l_seq, HB_H)
    return x2, (k_leaf, v_leaf, lf_leaf, s_hgrn, s_gdn, conv_new)


def _trunk(x, mods, w, sample):
    leaves = [[] for _ in range(6)]
    for l in range(DEPTH):
        x, st = _layer(l, x, mods[l], w, sample)
        for lst, s in zip(leaves, st):
            lst.append(s)
    return x, [jnp.stack(v) for v in leaves]


def kernel(x_prompt, x_sample, c_prompt, c_sample, cache_fox_k, cache_fox_v, cache_fox_logf, page_table,
           state_hgrn, state_gdn, state_gdn_conv, w_ada, b_ada, norm_mix, norm_ffn, w_in, hgrn_lb_logits,
           hgrn_norm, fox_b_f, fox_q_norm, fox_k_norm, gdn_conv_w, gdn_a_log, gdn_dt_bias, gdn_norm,
           w_br_a, w_br_b, w_br_c, w_out, peer_w_q, peer_keys, peer_u, peer_v):
    bp = x_prompt.shape[0]
    db, t_new, _ = x_sample.shape
    n_pool = cache_fox_k.shape[1]

    w = {
        "norm_mix": norm_mix, "norm_ffn": norm_ffn, "hgrn_lb_logits": hgrn_lb_logits, "hgrn_norm": hgrn_norm,
        "fox_b_f": fox_b_f, "fox_q_norm": fox_q_norm, "fox_k_norm": fox_k_norm, "gdn_conv_w": gdn_conv_w,
        "gdn_a_log": gdn_a_log, "gdn_dt_bias": gdn_dt_bias, "gdn_norm": gdn_norm,
        "w_in": [_pack_w_in(w_in[l]) for l in range(DEPTH)],
        "w_br_a": w_br_a.astype(BF16), "w_br_b": w_br_b.astype(BF16), "w_br_c": w_br_c.astype(BF16),
        "w_out": w_out.astype(BF16), "peer_w_q": peer_w_q.astype(BF16), "peer_keys": peer_keys.astype(BF16),
        "peer_u": peer_u.astype(BF16),
        "peer_vt": peer_v.astype(BF16).reshape(DEPTH, N_EXP // (PEER_TI * P_NK), PEER_TI * P_NK, D).transpose(0, 1, 3, 2),
    }

    n_c = bp + db
    c_all = jnp.pad(jnp.concatenate([c_prompt, c_sample], axis=0), ((0, (-n_c) % SUBLANES), (0, 0)))
    mod = _ada(c_all, w_ada, b_ada)
    mods_p = [mod[l, :bp].reshape(bp, 1, 6 * D) for l in range(DEPTH)]
    mods_s = [jnp.repeat(mod[l, bp:n_c], t_new, axis=0).reshape(1, db * t_new, 6 * D) for l in range(DEPTH)]

    sample = {
        "db": db, "t": t_new, "page_table": page_table,
        "state_hgrn": state_hgrn, "state_gdn": state_gdn, "state_conv": state_gdn_conv,
        "cache_kt": cache_fox_k.transpose(0, 1, 3, 4, 2).reshape(DEPTH * n_pool, B_W, PAGE),
        "cache_vt": cache_fox_v.transpose(0, 1, 3, 4, 2).reshape(DEPTH * n_pool, B_W, PAGE),
        "cache_lf_t": cache_fox_logf.transpose(0, 1, 3, 2).reshape(DEPTH * n_pool, HB_H, PAGE),
    }

    y_p, leaves_p = _trunk(x_prompt, mods_p, w, None)
    y_s, leaves_s = _trunk(x_sample.reshape(1, db * t_new, D), mods_s, w, sample)
    return (y_p, y_s.reshape(db, t_new, D), *leaves_p, *leaves_s)
```

```python
import functools
import math

import jax
import jax.numpy as jnp
from jax import lax
from jax.experimental import pallas as pl
from jax.experimental.pallas import tpu as pltpu

F32 = jnp.float32
BF16 = jnp.bfloat16
HI = lax.Precision.HIGHEST
NEG_INF = float("-inf")

D = 1024
DEPTH = 2
HA_H, HA_DK, HA_DV = 4, 128, 128
HB_H, HB_D = 8, 64
HC_H, HC_DK, HC_DV = 4, 128, 128
CONV_W = 4
P_H, P_TOPK, P_NK, P_QD = 8, 16, 128, 256
N_EXP = P_NK * P_NK
EPS = 1e-6
A_W = HA_H * HA_DK
B_W = HB_H * HB_D
C_K = HC_H * HC_DK
CONV_CH = 3 * C_K
PAGE = 128

LANES = 128
SUBLANES = 8
VMEM_LIMIT = 52 * 1024 * 1024

OFF_FA, OFF_QA, OFF_IA, OFF_GA = 0, 512, 1024, 1536
OFF_GATE = 2048
OFF_QB, OFF_KB, OFF_VB = 5120, 5632, 6144
OFF_QC, OFF_KC, OFF_VC = 6656, 7168, 7680
OFF_ZC = 8192
OFF_FB = 8704
OFF_AB = 8832
NP = 8960

LIN_C = 64
GDN_C = 64
HGRN_SUB = 16
HGRN_HEADS_PER_STEP = 4
PEER_JB = 32
PEER_TI = 4
PEER_PIECES = 8
PEER_KW = 256
PAGES_PER_STEP = 16


def _dot(a, b):
    return jnp.dot(a, b, preferred_element_type=F32)


def _dot_hi(a, b):
    return jnp.dot(a, b, preferred_element_type=F32, precision=HI)


def _dot_nt(a, b):
    return lax.dot_general(a, b, (((1,), (1,)), ((), ())), preferred_element_type=F32)


def _dot_tn(a, b, precision=None):
    return lax.dot_general(a, b, (((0,), (0,)), ((), ())), preferred_element_type=F32, precision=precision)


def _iota(shape, dim):
    return lax.broadcasted_iota(jnp.int32, shape, dim)


def _sigmoid(x):
    return jax.nn.sigmoid(x)


def _silu(x):
    return x * jax.nn.sigmoid(x)


def _softplus(x):
    return jnp.maximum(x, 0.0) + jnp.log1p(jnp.exp(-jnp.abs(x)))


def _log_sigmoid(x):
    return jnp.minimum(x, 0.0) - jnp.log1p(jnp.exp(-jnp.abs(x)))


def _params(sem):
    return pltpu.CompilerParams(dimension_semantics=sem, vmem_limit_bytes=VMEM_LIMIT)


def _chunk_start(c, c_len):
    return c * c_len if isinstance(c, int) else pl.multiple_of(c * c_len, c_len)


def _for_chunks(n_chunks, body, unroll=False):
    if n_chunks == 1 or unroll:
        for c in range(n_chunks):
            body(c, 0)
    else:
        lax.fori_loop(0, n_chunks, body, 0)


def _tile(n, pref):
    t = min(n, pref)
    assert n % t == 0, (n, pref)
    return t


def _ada_kernel(c_ref, w_ref, b_ref, o_ref):
    sc = _silu(c_ref[...]).astype(BF16)
    o_ref[...] = _dot(sc, w_ref[...].astype(BF16)) + b_ref[...]


def _ada(c_all, w_ada, b_ada):
    rows = c_all.shape[0]
    tn = 768
    return pl.pallas_call(
        _ada_kernel,
        out_shape=jax.ShapeDtypeStruct((DEPTH, rows, 6 * D), F32),
        grid=(DEPTH, 6 * D // tn),
        in_specs=[pl.BlockSpec((rows, D), lambda l, n: (0, 0)),
                  pl.BlockSpec((None, D, tn), lambda l, n: (l, 0, n)),
                  pl.BlockSpec((None, 1, tn), lambda l, n: (l, 0, n))],
        out_specs=pl.BlockSpec((None, rows, tn), lambda l, n: (l, 0, n)),
        compiler_params=_params(("parallel", "parallel")),
        name="ada",
    )(c_all, w_ada, b_ada.reshape(DEPTH, 1, 6 * D))


def _mod_spec(mod, tm, k):
    if mod.shape[1] == 1:
        return pl.BlockSpec((None, 1, D), lambda b, i, *_: (b, 0, k))
    return pl.BlockSpec((None, tm, D), lambda b, i, *_: (b, i, k))


def _in_kernel(x_ref, sh_ref, sc_ref, g_ref, w_ref, z_ref, h_scr):
    @pl.when(pl.program_id(2) == 0)
    def _():
        x = x_ref[...]
        r = lax.rsqrt(jnp.mean(x * x, axis=-1, keepdims=True) + EPS)
        h = (x * r) * g_ref[...] * (1.0 + sc_ref[...]) + sh_ref[...]
        h_scr[...] = h.astype(BF16)

    z_ref[...] = _dot(h_scr[...], w_ref[...])


def _in_proj(x, mod, gain, w_packed):
    bg, lg, _ = x.shape
    tm = _tile(lg, 512)
    tn = 1280
    return pl.pallas_call(
        _in_kernel,
        out_shape=jax.ShapeDtypeStruct((bg, lg, NP), F32),
        grid=(bg, lg // tm, NP // tn),
        in_specs=[pl.BlockSpec((None, tm, D), lambda b, i, n: (b, i, 0)),
                  _mod_spec(mod, tm, 0), _mod_spec(mod, tm, 1),
                  pl.BlockSpec((1, D), lambda b, i, n: (0, 0)),
                  pl.BlockSpec((D, tn), lambda b, i, n: (0, n))],
        out_specs=pl.BlockSpec((None, tm, tn), lambda b, i, n: (b, i, n)),
        scratch_shapes=[pltpu.VMEM((tm, D), BF16)],
        compiler_params=_params(("parallel", "parallel", "arbitrary")),
        name="in_proj",
    )(x, mod, mod, gain.reshape(1, D), w_packed)


def _hgrn_kernel(fa_ref, qa_ref, ia_ref, ga_ref, lbl_ref, nw_ref, s0_ref, o_ref, sout_ref, s_scr,
                 *, layer, chunk, n_chunks, l_valid, l_padded):
    i = pl.program_id(2)
    c_len = chunk

    @pl.when(i == 0)
    def _():
        s_scr[...] = s0_ref[...]

    lg = lbl_ref[...]
    e = jnp.exp(lg - jnp.max(lg, axis=0, keepdims=True))
    p = e / jnp.sum(e, axis=0, keepdims=True)
    cs = p[0:1]
    for j in range(1, layer + 1):
        cs = cs + p[j:j + 1]
    lb = jnp.maximum(cs - p[0:1], 0.0)
    log_lb = jnp.log(lb)
    log1m_lb = jnp.log1p(-lb)

    sub = min(HGRN_SUB, c_len)
    tri = (_iota((c_len, c_len), 1) <= _iota((c_len, c_len), 0)).astype(F32)
    lane = _iota((sub, c_len), 1)
    row1 = _iota((c_len, 1), 0)
    ones_cv = jnp.ones((c_len, HA_DV), F32)

    def chunk_body(c, carry):
        for hh in range(HGRN_HEADS_PER_STEP):
            head_chunk(c, hh)
        return carry

    def head_chunk(c, hh):
        r = _chunk_start(c, c_len)
        hl = slice(hh * HA_DK, (hh + 1) * HA_DK)
        fa = fa_ref[pl.ds(r, c_len), hl]
        qa = qa_ref[pl.ds(r, c_len), hl]
        v = ia_ref[pl.ds(r, c_len), hl]
        ga = ga_ref[pl.ds(r, c_len), hl]

        b_ = log1m_lb[:, hl] + _log_sigmoid(fa)
        log_f = jnp.maximum(log_lb[:, hl], b_) + jnp.log1p(jnp.exp(-jnp.abs(log_lb[:, hl] - b_)))
        k = (1.0 - lb[:, hl]) * _sigmoid(-fa)
        if l_valid < l_padded:
            valid = (i * (n_chunks * c_len) + r + row1) < l_valid
            log_f = jnp.where(valid, log_f, 0.0)
            k = jnp.where(valid, k, 0.0)
        q = _silu(qa)
        a_cum = _dot_hi(tri, log_f)

        blocks = []
        for bi in range(c_len // sub):
            lo, hi = bi * sub, (bi + 1) * sub
            if bi == 0:
                att_b = jnp.zeros((sub, c_len), F32)
            else:
                a_ref = a_cum[lo - 1:lo]
                qs = q[lo:hi] * jnp.exp(a_cum[lo:hi] - a_ref)
                ks = jnp.where(row1 < lo, k * jnp.exp(jnp.minimum(a_ref - a_cum, 0.0)), 0.0)
                att_b = _dot_nt(qs.astype(BF16), ks.astype(BF16))
            for s in range(lo, hi):
                r0 = (s // SUBLANES) * SUBLANES
                rel = a_cum[r0:hi] - a_cum[s:s + 1]
                dec = jnp.exp(jnp.where(row1[r0:hi] >= s, rel, NEG_INF))
                col = jnp.sum(q[r0:hi] * k[s:s + 1] * dec, axis=-1, keepdims=True)
                if r0 > lo:
                    col = jnp.concatenate([jnp.zeros((r0 - lo, 1), F32), col], axis=0)
                att_b = jnp.where(lane == s, col, att_b)
            blocks.append(att_b)
        att = blocks[0] if len(blocks) == 1 else jnp.concatenate(blocks, axis=0)

        s_prev = s_scr[hh]
        qd = q * jnp.exp(a_cum)
        o = _dot(qd.astype(BF16), s_prev.astype(BF16)) + _dot(att.astype(BF16), v.astype(BF16))
        a_last = a_cum[c_len - 1:c_len]
        kd = k * jnp.exp(a_last - a_cum)
        dec_s = jnp.exp(_dot_tn(log_f, ones_cv, precision=HI))
        s_scr[hh] = dec_s * s_prev + _dot_tn(kd.astype(BF16), v.astype(BF16))

        rr = lax.rsqrt(jnp.mean(o * o, axis=-1, keepdims=True) + EPS)
        o_ref[pl.ds(r, c_len), hl] = ((o * rr) * nw_ref[...] * _silu(ga)).astype(o_ref.dtype)

    _for_chunks(n_chunks, chunk_body, unroll=True)

    @pl.when(i == pl.num_programs(2) - 1)
    def _():
        sout_ref[...] = s_scr[...]


def _hgrn(z3, lb_logits, norm_w, s0, *, layer, chunk, l_valid):
    bs, lp, _ = z3.shape
    tb = _tile(lp, 4 * chunk)
    kern = functools.partial(_hgrn_kernel, layer=layer, chunk=chunk, n_chunks=tb // chunk,
                             l_valid=l_valid, l_padded=lp)

    hg = HGRN_HEADS_PER_STEP
    wide = hg * HA_DK

    def col(off):
        return pl.BlockSpec((None, tb, wide), lambda b, h, i: (b, i, off // wide + h))

    return pl.pallas_call(
        kern,
        out_shape=(jax.ShapeDtypeStruct((bs, lp, A_W), BF16),
                   jax.ShapeDtypeStruct((bs, HA_H, HA_DK, HA_DV), F32)),
        grid=(bs, HA_H // hg, lp // tb),
        in_specs=[col(OFF_FA), col(OFF_QA), col(OFF_IA), col(OFF_GA),
                  pl.BlockSpec((DEPTH, wide), lambda b, h, i: (0, h)),
                  pl.BlockSpec((1, HA_DV), lambda b, h, i: (0, 0)),
                  pl.BlockSpec((None, hg, HA_DK, HA_DV), lambda b, h, i: (b, h, 0, 0))],
        out_specs=(pl.BlockSpec((None, tb, wide), lambda b, h, i: (b, i, h)),
                   pl.BlockSpec((None, hg, HA_DK, HA_DV), lambda b, h, i: (b, h, 0, 0))),
        scratch_shapes=[pltpu.VMEM((hg, HA_DK, HA_DV), F32)],
        compiler_params=_params(("parallel", "parallel", "arbitrary")),
        name="hgrn2",
    )(z3, z3, z3, z3, lb_logits, norm_w.reshape(1, HA_DV), s0)


def _gdn_kernel(q_ref, k_ref, v_ref, zc_ref, ab_ref, cw_ref, alog_ref, dt_ref, nw_ref, conv0_ref, s0_ref,
                o_ref, sout_ref, s_scr, prev_scr, act_scr, gb_scr, x_scr,
                *, chunk, n_chunks, l_valid, l_padded):
    i = pl.program_id(1)
    c_len = chunk
    tb = n_chunks * c_len

    @pl.when(i == 0)
    def _():
        s_scr[...] = s0_ref[...]
        prev_scr[...] = conv0_ref[...]

    row8 = _iota((SUBLANES, C_K), 0)

    def conv(x, prev, w):
        y = x * w[CONV_W - 1:CONV_W]
        for j in range(1, CONV_W):
            xr = pltpu.roll(x, j, 0)
            head = jnp.where(row8 < j, pltpu.roll(prev, j, 0), xr[:SUBLANES])
            xs = head if tb == SUBLANES else jnp.concatenate([head, xr[SUBLANES:]], axis=0)
            y = y + xs * w[CONV_W - 1 - j:CONV_W - j]
        return _silu(y)

    for n, ref in enumerate((q_ref, k_ref, v_ref)):
        x = ref[...]
        lo, hi = n * C_K, (n + 1) * C_K
        y = conv(x, prev_scr[:, lo:hi], cw_ref[:, lo:hi])
        for h in range(HC_H):
            act_scr[n * HC_H + h] = y[:, h * HC_DK:(h + 1) * HC_DK]
        prev_scr[:, lo:hi] = x[tb - SUBLANES:]

    ab = ab_ref[...]
    g_all = -jnp.exp(alog_ref[...]) * _softplus(ab + dt_ref[...])
    b_all = _sigmoid(ab)
    if l_valid < l_padded:
        valid = (i * tb + _iota((tb, 1), 0)) < l_valid
        g_all = jnp.where(valid, g_all, 0.0)
        b_all = jnp.where(valid, b_all, 0.0)
    gb_scr[:, 0:LANES] = g_all
    gb_scr[:, LANES:2 * LANES] = b_all

    ii = _iota((c_len, c_len), 0)
    jj = _iota((c_len, c_len), 1)
    tri = (jj <= ii).astype(F32)
    tri_u = (ii <= jj).astype(F32)

    def chunk_body(c, carry):
        r = _chunk_start(c, c_len)
        heads = []
        for h in range(HC_H):
            lo, hi = h * HC_DK, (h + 1) * HC_DK
            qh = act_scr[h, pl.ds(r, c_len), :]
            kh = act_scr[HC_H + h, pl.ds(r, c_len), :]
            vh = act_scr[2 * HC_H + h, pl.ds(r, c_len), :]
            qh = qh * lax.rsqrt(jnp.sum(qh * qh, axis=-1, keepdims=True) + EPS) * (HC_DK ** -0.5)
            kh = kh * lax.rsqrt(jnp.sum(kh * kh, axis=-1, keepdims=True) + EPS)
            g_col = gb_scr[pl.ds(r, c_len), h:h + 1]
            b_col = gb_scr[pl.ds(r, c_len), LANES + HC_H + h:LANES + HC_H + h + 1]
            g_b = jnp.broadcast_to(g_col, (c_len, LANES))
            g_cum = _dot_hi(tri, g_b)
            g_row = _dot_tn(g_b, tri_u, precision=HI)[:c_len]
            rel = g_cum[:, :c_len] - g_row
            d_causal = jnp.exp(jnp.where(jj <= ii, rel, NEG_INF))
            d_strict_t = jnp.exp(jnp.where(jj > ii, -rel, NEG_INF))
            kb = (b_col * kh).astype(BF16)
            khb = kh.astype(BF16)
            l_t = _dot_nt(khb, kb) * d_strict_t
            qk = _dot_nt(qh.astype(BF16), khb) * d_causal
            e_g = jnp.exp(g_cum)
            x_scr[h, 0] = b_col * e_g * kh
            x_scr[h, 1] = b_col * vh
            heads.append((qh, kh, l_t, qk, e_g, g_cum))

        for t in range(1, c_len):
            r1 = ((t + SUBLANES - 1) // SUBLANES) * SUBLANES
            for h in range(HC_H):
                col = heads[h][2][0:r1, t:t + 1]
                for part in range(2):
                    contrib = jnp.sum(x_scr[h, part, 0:r1, :] * col, axis=0, keepdims=True)
                    x_scr[h, part, t:t + 1, :] = x_scr[h, part, t:t + 1, :] - contrib

        for h in range(HC_H):
            qh, kh, _, qk, e_g, g_cum = heads[h]
            lo, hi = h * HC_DV, (h + 1) * HC_DV
            s_prev = s_scr[h]
            s_b = s_prev.astype(BF16)
            u = x_scr[h, 1] - _dot(x_scr[h, 0].astype(BF16), s_b)
            u_b = u.astype(BF16)
            o = e_g * _dot(qh.astype(BF16), s_b) + _dot(qk.astype(BF16), u_b)
            g_last = g_cum[c_len - 1:c_len]
            kd = kh * jnp.exp(g_last - g_cum)
            s_scr[h] = jnp.exp(g_last) * s_prev + _dot_tn(kd.astype(BF16), u_b)
            rr = lax.rsqrt(jnp.mean(o * o, axis=-1, keepdims=True) + EPS)
            zc = zc_ref[pl.ds(r, c_len), lo:hi]
            o_ref[pl.ds(r, c_len), lo:hi] = ((o * rr) * nw_ref[...] * _silu(zc)).astype(o_ref.dtype)
        return carry

    _for_chunks(n_chunks, chunk_body)

    @pl.when(i == pl.num_programs(1) - 1)
    def _():
        sout_ref[...] = s_scr[...]


def _gdn(z3, conv_w, a_log, dt_bias, norm_w, conv0, s0, *, chunk, l_valid):
    bs, lp, _ = z3.shape
    tb = _tile(lp, 4 * chunk)
    kern = functools.partial(_gdn_kernel, chunk=chunk, n_chunks=tb // chunk, l_valid=l_valid, l_padded=lp)
    pad = jnp.zeros((LANES - HC_H,), F32)
    alog_row = jnp.concatenate([a_log, pad]).reshape(1, LANES)
    dt_row = jnp.concatenate([dt_bias, pad]).reshape(1, LANES)

    def wide(off):
        return pl.BlockSpec((None, tb, C_K), lambda b, i: (b, i, off // C_K))

    return pl.pallas_call(
        kern,
        out_shape=(jax.ShapeDtypeStruct((bs, lp, C_K), BF16),
                   jax.ShapeDtypeStruct((bs, HC_H, HC_DK, HC_DV), F32)),
        grid=(bs, lp // tb),
        in_specs=[wide(OFF_QC), wide(OFF_KC), wide(OFF_VC), wide(OFF_ZC),
                  pl.BlockSpec((None, tb, LANES), lambda b, i: (b, i, OFF_AB // LANES)),
                  pl.BlockSpec((CONV_W, CONV_CH), lambda b, i: (0, 0)),
                  pl.BlockSpec((1, LANES), lambda b, i: (0, 0)),
                  pl.BlockSpec((1, LANES), lambda b, i: (0, 0)),
                  pl.BlockSpec((1, HC_DV), lambda b, i: (0, 0)),
                  pl.BlockSpec((None, SUBLANES, CONV_CH), lambda b, i: (b, 0, 0)),
                  pl.BlockSpec((None, HC_H, HC_DK, HC_DV), lambda b, i: (b, 0, 0, 0))],
        out_specs=(pl.BlockSpec((None, tb, C_K), lambda b, i: (b, i, 0)),
                   pl.BlockSpec((None, HC_H, HC_DK, HC_DV), lambda b, i: (b, 0, 0, 0))),
        scratch_shapes=[pltpu.VMEM((HC_H, HC_DK, HC_DV), F32),
                        pltpu.VMEM((SUBLANES, CONV_CH), F32),
                        pltpu.VMEM((3 * HC_H, tb, HC_DK), F32),
                        pltpu.VMEM((tb, 2 * LANES), F32),
                        pltpu.VMEM((HC_H, 2, chunk, HC_DK), F32)],
        compiler_params=_params(("parallel", "arbitrary")),
        name="gdn",
    )(z3, z3, z3, z3, z3, conv_w, alog_row, dt_row, norm_w.reshape(1, HC_DV), conv0, s0)


def _foxprep_kernel(q_ref, k_ref, v_ref, fb_ref, gq_ref, gk_ref, bf_ref, bd_ref,
                    qo_ref, ko_ref, kbo_ref, vbo_ref, lfo_ref, *rest, cumsum):
    bd = bd_ref[...]

    def head_rms(x, g):
        x2 = x * x
        hi = x2.astype(BF16)
        lo = (x2 - hi.astype(F32)).astype(BF16)
        ss = _dot(hi, bd) + _dot(lo, bd)
        return x * lax.rsqrt(ss * (1.0 / HB_D) + EPS) * g

    qn = head_rms(q_ref[...], gq_ref[...])
    kn = head_rms(k_ref[...], gk_ref[...])
    qo_ref[...] = (qn * (HB_D ** -0.5)).astype(BF16)
    ko_ref[...] = kn
    kbo_ref[...] = kn.astype(BF16)
    vbo_ref[...] = v_ref[...].astype(BF16)
    lf = _log_sigmoid(fb_ref[...] + bf_ref[...])
    lf = jnp.where(_iota(lf.shape, 1) < HB_H, lf, 0.0)
    lfo_ref[...] = lf
    if cumsum:
        ft_ref, carry = rest
        tm = lf.shape[0]

        @pl.when(pl.program_id(1) == 0)
        def _():
            carry[...] = jnp.zeros_like(carry)

        tri = (_iota((tm, tm), 1) <= _iota((tm, tm), 0)).astype(F32)
        f_cum = _dot_hi(tri, lf) + carry[...]
        carry[...] = f_cum[tm - 1:tm]
        ft_ref[...] = f_cum.T[:HB_H]


def _fox_prep(z3, gq, gk, b_f, *, cumsum):
    bg, lg, _ = z3.shape
    tm = _tile(lg, 256)
    gq_row = jnp.tile(gq, HB_H).reshape(1, B_W)
    gk_row = jnp.tile(gk, HB_H).reshape(1, B_W)
    bf_row = jnp.concatenate([b_f, jnp.zeros((LANES - HB_H,), F32)]).reshape(1, LANES)
    seg = jnp.arange(B_W) // HB_D
    bd = (seg[:, None] == seg[None, :]).astype(BF16)

    def wide(off):
        return pl.BlockSpec((None, tm, B_W), lambda b, i: (b, i, off // B_W))

    tok = pl.BlockSpec((None, tm, B_W), lambda b, i: (b, i, 0))
    out_shape = [jax.ShapeDtypeStruct((bg, lg, B_W), BF16), jax.ShapeDtypeStruct((bg, lg, B_W), F32),
                 jax.ShapeDtypeStruct((bg, lg, B_W), BF16), jax.ShapeDtypeStruct((bg, lg, B_W), BF16),
                 jax.ShapeDtypeStruct((bg, lg, LANES), F32)]
    out_specs = [tok, tok, tok, tok, pl.BlockSpec((None, tm, LANES), lambda b, i: (b, i, 0))]
    scratch = []
    if cumsum:
        out_shape.append(jax.ShapeDtypeStruct((bg, HB_H, lg), F32))
        out_specs.append(pl.BlockSpec((None, HB_H, tm), lambda b, i: (b, 0, i)))
        scratch.append(pltpu.VMEM((1, LANES), F32))
    return pl.pallas_call(
        functools.partial(_foxprep_kernel, cumsum=cumsum),
        out_shape=tuple(out_shape),
        grid=(bg, lg // tm),
        in_specs=[wide(OFF_QB), wide(OFF_KB), wide(OFF_VB),
                  pl.BlockSpec((None, tm, LANES), lambda b, i: (b, i, OFF_FB // LANES)),
                  pl.BlockSpec((1, B_W), lambda b, i: (0, 0)),
                  pl.BlockSpec((1, B_W), lambda b, i: (0, 0)),
                  pl.BlockSpec((1, LANES), lambda b, i: (0, 0)),
                  pl.BlockSpec((B_W, B_W), lambda b, i: (0, 0))],
        out_specs=tuple(out_specs),
        scratch_shapes=scratch,
        compiler_params=_params(("parallel", "arbitrary")),
        name="fox_prep",
    )(z3, z3, z3, z3, gq_row, gk_row, bf_row, bd)


def _foxattn_kernel(q_ref, k_ref, v_ref, f_ref, o_ref, *, tq):
    qi = pl.program_id(2)
    q = q_ref[...]
    lane_q = _iota(q.shape, 1)
    q_heads = (jnp.where(lane_q < HB_D, q, jnp.zeros_like(q)), jnp.where(lane_q >= HB_D, q, jnp.zeros_like(q)))
    row = _iota((tq, tq), 0)
    colm = _iota((tq, tq), 1)

    def step(j, carry, masked):
        r = pl.multiple_of(j * tq, tq)
        kj = k_ref[pl.ds(r, tq), :]
        vj = v_ref[pl.ds(r, tq), :]
        fj = f_ref[j]
        out = []
        for hh in range(2):
            m, l, acc = carry[hh]
            s = _dot_nt(q_heads[hh], kj) - fj[hh:hh + 1, :]
            if masked:
                s = jnp.where(colm <= row, s, NEG_INF)
            m_new = jnp.maximum(m, jnp.max(s, axis=-1, keepdims=True))
            alpha = jnp.exp(m - m_new)
            p = jnp.exp(s - m_new)
            l = alpha * l + jnp.sum(p, axis=-1, keepdims=True)
            acc = alpha * acc + _dot(p.astype(BF16), vj)
            out.append((m_new, l, acc))
        return tuple(out)

    init = tuple((jnp.full((tq, 1), -1e30, F32), jnp.zeros((tq, 1), F32), jnp.zeros((tq, LANES), F32))
                 for _ in range(2))
    carry = lax.fori_loop(0, qi, lambda j, c: step(j, c, False), init)
    (_, l0, a0), (_, l1, a1) = step(qi, carry, True)
    o = jnp.where(_iota((tq, LANES), 1) < HB_D, a0 / l0, a1 / l1)
    o_ref[...] = o.astype(o_ref.dtype)


def _fox_attn_prompt(qb, kb, vb, ft):
    bg, s_len, _ = qb.shape
    tq = _tile(s_len, 512)
    nk = s_len // tq
    pairs = HB_H // 2
    f5 = ft.reshape(bg, pairs, 2, nk, tq).transpose(0, 1, 3, 2, 4)
    return pl.pallas_call(
        functools.partial(_foxattn_kernel, tq=tq),
        out_shape=jax.ShapeDtypeStruct((bg, s_len, B_W), BF16),
        grid=(bg, pairs, nk),
        in_specs=[pl.BlockSpec((None, tq, LANES), lambda b, p, i: (b, i, p)),
                  pl.BlockSpec((None, s_len, LANES), lambda b, p, i: (b, 0, p)),
                  pl.BlockSpec((None, s_len, LANES), lambda b, p, i: (b, 0, p)),
                  pl.BlockSpec((None, None, nk, 2, tq), lambda b, p, i: (b, p, 0, 0, 0))],
        out_specs=pl.BlockSpec((None, tq, LANES), lambda b, p, i: (b, i, p)),
        compiler_params=_params(("parallel", "parallel", "arbitrary")),
        name="fox_attn_prompt",
    )(qb, kb, vb, f5)


def _foxsample_kernel(pt_ref, q_ref, kn_ref, vn_ref, lfn_ref, *rest, n_new, ps):
    k_pages = rest[0:ps]
    v_pages = rest[ps:2 * ps]
    f_pages = rest[2 * ps:3 * ps]
    o_ref, q_scr, m_scr, l_scr, acc_scr, fc_scr = rest[3 * ps:]
    g = pl.program_id(1)
    rows = n_new * HB_H
    hmask = (_iota((HB_H, B_W), 1) // HB_D) == _iota((HB_H, B_W), 0)

    @pl.when(g == 0)
    def _():
        for t in range(n_new):
            qt = jnp.broadcast_to(q_ref[t:t + 1, :], (HB_H, B_W))
            q_scr[t * HB_H:(t + 1) * HB_H, :] = jnp.where(hmask, qt, 0.0)
        m_scr[...] = jnp.full_like(m_scr, -1e30)
        l_scr[...] = jnp.zeros_like(l_scr)
        acc_scr[...] = jnp.zeros_like(acc_scr)
        fc_scr[...] = jnp.zeros_like(fc_scr)

    tri_u = (_iota((PAGE, PAGE), 0) <= _iota((PAGE, PAGE), 1)).astype(F32)

    def update(s, v_b):
        m = m_scr[...]
        m_new = jnp.maximum(m, jnp.max(s, axis=-1, keepdims=True))
        alpha = jnp.exp(m - m_new)
        p = jnp.exp(s - m_new)
        l_scr[...] = alpha * l_scr[...] + jnp.sum(p, axis=-1, keepdims=True)
        acc_scr[...] = alpha * acc_scr[...] + _dot_nt(p.astype(BF16), v_b)
        m_scr[...] = m_new

    def cum_forget(lf_parts):
        f_loc = [_dot_hi(lf_t, tri_u) for lf_t in lf_parts]
        off = fc_scr[...]
        out = []
        for fl in f_loc:
            out.append(fl + off)
            off = off + fl[:, PAGE - 1:PAGE]
        fc_scr[...] = off
        return out

    f_cat = jnp.concatenate(cum_forget([f_pages[r][...] for r in range(ps)]), axis=1)
    bias = jnp.concatenate([f_cat] * n_new, axis=0)
    k_cat = jnp.concatenate([k_pages[r][...].astype(BF16) for r in range(ps)], axis=1)
    v_cat = jnp.concatenate([v_pages[r][...].astype(BF16) for r in range(ps)], axis=1)
    q_rows = q_scr[...].astype(BF16)
    update(_dot(q_rows, k_cat) - bias, v_cat)

    @pl.when(g == pl.num_programs(1) - 1)
    def _():
        f_new = cum_forget([lfn_ref[...]])[0]
        s = _dot(q_rows, kn_ref[...].astype(BF16)) - jnp.concatenate([f_new] * n_new, axis=0)
        visible = _iota((rows, PAGE), 1) <= (_iota((rows, PAGE), 0) // HB_H)
        update(jnp.where(visible, s, NEG_INF), vn_ref[...].astype(BF16))
        o = acc_scr[...] / l_scr[...]
        out = jnp.zeros((SUBLANES, B_W), F32)
        out_row = _iota((SUBLANES, B_W), 0)
        for t in range(n_new):
            ot = jnp.where(hmask, o[t * HB_H:(t + 1) * HB_H, :], 0.0)
            out = jnp.where(out_row == t, jnp.sum(ot, axis=0, keepdims=True), out)
        o_ref[...] = out.astype(o_ref.dtype)


def _fox_attn_sample(layer, q_new, k_new, v_new, lf_new, cache_kt, cache_vt, cache_lf_t, page_table):
    db, n_new, _ = q_new.shape
    n_pages = page_table.shape[1]
    n_pool = cache_kt.shape[0] // DEPTH
    ps = min(PAGES_PER_STEP, n_pages)
    assert n_pages % ps == 0 and n_new <= SUBLANES
    qp = jnp.pad(q_new, ((0, 0), (0, SUBLANES - n_new), (0, 0)))
    pad_pos = ((0, 0), (0, 0), (0, PAGE - n_new))
    knp = jnp.pad(k_new.transpose(0, 2, 1), pad_pos)
    vnp = jnp.pad(v_new.transpose(0, 2, 1), pad_pos)
    lfp = jnp.pad(lf_new.transpose(0, 2, 1), pad_pos)
    base = layer * n_pool

    def page_spec(r, rows, cols):
        return pl.BlockSpec((None, rows, cols), lambda b, g, pt: (base + pt[b, g * ps + r], 0, 0))

    in_specs = [pl.BlockSpec((None, SUBLANES, B_W), lambda b, g, pt: (b, 0, 0)),
                pl.BlockSpec((None, B_W, PAGE), lambda b, g, pt: (b, 0, 0)),
                pl.BlockSpec((None, B_W, PAGE), lambda b, g, pt: (b, 0, 0)),
                pl.BlockSpec((None, HB_H, PAGE), lambda b, g, pt: (b, 0, 0))]
    in_specs += [page_spec(r, B_W, PAGE) for r in range(ps)]
    in_specs += [page_spec(r, B_W, PAGE) for r in range(ps)]
    in_specs += [page_spec(r, HB_H, PAGE) for r in range(ps)]
    rows = n_new * HB_H
    return pl.pallas_call(
        functools.partial(_foxsample_kernel, n_new=n_new, ps=ps),
        out_shape=jax.ShapeDtypeStruct((db, SUBLANES, B_W), BF16),
        grid_spec=pltpu.PrefetchScalarGridSpec(
            num_scalar_prefetch=1,
            grid=(db, n_pages // ps),
            in_specs=in_specs,
            out_specs=pl.BlockSpec((None, SUBLANES, B_W), lambda b, g, pt: (b, 0, 0)),
            scratch_shapes=[pltpu.VMEM((rows, B_W), F32), pltpu.VMEM((rows, 1), F32),
                            pltpu.VMEM((rows, 1), F32), pltpu.VMEM((rows, B_W), F32),
                            pltpu.VMEM((HB_H, 1), F32)]),
        compiler_params=_params(("parallel", "arbitrary")),
        name="fox_attn_sample",
    )(page_table, qp, knp, vnp, lfp, *([cache_kt] * ps), *([cache_vt] * ps), *([cache_lf_t] * ps))


def _merge_kernel(oa_ref, ob_ref, oc_ref, ga_ref, gb_ref, gc_ref, x_ref, gt1_ref, sh2_ref, sc2_ref,
                  wa_ref, wb_ref, wc_ref, wo_ref, nf_ref, wq_ref, keys_ref,
                  x1_ref, h2t_ref, st_ref):
    merged = (_sigmoid(ga_ref[...]) * _dot(oa_ref[...], wa_ref[...])
              + _sigmoid(gb_ref[...]) * _dot(ob_ref[...], wb_ref[...])
              + _sigmoid(gc_ref[...]) * _dot(oc_ref[...], wc_ref[...]))
    x1 = x_ref[...] + gt1_ref[...] * _dot(merged.astype(BF16), wo_ref[...])
    x1_ref[...] = x1
    r = lax.rsqrt(jnp.mean(x1 * x1, axis=-1, keepdims=True) + EPS)
    h2 = (x1 * r) * nf_ref[...] * (1.0 + sc2_ref[...]) + sh2_ref[...]
    h2t_ref[...] = h2.T.astype(BF16)
    qb = _dot(h2.astype(BF16), wq_ref[...]).astype(BF16)
    half = P_QD // 2
    for h in range(P_H):
        for p in range(2):
            lo = (h * 2 + p) * half
            st_ref[lo:lo + half, :] = _dot_nt(keys_ref[p], qb[:, lo:lo + half])


def _merge(oa, ob, oc, z3, x, mod, w_a, w_b, w_c, w_o, norm_ffn, w_q, keys):
    bg, lg, _ = x.shape
    tm = _tile(lg, 256)
    nl = lg // tm
    t_all = bg * lg

    def tok(width):
        return pl.BlockSpec((None, tm, width), lambda b, i: (b, i, 0))

    def gate(k):
        return pl.BlockSpec((None, tm, D), lambda b, i: (b, i, OFF_GATE // D + k))

    def full(shape):
        return pl.BlockSpec(shape, lambda b, i: (0,) * len(shape))

    return pl.pallas_call(
        _merge_kernel,
        out_shape=(jax.ShapeDtypeStruct((bg, lg, D), F32),
                   jax.ShapeDtypeStruct((D, t_all), BF16),
                   jax.ShapeDtypeStruct((P_H * P_QD, t_all), F32)),
        grid=(bg, nl),
        in_specs=[tok(A_W), tok(B_W), tok(C_K), gate(0), gate(1), gate(2), tok(D),
                  _mod_spec(mod, tm, 2), _mod_spec(mod, tm, 3), _mod_spec(mod, tm, 4),
                  full((A_W, D)), full((B_W, D)), full((C_K, D)), full((D, D)), full((1, D)),
                  full((D, P_H * P_QD)), full((2, P_NK, P_QD // 2))],
        out_specs=(tok(D),
                   pl.BlockSpec((D, tm), lambda b, i: (0, b * nl + i)),
                   pl.BlockSpec((P_H * P_QD, tm), lambda b, i: (0, b * nl + i))),
        compiler_params=_params(("parallel", "parallel")),
        name="merge",
    )(oa, ob, oc, z3, z3, z3, x, mod, mod, mod, w_a, w_b, w_c, w_o, norm_ffn.reshape(1, D), w_q, keys)


def _topk_kernel(st_ref, o_ref):
    tt = st_ref.shape[1]

    def top_rows(s_ref, lo):
        n_v = P_NK // SUBLANES
        v = [s_ref[lo + k * SUBLANES:lo + (k + 1) * SUBLANES, :] for k in range(n_v)]
        k = 2
        while k <= n_v:
            j = k // 2
            while j >= 1:
                for a in range(n_v):
                    b = a ^ j
                    if b > a:
                        hi, lo_v = jnp.maximum(v[a], v[b]), jnp.minimum(v[a], v[b])
                        v[a], v[b] = (hi, lo_v) if (a & k) == 0 else (lo_v, hi)
                j //= 2
            k *= 2
        rows = []
        for it in range(P_TOPK + 1):
            m = jnp.max(v[0], axis=0, keepdims=True)
            rows.append(m)
            popped = v[0] == m
            for a in range(P_TOPK - it):
                v[a] = jnp.where(popped, v[a + 1] if a + 1 < n_v else NEG_INF, v[a])
        return rows

    rank = _iota((P_TOPK, tt), 0)
    rank8 = _iota((SUBLANES, tt), 0)
    thr_rows, nrm_rows = [], []
    for h in range(P_H):
        lo = h * P_QD
        v1 = top_rows(st_ref, lo)
        v2 = top_rows(st_ref, lo + P_NK)
        v1s = jnp.concatenate(v1[:P_TOPK], axis=0)
        v2s = jnp.concatenate(v2[:P_TOPK], axis=0)
        groups = [jnp.where(rank8 == 0, v1[P_TOPK] + v2[0], jnp.where(rank8 == 1, v1[0] + v2[P_TOPK], NEG_INF))]
        for b in range(3):
            groups.append(jnp.where(rank < P_TOPK // (b + 1), v1s + v2[b], NEG_INF))
        for a in range(4):
            nb = P_TOPK // (a + 1)
            n_rows = P_TOPK if nb > SUBLANES else SUBLANES
            rk = _iota((n_rows, tt), 0)
            ok = jnp.where(rk >= 3, rk, nb) < nb
            groups.append(jnp.where(ok, v2s[:n_rows] + v1[a], NEG_INF))
        m_top = None
        z = None
        for it in range(P_TOPK):
            m = functools.reduce(jnp.maximum, [jnp.max(gp, axis=0, keepdims=True) for gp in groups])
            if it == 0:
                m_top = m
                z = jnp.ones_like(m)
            else:
                z = z + jnp.exp(m - m_top)
            groups = [jnp.where(gp == m, NEG_INF, gp) for gp in groups]
        m_next = functools.reduce(jnp.maximum, [jnp.max(gp, axis=0, keepdims=True) for gp in groups])
        thr_rows.append(0.5 * m + 0.5 * m_next)
        nrm_rows.append(-(m_top + jnp.log(z)))
    o_ref[...] = jnp.concatenate(thr_rows + nrm_rows, axis=0)


def _topk(st):
    t_all = st.shape[1]
    tt = _tile(t_all, 256)
    return pl.pallas_call(
        _topk_kernel,
        out_shape=jax.ShapeDtypeStruct((2 * P_H, t_all), F32),
        grid=(t_all // tt,),
        in_specs=[pl.BlockSpec((P_H * P_QD, tt), lambda t: (0, t))],
        out_specs=pl.BlockSpec((2 * P_H, tt), lambda t: (0, t)),
        compiler_params=_params(("parallel",)),
        name="peer_topk",
    )(st)


def _peer_kernel(st_ref, stat_ref, h2t_ref, u0_ref, u1_ref, vt0_ref, vt1_ref, x1_ref, gt2_ref, o_ref,
                 acc_scr, e1_scr, tau_scr, e2_scr, s2_scr, ht0_scr, ht1_scr, wa0_scr, wa1_scr, *, ti, n_tiles):
    s = pl.program_id(2)
    tt = st_ref.shape[1]

    @pl.when(s == 0)
    def _():
        acc_scr[...] = jnp.zeros_like(acc_scr)
        ht0_scr[...] = jnp.zeros_like(ht0_scr)
        ht1_scr[...] = jnp.zeros_like(ht1_scr)
        wa0_scr[...] = jnp.zeros_like(wa0_scr)
        wa1_scr[...] = jnp.zeros_like(wa1_scr)
        for h in range(P_H):
            lo = h * P_QD
            hr = slice(h * P_NK, (h + 1) * P_NK)
            for lt in range(tt // LANES):
                ls = slice(lt * LANES, (lt + 1) * LANES)
                s1 = st_ref[lo:lo + P_NK, ls]
                s2 = st_ref[lo + P_NK:lo + 2 * P_NK, ls]
                mx2 = jnp.max(s2, axis=0, keepdims=True)
                s2_scr[lt, hr, :] = s2
                e2_scr[lt, hr, :] = jnp.exp(s2 - mx2)
                e1_scr[hr, ls] = 0.5 * jnp.exp(s1 + (stat_ref[P_H + h:P_H + h + 1, ls] + mx2))
                tau_scr[hr, ls] = stat_ref[h:h + 1, ls] - s1

    c0 = math.sqrt(2.0 / math.pi)
    c1 = c0 * 0.044715
    tile_b = jnp.clip(s - 1, 0, n_tiles - 1)

    def stages(ht_w, ht_r, wa_w, wa_r):
        te = ti * P_NK
        n_lt = tt // LANES
        n_p = min(PEER_PIECES, ti * n_lt)
        per_group = ti * n_lt // n_p
        n_split = 2 if n_lt % 2 == 0 else 1
        m_split = n_p // n_split
        ma, mc, nw = te // m_split, D // m_split, n_lt // n_split
        assert per_group >= 1 and (te // 2) % ma == 0 and (D // 2) % mc == 0

        def matmul_pieces(pi):
            mi, ni = pi // n_split, pi % n_split
            lts = range(ni * nw, (ni + 1) * nw)
            u_half = (u0_ref, u1_ref)[mi * ma // (te // 2)]
            a0 = (mi * ma) % (te // 2)
            part_a = _dot(u_half[a0:a0 + ma, :], h2t_ref[:, ni * nw * LANES:(ni + 1) * nw * LANES])
            vt_half = (vt0_ref, vt1_ref)[mi * mc // (D // 2)]
            v0 = (mi * mc) % (D // 2)
            part_c = _dot(vt_half[v0:v0 + mc, :], jnp.concatenate([wa_r[lt] for lt in lts], axis=1))
            for k, lt in enumerate(lts):
                ht_w[lt, mi * ma:(mi + 1) * ma, :] = part_a[:, k * LANES:(k + 1) * LANES]
                acc_scr[lt, mi * mc:(mi + 1) * mc, :] += part_c[:, k * LANES:(k + 1) * LANES]

        for ii in range(ti):
            i_row = tile_b * ti + ii
            tau_rows = [tau_scr[pl.ds(h * P_NK + i_row, 1), :] for h in range(P_H)]
            e1_rows = [e1_scr[pl.ds(h * P_NK + i_row, 1), :] for h in range(P_H)]
            for lt in range(n_lt):
                if (ii * n_lt + lt) % per_group == 0:
                    matmul_pieces((ii * n_lt + lt) // per_group)
                ls = slice(lt * LANES, (lt + 1) * LANES)
                tau_b = [jnp.broadcast_to(tau_rows[h][:, ls], (PEER_JB, LANES)) for h in range(P_H)]
                e1_b = [jnp.broadcast_to(e1_rows[h][:, ls], (PEER_JB, LANES)) for h in range(P_H)]
                for jb in range(P_NK // PEER_JB):
                    j0 = jb * PEER_JB
                    hs = ht_r[lt, ii * P_NK + j0:ii * P_NK + j0 + PEER_JB, :]
                    act = hs * (1.0 + jnp.tanh(hs * (c0 + c1 * (hs * hs))))
                    w = None
                    for h in range(P_H):
                        jr = slice(h * P_NK + j0, h * P_NK + j0 + PEER_JB)
                        wh = jnp.where(s2_scr[lt, jr, :] >= tau_b[h], e2_scr[lt, jr, :] * e1_b[h], 0.0)
                        w = wh if w is None else w + wh
                    wa_w[lt, ii * P_NK + j0:ii * P_NK + j0 + PEER_JB, :] = (w * act).astype(BF16)

    @pl.when(s % 2 == 0)
    def _():
        stages(ht0_scr, ht1_scr, wa1_scr, wa0_scr)

    @pl.when(s % 2 == 1)
    def _():
        stages(ht1_scr, ht0_scr, wa0_scr, wa1_scr)

    @pl.when(s == pl.num_programs(2) - 1)
    def _():
        for lt in range(tt // LANES):
            rows = slice(lt * LANES, (lt + 1) * LANES)
            gt2 = gt2_ref[rows, :] if gt2_ref.shape[0] == tt else gt2_ref[0:1, :]
            o_ref[rows, :] = x1_ref[rows, :] + gt2 * acc_scr[lt].T


def _peer(st, stats, h2t, u_b, vt_b, x1, mod):
    bg, lg, _ = x1.shape
    tt = _tile(lg, 512)
    nl = lg // tt
    ti = PEER_TI
    te = ti * P_NK
    assert vt_b.shape == (N_EXP // te, D, te)
    n_tiles = N_EXP // te
    if mod.shape[1] == 1:
        mod = jnp.broadcast_to(mod, (bg, SUBLANES, 6 * D))
        gate_spec = pl.BlockSpec((None, SUBLANES, D), lambda b, i, s: (b, 0, 5))
    else:
        gate_spec = _mod_spec(mod, tt, 5)
    return pl.pallas_call(
        functools.partial(_peer_kernel, ti=ti, n_tiles=n_tiles),
        out_shape=jax.ShapeDtypeStruct((bg, lg, D), F32),
        grid=(bg, nl, n_tiles + 2),
        in_specs=[pl.BlockSpec((P_H * P_QD, tt), lambda b, i, s: (0, b * nl + i)),
                  pl.BlockSpec((2 * P_H, tt), lambda b, i, s: (0, b * nl + i)),
                  pl.BlockSpec((D, tt), lambda b, i, s: (0, b * nl + i)),
                  pl.BlockSpec((te // 2, D), lambda b, i, s: (2 * jnp.minimum(s, n_tiles - 1), 0)),
                  pl.BlockSpec((te // 2, D), lambda b, i, s: (2 * jnp.minimum(s, n_tiles - 1) + 1, 0)),
                  pl.BlockSpec((None, D // 2, te), lambda b, i, s: (jnp.clip(s - 2, 0, n_tiles - 1), 0, 0)),
                  pl.BlockSpec((None, D // 2, te), lambda b, i, s: (jnp.clip(s - 2, 0, n_tiles - 1), 1, 0)),
                  pl.BlockSpec((None, tt, D), lambda b, i, s: (b, i, 0)),
                  gate_spec],
        out_specs=pl.BlockSpec((None, tt, D), lambda b, i, s: (b, i, 0)),
        scratch_shapes=[pltpu.VMEM((tt // LANES, D, LANES), F32),
                        pltpu.VMEM((P_H * P_NK, tt), F32), pltpu.VMEM((P_H * P_NK, tt), F32),
                        pltpu.VMEM((tt // LANES, P_H * P_NK, LANES), F32),
                        pltpu.VMEM((tt // LANES, P_H * P_NK, LANES), F32),
                        pltpu.VMEM((tt // LANES, te, LANES), F32), pltpu.VMEM((tt // LANES, te, LANES), F32),
                        pltpu.VMEM((tt // LANES, te, LANES), BF16), pltpu.VMEM((tt // LANES, te, LANES), BF16)],
        compiler_params=_params(("parallel", "parallel", "arbitrary")),
        name="peer_experts",
    )(st, stats, h2t, u_b, u_b, vt_b, vt_b, x1, mod)


def _pack_w_in(w):
    pad = jnp.zeros((D, LANES - 8), w.dtype)
    o_fb = 2 * A_W + 2 * A_W + 3 * B_W
    o_c = o_fb + HB_H
    o_ab = o_c + 3 * C_K
    o_zc = o_ab + 2 * HC_H
    o_gate = o_zc + C_K
    packed = jnp.concatenate([
        w[:, 0:4 * A_W], w[:, o_gate:o_gate + 3 * D], w[:, 4 * A_W:o_fb], w[:, o_c:o_ab],
        w[:, o_zc:o_gate], w[:, o_fb:o_c], pad, w[:, o_ab:o_zc], pad], axis=1)
    assert packed.shape[1] == NP
    return packed.astype(BF16)


def _layer(l, x, mod, w, sample):
    bg, lg, _ = x.shape
    z = _in_proj(x, mod, w["norm_mix"][l], w["w_in"][l])
    if sample is None:
        z3 = z
        bs, l_seq = bg, lg
        chunk = LIN_C
        hgrn0 = jnp.zeros((bs, HA_H, HA_DK, HA_DV), F32)
        gdn0 = jnp.zeros((bs, HC_H, HC_DK, HC_DV), F32)
        conv0 = jnp.zeros((bs, SUBLANES, CONV_CH), F32)
    else:
        bs, l_seq = sample["db"], sample["t"]
        chunk = SUBLANES
        z3 = jnp.pad(z.reshape(bs, l_seq, NP), ((0, 0), (0, SUBLANES - l_seq), (0, 0)))
        hgrn0 = sample["state_hgrn"][l]
        gdn0 = sample["state_gdn"][l]
        conv0 = jnp.pad(sample["state_conv"][l], ((0, 0), (SUBLANES - (CONV_W - 1), 0), (0, 0)))

    o_a, s_hgrn = _hgrn(z3, w["hgrn_lb_logits"], w["hgrn_norm"][l], hgrn0, layer=l, chunk=chunk, l_valid=l_seq)
    o_c, s_gdn = _gdn(z3, w["gdn_conv_w"][l], w["gdn_a_log"][l], w["gdn_dt_bias"][l], w["gdn_norm"][l],
                      conv0, gdn0, chunk=min(chunk, GDN_C), l_valid=l_seq)
    zseq = z.reshape(bs, l_seq, NP)
    conv_new = zseq[:, l_seq - (CONV_W - 1):, OFF_QC:OFF_QC + CONV_CH]

    prep = _fox_prep(z, w["fox_q_norm"][l], w["fox_k_norm"][l], w["fox_b_f"][l], cumsum=sample is None)
    qn, kn, knb, vnb, lf = prep[:5]
    v_b = z[:, :, OFF_VB:OFF_VB + B_W]
    if sample is None:
        o_b = _fox_attn_prompt(qn, knb, vnb, prep[5])
    else:
        o_b = _fox_attn_sample(
            l, qn.astype(F32).reshape(bs, l_seq, B_W), kn.reshape(bs, l_seq, B_W), v_b.reshape(bs, l_seq, B_W),
            lf[0, :, :HB_H].reshape(bs, l_seq, HB_H), sample["cache_kt"], sample["cache_vt"], sample["cache_lf_t"],
            sample["page_table"])
        o_b = o_b[:, :l_seq].reshape(bg, lg, B_W)
        o_a = o_a[:, :l_seq].reshape(bg, lg, A_W)
        o_c = o_c[:, :l_seq].reshape(bg, lg, C_K)

    x1, h2t, st = _merge(o_a, o_b, o_c, z, x, mod, w["w_br_a"][l], w["w_br_b"][l], w["w_br_c"][l],
                         w["w_out"][l], w["norm_ffn"][l], w["peer_w_q"][l], w["peer_keys"][l])
    stats = _topk(st)
    x2 = _peer(st, stats, h2t, w["peer_u"][l], w["peer_vt"][l], x1, mod)

    k_leaf = kn.reshape(bs, l_seq, HB_H, HB_D)
    v_leaf = v_b.reshape(bs, l_seq, HB_H, HB_D)
    lf_leaf = lf[:, :, :HB_H].reshape(bs, l_seq, HB_H)
    return x2, (k_leaf, v_leaf, lf_leaf, s_hgrn, s_gdn, conv_new)


def _trunk(x, mods, w, sample):
    leaves = [[] for _ in range(6)]
    for l in range(DEPTH):
        x, st = _layer(l, x, mods[l], w, sample)
        for lst, s in zip(leaves, st):
            lst.append(s)
    return x, [jnp.stack(v) for v in leaves]


def kernel(x_prompt, x_sample, c_prompt, c_sample, cache_fox_k, cache_fox_v, cache_fox_logf, page_table,
           state_hgrn, state_gdn, state_gdn_conv, w_ada, b_ada, norm_mix, norm_ffn, w_in, hgrn_lb_logits,
           hgrn_norm, fox_b_f, fox_q_norm, fox_k_norm, gdn_conv_w, gdn_a_log, gdn_dt_bias, gdn_norm,
           w_br_a, w_br_b, w_br_c, w_out, peer_w_q, peer_keys, peer_u, peer_v):
    bp = x_prompt.shape[0]
    db, t_new, _ = x_sample.shape
    n_pool = cache_fox_k.shape[1]

    w = {
        "norm_mix": norm_mix, "norm_ffn": norm_ffn, "hgrn_lb_logits": hgrn_lb_logits, "hgrn_norm": hgrn_norm,
        "fox_b_f": fox_b_f, "fox_q_norm": fox_q_norm, "fox_k_norm": fox_k_norm, "gdn_conv_w": gdn_conv_w,
        "gdn_a_log": gdn_a_log, "gdn_dt_bias": gdn_dt_bias, "gdn_norm": gdn_norm,
        "w_in": [_pack_w_in(w_in[l]) for l in range(DEPTH)],
        "w_br_a": w_br_a.astype(BF16), "w_br_b": w_br_b.astype(BF16), "w_br_c": w_br_c.astype(BF16),
        "w_out": w_out.astype(BF16), "peer_w_q": peer_w_q.astype(BF16), "peer_keys": peer_keys.astype(BF16),
        "peer_u": peer_u.astype(BF16),
        "peer_vt": peer_v.astype(BF16).reshape(DEPTH, N_EXP // (PEER_TI * P_NK), PEER_TI * P_NK, D).transpose(0, 1, 3, 2),
    }

    n_c = bp + db
    c_all = jnp.pad(jnp.concatenate([c_prompt, c_sample], axis=0), ((0, (-n_c) % SUBLANES), (0, 0)))
    mod = _ada(c_all, w_ada, b_ada)
    mods_p = [mod[l, :bp].reshape(bp, 1, 6 * D) for l in range(DEPTH)]
    mods_s = [jnp.repeat(mod[l, bp:n_c], t_new, axis=0).reshape(1, db * t_new, 6 * D) for l in range(DEPTH)]

    sample = {
        "db": db, "t": t_new, "page_table": page_table,
        "state_hgrn": state_hgrn, "state_gdn": state_gdn, "state_conv": state_gdn_conv,
        "cache_kt": cache_fox_k.transpose(0, 1, 3, 4, 2).reshape(DEPTH * n_pool, B_W, PAGE),
        "cache_vt": cache_fox_v.transpose(0, 1, 3, 4, 2).reshape(DEPTH * n_pool, B_W, PAGE),
        "cache_lf_t": cache_fox_logf.transpose(0, 1, 3, 2).reshape(DEPTH * n_pool, HB_H, PAGE),
    }

    y_p, leaves_p = _trunk(x_prompt, mods_p, w, None)
    y_s, leaves_s = _trunk(x_sample.reshape(1, db * t_new, D), mods_s, w, sample)
    return (y_p, y_s.reshape(db, t_new, D), *leaves_p, *leaves_s)
```
